```python
import functools
import jax, jax.numpy as jnp
from jax import lax
import numpy as np

D_MODEL = 4096
BATCH = 1
SEQ = 8192
DEPTH = 1
DEC_BATCH = 128
DEC_SEQ = 4
PAST_LEN = 2048
PAGE_SIZE = 128

HEAD_DIM = 128
N_HEADS = D_MODEL // HEAD_DIM
H_FOX = N_HEADS // 2
H_DSA = N_HEADS - H_FOX
H_DSA_KV = max(1, H_DSA // 4)
DSA_GROUP = H_DSA // H_DSA_KV
FOX_W = H_FOX * HEAD_DIM
DSA_W = H_DSA * HEAD_DIM
DSA_KV_W = H_DSA_KV * HEAD_DIM
H_IDX = 16
IDX_DIM = 64
TOPK_MAX = 256
D_FF = ((8 * D_MODEL // 3 + 255) // 256) * 256
CONV_W = 3
Q_BLOCK = 128
FORGET_BIAS = 3.0
RMS_EPS = 1e-6
SPLIT_SIZES = (FOX_W, FOX_W, FOX_W, H_FOX, DSA_W, DSA_KV_W, DSA_KV_W, H_IDX * IDX_DIM, IDX_DIM, H_IDX)
SPLIT_POINTS = tuple(sum(SPLIT_SIZES[:i + 1]) for i in range(len(SPLIT_SIZES) - 1))
D_IN = sum(SPLIT_SIZES)

kernel_name = 'hymba_fox_dsa_convffn_step'


def _rmsnorm(x, g):
    xf = x.astype(jnp.float32)
    y = xf * lax.rsqrt(jnp.mean(xf * xf, axis=-1, keepdims=True) + RMS_EPS)
    return (y * g.astype(jnp.float32)).astype(x.dtype)


def _alibi_slopes():
    return 2.0 ** (-8.0 * jnp.arange(1, H_DSA + 1, dtype=jnp.float32) / H_DSA)


def _take_rows(rows, idx):
    return jax.vmap(lambda r, i: r[i])(rows, idx)


def _project(h, w_in, b_f):
    B, T = h.shape[:2]
    z = h @ w_in
    fq, fk, fv, fg, dq, dk, dv, iq, ik, iw = jnp.split(z, SPLIT_POINTS, axis=-1)
    fq = fq.reshape(B, T, H_FOX, HEAD_DIM)
    fk = fk.reshape(B, T, H_FOX, HEAD_DIM)
    fv = fv.reshape(B, T, H_FOX, HEAD_DIM)
    logf = jax.nn.log_sigmoid(fg.astype(jnp.float32) + b_f.astype(jnp.float32))
    dq = dq.reshape(B, T, H_DSA, HEAD_DIM)
    dk = dk.reshape(B, T, H_DSA_KV, HEAD_DIM)
    dv = dv.reshape(B, T, H_DSA_KV, HEAD_DIM)
    iq = iq.reshape(B, T, H_IDX, IDX_DIM)
    return fq, fk, fv, logf, dq, dk, dv, iq, ik, iw


def _fox_attend(q, F_q, q_pos, segments):
    scores = []
    for k, _, F_k, k_pos in segments:
        s = jnp.einsum('bqhd,bkhd->bhqk', q, k, preferred_element_type=jnp.float32) * (HEAD_DIM ** -0.5)
        s = s + (jnp.swapaxes(F_q, 1, 2)[..., :, None] - jnp.swapaxes(F_k, 1, 2)[..., None, :])
        s = jnp.where(k_pos[None, :] <= q_pos[:, None], s, -jnp.inf)
        scores.append(s)
    p = jax.nn.softmax(jnp.concatenate(scores, axis=-1), axis=-1)
    out = None
    start = 0
    for k, v, _, _ in segments:
        n = k.shape[1]
        o = jnp.einsum('bhqk,bkhd->bqhd', p[..., start:start + n].astype(v.dtype), v)
        out = o if out is None else out + o
        start += n
    return out


def _indexer_topk(iq, iw, q_pos, ik, k_top):
    Tk = ik.shape[1]
    dots = jnp.einsum('bqhd,bkd->bqhk', iq, ik, preferred_element_type=jnp.float32) * (IDX_DIM ** -0.5)
    score = jnp.einsum('bqh,bqhk->bqk', iw.astype(jnp.float32) * (H_IDX ** -0.5), jax.nn.relu(dots))
    k_pos = jnp.arange(Tk)
    score = jnp.where(k_pos[None, None, :] <= q_pos[None, :, None], score, -jnp.inf)
    _, idx = lax.top_k(score, k_top)
    valid = idx <= q_pos[None, :, None]
    return idx, valid


def _sparse_attend(q, q_pos, k_sel, v_sel, sel_pos, valid):
    B, Tq = q.shape[:2]
    qg = q.reshape(B, Tq, H_DSA_KV, DSA_GROUP, HEAD_DIM)
    s = jnp.einsum('bqngd,bqknd->bqngk', qg, k_sel, preferred_element_type=jnp.float32) * (HEAD_DIM ** -0.5)
    dist = (q_pos[None, :, None] - sel_pos).astype(jnp.float32)
    slopes = _alibi_slopes().reshape(H_DSA_KV, DSA_GROUP)
    s = s - slopes[None, None, :, :, None] * dist[:, :, None, None, :]
    s = jnp.where(valid[:, :, None, None, :], s, -jnp.inf)
    p = jax.nn.softmax(s, axis=-1).astype(v_sel.dtype)
    o = jnp.einsum('bqngk,bqknd->bqngd', p, v_sel)
    return o.reshape(B, Tq, H_DSA, HEAD_DIM)


def _mixer_prompt(h, w_in, b_f):
    fq, fk, fv, logf, dq, dk, dv, iq, ik, iw = _project(h, w_in, b_f)
    B, S = h.shape[:2]
    n_blk = S // Q_BLOCK
    k_top = min(TOPK_MAX, S // 4)
    pos = jnp.arange(S)
    F = jnp.cumsum(logf, axis=1)

    def to_blocks(a):
        return jnp.swapaxes(a.reshape((B, n_blk, Q_BLOCK) + a.shape[2:]), 0, 1)

    def from_blocks(a):
        return jnp.swapaxes(a, 0, 1).reshape((B, S) + a.shape[3:])

    def block(args):
        fq_b, F_b, dq_b, iq_b, iw_b, pos_b = args
        o_f = _fox_attend(fq_b, F_b, pos_b, ((fk, fv, F, pos),))
        idx, valid = _indexer_topk(iq_b, iw_b, pos_b, ik, k_top)
        o_d = _sparse_attend(dq_b, pos_b, _take_rows(dk, idx), _take_rows(dv, idx), idx, valid)
        return o_f, o_d

    o_f, o_d = lax.map(block, (to_blocks(fq), to_blocks(F), to_blocks(dq), to_blocks(iq),
                               to_blocks(iw), pos.reshape(n_blk, Q_BLOCK)))
    return from_blocks(o_f), from_blocks(o_d), (fk, fv, logf, dk, dv, ik)


def _mixer_sample(h, page_table, cache_fox_k, cache_fox_v, cache_fox_logf, cache_dsa_k, cache_dsa_v,
                  cache_idx_k, w_in, b_f):
    fq, fk, fv, logf, dq, dk, dv, iq, ik, iw = _project(h, w_in, b_f)
    B, T = h.shape[:2]
    n_pages = PAST_LEN // PAGE_SIZE
    past = n_pages * PAGE_SIZE

    def gather_past(cache):
        g = cache[page_table]
        return g.reshape((B, past) + cache.shape[2:])

    q_pos = past + jnp.arange(T)
    past_pos = jnp.arange(past)
    F_past = jnp.cumsum(gather_past(cache_fox_logf).astype(jnp.float32), axis=1)
    F_new = F_past[:, -1:] + jnp.cumsum(logf, axis=1)
    o_f = _fox_attend(fq, F_new, q_pos, ((gather_past(cache_fox_k), gather_past(cache_fox_v), F_past, past_pos),
                                         (fk, fv, F_new, q_pos)))
    ik_all = jnp.concatenate([gather_past(cache_idx_k), ik.astype(cache_idx_k.dtype)], axis=1)
    k_top = min(TOPK_MAX, (past + T) // 4)
    idx, valid = _indexer_topk(iq, iw, q_pos, ik_all, k_top)
    in_past = idx < past
    pidx = jnp.minimum(idx, past - 1)
    phys = jax.vmap(lambda pt, i: pt[i])(page_table, pidx // PAGE_SIZE)
    off = pidx % PAGE_SIZE
    nidx = jnp.clip(idx - past, 0, T - 1)

    def select_rows(cache, new):
        past_rows = cache[phys, off]
        new_rows = _take_rows(new, nidx)
        return jnp.where(in_past[..., None, None], past_rows, new_rows.astype(past_rows.dtype))

    o_d = _sparse_attend(dq, q_pos, select_rows(cache_dsa_k, dk), select_rows(cache_dsa_v, dv), idx, valid)
    return o_f, o_d, (fk, fv, logf, dk, dv, ik)


def _layer(x, c, mixer, conv_prev, w_mod, b_mod, g_pre_attn, g_post_attn, g_fox_out, g_dsa_out, w_out,
           g_pre_ffn, g_post_ffn, w_gate, w_up, w_conv, b_conv, w_down):
    B, T = x.shape[:2]
    mod = jax.nn.silu(c) @ w_mod + b_mod
    sh1, sc1, gt1, sh2, sc2, gt2 = jnp.split(mod[:, None, :], 6, axis=-1)
    h = _rmsnorm(x, g_pre_attn) * (1 + sc1) + sh1
    o_f, o_d, kv_new = mixer(h)
    o = jnp.concatenate([_rmsnorm(o_f.reshape(B, T, FOX_W), g_fox_out),
                         _rmsnorm(o_d.reshape(B, T, DSA_W), g_dsa_out)], axis=-1) @ w_out
    x = x + gt1 * _rmsnorm(o, g_post_attn)
    h = _rmsnorm(x, g_pre_ffn) * (1 + sc2) + sh2
    g = h @ w_gate
    gp = jnp.concatenate([conv_prev.astype(g.dtype), g], axis=1)
    conv = b_conv
    for j in range(CONV_W):
        conv = conv + w_conv[j] * gp[:, j:j + T]
    f = (jax.nn.silu(conv) * (h @ w_up)) @ w_down
    x = x + gt2 * _rmsnorm(f, g_post_ffn)
    return x, kv_new, gp[:, T:]


def setup_inputs(seed: int = 0) -> dict:
    key = jax.random.key(seed)
    ks = jax.random.split(key, 28)
    f32 = jnp.float32
    n_pages = PAST_LEN // PAGE_SIZE
    n_used = DEC_BATCH * n_pages
    n_phys = n_used + max(1, n_used // 4)

    def nrm(k, shape, scale):
        return jax.random.normal(k, shape, f32) * scale

    def gain(k, n):
        return 1.0 + 0.05 * jax.random.normal(k, (DEPTH, n), f32)

    page_table = jax.random.permutation(ks[4], n_phys)[:n_used].reshape(DEC_BATCH, n_pages).astype(jnp.int32)
    return {
        'x_prompt': nrm(ks[0], (BATCH, SEQ, D_MODEL), 1.0),
        'x_sample': nrm(ks[1], (DEC_BATCH, DEC_SEQ, D_MODEL), 1.0),
        'c_prompt': nrm(ks[2], (BATCH, D_MODEL), 1.0),
        'c_sample': nrm(ks[3], (DEC_BATCH, D_MODEL), 1.0),
        'page_table': page_table,
        'cache_fox_k': nrm(ks[5], (DEPTH, n_phys, PAGE_SIZE, H_FOX, HEAD_DIM), 1.0),
        'cache_fox_v': nrm(ks[6], (DEPTH, n_phys, PAGE_SIZE, H_FOX, HEAD_DIM), 1.0),
        'cache_fox_logf': jax.nn.log_sigmoid(FORGET_BIAS + nrm(ks[7], (DEPTH, n_phys, PAGE_SIZE, H_FOX), 1.0)),
        'cache_dsa_k': nrm(ks[8], (DEPTH, n_phys, PAGE_SIZE, H_DSA_KV, HEAD_DIM), 1.0),
        'cache_dsa_v': nrm(ks[9], (DEPTH, n_phys, PAGE_SIZE, H_DSA_KV, HEAD_DIM), 1.0),
        'cache_idx_k': nrm(ks[10], (DEPTH, n_phys, PAGE_SIZE, IDX_DIM), 1.0),
        'state_conv': nrm(ks[11], (DEPTH, DEC_BATCH, CONV_W - 1, D_FF), 1.0),
        'w_in': nrm(ks[12], (DEPTH, D_MODEL, D_IN), D_MODEL ** -0.5),
        'b_f': FORGET_BIAS + nrm(ks[13], (DEPTH, H_FOX), 0.1),
        'w_out': nrm(ks[14], (DEPTH, FOX_W + DSA_W, D_MODEL), (FOX_W + DSA_W) ** -0.5),
        'g_fox_out': gain(ks[15], FOX_W),
        'g_dsa_out': gain(ks[16], DSA_W),
        'w_mod': nrm(ks[17], (DEPTH, D_MODEL, 6 * D_MODEL), 0.5 * D_MODEL ** -0.5),
        'b_mod': nrm(ks[18], (DEPTH, 6 * D_MODEL), 0.01),
        'g_pre_attn': gain(ks[19], D_MODEL),
        'g_post_attn': gain(ks[20], D_MODEL),
        'g_pre_ffn': gain(ks[21], D_MODEL),
        'g_post_ffn': gain(ks[22], D_MODEL),
        'w_gate': nrm(ks[23], (DEPTH, D_MODEL, D_FF), D_MODEL ** -0.5),
        'w_up': nrm(ks[24], (DEPTH, D_MODEL, D_FF), D_MODEL ** -0.5),
        'w_conv': nrm(ks[25], (DEPTH, CONV_W, D_FF), CONV_W ** -0.5),
        'b_conv': nrm(ks[26], (DEPTH, D_FF), 0.01),
        'w_down': nrm(ks[27], (DEPTH, D_FF, D_MODEL), D_FF ** -0.5),
    }


def reference(x_prompt, x_sample, c_prompt, c_sample, page_table, cache_fox_k, cache_fox_v, cache_fox_logf,
              cache_dsa_k, cache_dsa_v, cache_idx_k, state_conv, w_in, b_f, w_out, g_fox_out, g_dsa_out,
              w_mod, b_mod, g_pre_attn, g_post_attn, g_pre_ffn, g_post_ffn, w_gate, w_up, w_conv, b_conv, w_down):
    xp, xs = x_prompt, x_sample
    p_states, s_states = [], []
    for l in range(DEPTH):
        shared = (w_mod[l], b_mod[l], g_pre_attn[l], g_post_attn[l], g_fox_out[l], g_dsa_out[l], w_out[l],
                  g_pre_ffn[l], g_post_ffn[l], w_gate[l], w_up[l], w_conv[l], b_conv[l], w_down[l])
        mix_p = functools.partial(_mixer_prompt, w_in=w_in[l], b_f=b_f[l])
        conv0 = jnp.zeros((xp.shape[0], CONV_W - 1, D_FF), xp.dtype)
        xp, kv_p, conv_p = _layer(xp, c_prompt, mix_p, conv0, *shared)
        mix_s = functools.partial(_mixer_sample, page_table=page_table, cache_fox_k=cache_fox_k[l],
                                  cache_fox_v=cache_fox_v[l], cache_fox_logf=cache_fox_logf[l],
                                  cache_dsa_k=cache_dsa_k[l], cache_dsa_v=cache_dsa_v[l],
                                  cache_idx_k=cache_idx_k[l], w_in=w_in[l], b_f=b_f[l])
        xs, kv_s, conv_s = _layer(xs, c_sample, mix_s, state_conv[l], *shared)
        p_states.append(kv_p + (conv_p,))
        s_states.append(kv_s + (conv_s,))

    def stack(states, i):
        return jnp.stack([st[i] for st in states])

    p_fox_k, p_fox_v, p_fox_logf = stack(p_states, 0), stack(p_states, 1), stack(p_states, 2)
    p_dsa_k, p_dsa_v, p_idx_k, p_conv = stack(p_states, 3), stack(p_states, 4), stack(p_states, 5), stack(p_states, 6)
    s_fox_k, s_fox_v, s_fox_logf = stack(s_states, 0), stack(s_states, 1), stack(s_states, 2)
    s_dsa_k, s_dsa_v, s_idx_k, s_conv = stack(s_states, 3), stack(s_states, 4), stack(s_states, 5), stack(s_states, 6)
    y_prompt, y_sample = xp, xs
    return (y_prompt, y_sample, p_fox_k, p_fox_v, p_fox_logf, p_dsa_k, p_dsa_v, p_idx_k, p_conv,
            s_fox_k, s_fox_v, s_fox_logf, s_dsa_k, s_dsa_v, s_idx_k, s_conv)
```

```python
import functools

import jax
import jax.numpy as jnp
from jax import lax
from jax.experimental import pallas as pl
from jax.experimental.pallas import tpu as pltpu

HEAD_DIM = 128
H_FOX = 16
H_DSA = 16
H_DSA_KV = 4
DSA_GROUP = H_DSA // H_DSA_KV
H_IDX = 16
IDX_DIM = 64
TOPK_MAX = 256
CONV_W = 3
PAGE_SIZE = 128
RMS_EPS = 1e-6
FOX_W = H_FOX * HEAD_DIM
DSA_W = H_DSA * HEAD_DIM
DSA_KV_W = H_DSA_KV * HEAD_DIM
IQ_W = H_IDX * IDX_DIM

Z_FQ = 0
Z_FK = Z_FQ + FOX_W
Z_FV = Z_FK + FOX_W
Z_DQ = Z_FV + FOX_W
Z_DK = Z_DQ + DSA_W
Z_DV = Z_DK + DSA_KV_W
Z_IQ = Z_DV + DSA_KV_W
Z_SM = Z_IQ + IQ_W
SM_IK = 0
SM_FG = IDX_DIM
SM_IW = SM_FG + H_FOX
Z_USED = Z_SM + 128
Z_TN = 768
Z_W = -(-Z_USED // Z_TN) * Z_TN

NEG = -1e30
INT_MIN = -2 ** 31
MIB = 1024 * 1024
BF16 = jnp.bfloat16
F32 = jnp.float32

_NT = (((1,), (1,)), ((), ()))


def _cparams(sem, vmem_mib=48):
    return pltpu.CompilerParams(dimension_semantics=sem, vmem_limit_bytes=vmem_mib * MIB)


def _rms(x, g):
    return x * lax.rsqrt(jnp.mean(x * x, axis=-1, keepdims=True) + RMS_EPS) * g


def _sigmoid(x):
    return 1.0 / (1.0 + jnp.exp(-x))


def _sort_key(x):
    b = pltpu.bitcast(x, jnp.int32)
    return b ^ ((b >> 31) & jnp.int32(0x7FFFFFFF))


def _row_spec(arr, tm):
    d = arr.shape[1]
    if arr.shape[0] == 1:
        return pl.BlockSpec((1, d), lambda i: (0, 0))
    return pl.BlockSpec((tm, d), lambda i: (i, 0))


def _mod_kernel(c_ref, w_ref, b_ref, o_ref):
    c = c_ref[...]
    a = (c * _sigmoid(c)).astype(BF16)
    o_ref[...] = jnp.dot(a, w_ref[...].astype(BF16), preferred_element_type=F32) + b_ref[...]


def _modulation(c, w_mod, b_mod):
    r, d = c.shape
    n = w_mod.shape[1]
    tn = 512
    return pl.pallas_call(
        _mod_kernel,
        grid=(n // tn,),
        in_specs=[pl.BlockSpec((r, d), lambda j: (0, 0)),
                  pl.BlockSpec((d, tn), lambda j: (0, j)),
                  pl.BlockSpec((1, tn), lambda j: (0, j))],
        out_specs=pl.BlockSpec((r, tn), lambda j: (0, j)),
        out_shape=jax.ShapeDtypeStruct((r, n), F32),
        compiler_params=_cparams(("parallel",)),
        name="modulation",
    )(c, w_mod, b_mod.reshape(1, n))


def _prenorm_kernel(x_ref, g_ref, sc_ref, sh_ref, o_ref):
    y = _rms(x_ref[...], g_ref[...])
    o_ref[...] = (y * (1.0 + sc_ref[...]) + sh_ref[...]).astype(o_ref.dtype)


def _prenorm(x, g, sc, sh, tm=256):
    m, d = x.shape
    tm = min(tm, m)
    return pl.pallas_call(
        _prenorm_kernel,
        grid=(m // tm,),
        in_specs=[pl.BlockSpec((tm, d), lambda i: (i, 0)), _row_spec(g, tm), _row_spec(sc, tm), _row_spec(sh, tm)],
        out_specs=pl.BlockSpec((tm, d), lambda i: (i, 0)),
        out_shape=jax.ShapeDtypeStruct((m, d), BF16),
        compiler_params=_cparams(("parallel",)),
        name="prenorm",
    )(x, g, sc, sh)


def _attn_norm_kernel(of_ref, od_ref, gf_ref, gd_ref, o_ref):
    o_ref[:, :FOX_W] = _rms(of_ref[...], gf_ref[...]).astype(o_ref.dtype)
    o_ref[:, FOX_W:] = _rms(od_ref[...], gd_ref[...]).astype(o_ref.dtype)


def _attn_norm(o_f, o_d, g_f, g_d, tm=256):
    m = o_f.shape[0]
    tm = min(tm, m)
    return pl.pallas_call(
        _attn_norm_kernel,
        grid=(m // tm,),
        in_specs=[pl.BlockSpec((tm, FOX_W), lambda i: (i, 0)), pl.BlockSpec((tm, DSA_W), lambda i: (i, 0)),
                  _row_spec(g_f, tm), _row_spec(g_d, tm)],
        out_specs=pl.BlockSpec((tm, FOX_W + DSA_W), lambda i: (i, 0)),
        out_shape=jax.ShapeDtypeStruct((m, FOX_W + DSA_W), BF16),
        compiler_params=_cparams(("parallel",)),
        name="attn_norm",
    )(o_f, o_d, g_f, g_d)


def _res_pre_kernel(o_ref, x_ref, gt_ref, gpost_ref, gpre_ref, sc_ref, sh_ref, x1_ref, h2_ref):
    x1 = x_ref[...] + gt_ref[...] * _rms(o_ref[...], gpost_ref[...])
    x1_ref[...] = x1
    h2_ref[...] = (_rms(x1, gpre_ref[...]) * (1.0 + sc_ref[...]) + sh_ref[...]).astype(h2_ref.dtype)


def _res_pre(o, x, gt, g_post, g_pre, sc, sh, tm=128):
    m, d = x.shape
    tm = min(tm, m)
    blk =pl.BlockSpec((tm, d), lambda i: (i, 0))
    return pl.pallas_call(
        _res_pre_kernel,
        grid=(m // tm,),
        in_specs=[blk, blk, _row_spec(gt, tm), _row_spec(g_post, tm), _row_spec(g_pre, tm),
                  _row_spec(sc, tm), _row_spec(sh, tm)],
        out_specs=[blk, blk],
        out_shape=[jax.ShapeDtypeStruct((m, d), F32), jax.ShapeDtypeStruct((m, d), BF16)],
        compiler_params=_cparams(("parallel",)),
        name="res_pre",
    )(o, x, gt, g_post, g_pre, sc, sh)


def _res_kernel(f_ref, x_ref, gt_ref, gpost_ref, y_ref):
    y_ref[...] = x_ref[...] + gt_ref[...] * _rms(f_ref[...], gpost_ref[...])


def _res(f, x, gt, g_post, tm=256):
    m, d = x.shape
    tm = min(tm, m)
    blk =pl.BlockSpec((tm, d), lambda i: (i, 0))
    return pl.pallas_call(
        _res_kernel,
        grid=(m // tm,),
        in_specs=[blk, blk, _row_spec(gt, tm), _row_spec(g_post, tm)],
        out_specs=blk,
        out_shape=jax.ShapeDtypeStruct((m, d), F32),
        compiler_params=_cparams(("parallel",)),
        name="res",
    )(f, x, gt, g_post)


def _mm_kernel(a_ref, w_ref, o_ref):
    o_ref[...] = jnp.dot(a_ref[...], w_ref[...], preferred_element_type=F32)


def _matmul(a, w, tm, tn, name):
    m, k = a.shape
    n = w.shape[1]
    tm = min(tm, m)
    return pl.pallas_call(
        _mm_kernel,
        grid=(m // tm, n // tn),
        in_specs=[pl.BlockSpec((tm, k), lambda i, j: (i, 0)), pl.BlockSpec((k, tn), lambda i, j: (0, j))],
        out_specs=pl.BlockSpec((tm, tn), lambda i, j: (i, j)),
        out_shape=jax.ShapeDtypeStruct((m, n), F32),
        compiler_params=_cparams(("parallel", "parallel")),
        name=name,
    )(a, w)


def _mmk_kernel(a_ref, w_ref, o_ref):
    @pl.when(pl.program_id(2) == 0)
    def _():
        o_ref[...] = jnp.zeros_like(o_ref)

    o_ref[...] += jnp.dot(a_ref[...], w_ref[...], preferred_element_type=F32)


def _matmul_ksplit(a, w, tm, tn, tk, name):
    m, k = a.shape
    n = w.shape[1]
    tm = min(tm, m)
    return pl.pallas_call(
        _mmk_kernel,
        grid=(m // tm, n // tn, k // tk),
        in_specs=[pl.BlockSpec((tm, tk), lambda i, j, kk: (i, kk)), pl.BlockSpec((tk, tn), lambda i, j, kk: (kk, j))],
        out_specs=pl.BlockSpec((tm, tn), lambda i, j, kk: (i, j)),
        out_shape=jax.ShapeDtypeStruct((m, n), F32),
        compiler_params=_cparams(("parallel", "parallel", "arbitrary")),
        name=name,
    )(a, w)


def _logf_kernel(zs_ref, bf_ref, logf_ref, cum_ref, carry_ref, *, tm):
    @pl.when(pl.program_id(0) == 0)
    def _():
        carry_ref[...] = jnp.zeros_like(carry_ref)

    x = zs_ref[:, SM_FG:SM_FG + H_FOX] + bf_ref[...]
    lf = jnp.minimum(x, 0.0) - jnp.log1p(jnp.exp(-jnp.abs(x)))
    logf_ref[...] = lf
    row = lax.broadcasted_iota(jnp.int32, (tm, tm), 0)
    col = lax.broadcasted_iota(jnp.int32, (tm, tm), 1)
    tri = (col <= row).astype(F32)
    cum = jnp.dot(tri, lf, preferred_element_type=F32, precision=lax.Precision.HIGHEST) + carry_ref[...]
    cum_ref[...] = cum
    carry_ref[...] = cum[tm - 1:tm, :]


def _logf_cumsum(z, b_f, tm=256):
    m = z.shape[0]
    tm = min(tm, m)
    out = jax.ShapeDtypeStruct((m, H_FOX), F32)
    return pl.pallas_call(
        functools.partial(_logf_kernel, tm=tm),
        grid=(m // tm,),
        in_specs=[pl.BlockSpec((tm, 128), lambda i: (i, Z_SM // 128)), pl.BlockSpec((1, H_FOX), lambda i: (0, 0))],
        out_specs=[pl.BlockSpec((tm, H_FOX), lambda i: (i, 0)), pl.BlockSpec((tm, H_FOX), lambda i: (i, 0))],
        out_shape=[out, out],
        scratch_shapes=[pltpu.VMEM((1, H_FOX), F32)],
        compiler_params=_cparams(("arbitrary",)),
        name="logf_cumsum",
    )(z, b_f.reshape(1, H_FOX))


def _fox_kernel(q_ref, k_ref, v_ref, fk_ref, o_ref, qs_ref, m_ref, l_ref, acc_ref, *, t):
    qi = pl.program_id(1)
    ki = pl.program_id(2)

    @pl.when(ki == 0)
    def _():
        qs_ref[...] = (q_ref[...] * (HEAD_DIM ** -0.5)).astype(BF16)
        m_ref[...] = jnp.full_like(m_ref, NEG)
        l_ref[...] = jnp.zeros_like(l_ref)
        acc_ref[...] = jnp.zeros_like(acc_ref)

    def step(diagonal):
        k = k_ref[...].astype(BF16)
        v = v_ref[...].astype(BF16)
        s = lax.dot_general(qs_ref[...], k, _NT, preferred_element_type=F32) - fk_ref[...]
        if diagonal:
            row = lax.broadcasted_iota(jnp.int32, (t, t), 0)
            col = lax.broadcasted_iota(jnp.int32, (t, t), 1)
            s = jnp.where(col <= row, s, NEG)
        m_prev = m_ref[...]
        m_new = jnp.maximum(m_prev, jnp.max(s, axis=-1, keepdims=True))
        alpha = jnp.exp(m_prev - m_new)
        p = jnp.exp(s - m_new)
        l_ref[...] = alpha * l_ref[...] + jnp.sum(p, axis=-1, keepdims=True)
        acc_ref[...] = alpha * acc_ref[...] + jnp.dot(p.astype(BF16), v, preferred_element_type=F32)
        m_ref[...] = m_new

    @pl.when(ki < qi)
    def _():
        step(False)

    @pl.when(ki == qi)
    def _():
        step(True)

    @pl.when(ki == pl.num_programs(2) - 1)
    def _():
        o_ref[...] = acc_ref[...] / l_ref[...]


def _fox_prompt(z, cum_t, t=512):
    s_len = z.shape[0]
    n = s_len // t
    return pl.pallas_call(
        functools.partial(_fox_kernel, t=t),
        grid=(H_FOX, n, n),
        in_specs=[pl.BlockSpec((t, HEAD_DIM), lambda h, qi, ki: (qi, Z_FQ // HEAD_DIM + h)),
                  pl.BlockSpec((t, HEAD_DIM), lambda h, qi, ki: (jnp.minimum(ki, qi), Z_FK // HEAD_DIM + h)),
                  pl.BlockSpec((t, HEAD_DIM), lambda h, qi, ki: (jnp.minimum(ki, qi), Z_FV // HEAD_DIM + h)),
                  pl.BlockSpec((None, 1, t), lambda h, qi, ki: (h, 0, jnp.minimum(ki, qi)))],
        out_specs=pl.BlockSpec((t, HEAD_DIM), lambda h, qi, ki: (qi, h)),
        out_shape=jax.ShapeDtypeStruct((s_len, FOX_W), F32),
        scratch_shapes=[pltpu.VMEM((t, HEAD_DIM), BF16), pltpu.VMEM((t, 1), F32), pltpu.VMEM((t, 1), F32),
                        pltpu.VMEM((t, HEAD_DIM), F32)],
        compiler_params=_cparams(("parallel", "parallel", "arbitrary")),
        name="fox_prompt",
    )(z, z, z, cum_t)


def _bisect_threshold(count_ge, shape, k_top):
    def body(it, thr):
        cand = thr + lax.shift_left(jnp.int32(1), jnp.int32(31) - it)
        return jnp.where(count_ge(cand) >= k_top, cand, thr)

    return lax.fori_loop(0, 32, body, jnp.full(shape, INT_MIN, jnp.int32))


def _idx_kernel(iq_ref, sm_ref, ik_ref, bias_ref, ikb_ref, iqs_ref, key_ref, *, tq, tkc, k_top):
    i = pl.program_id(0)
    s_len = key_ref.shape[1]

    @pl.when(i == 0)
    def _():
        ikb_ref[...] = ik_ref[:, SM_IK:SM_IK + IDX_DIM].astype(BF16)

    w = sm_ref[:, SM_IW:SM_IW + H_IDX] * (H_IDX ** -0.5)
    wabs = jnp.abs(w) * (IDX_DIM ** -0.5)
    sgn = jnp.where(w > 0, 1.0, -1.0)
    for h in range(H_IDX):
        iqs_ref[h] = (iq_ref[:, h * IDX_DIM:(h + 1) * IDX_DIM] * wabs[:, h:h + 1]).astype(BF16)

    n_chunks = ((i + 1) * tq + tkc - 1) // tkc
    qpos = i * tq + lax.broadcasted_iota(jnp.int32, (tq, tkc), 0)

    def score_chunk(c, carry):
        off = pl.multiple_of(c * tkc, tkc)
        kc = ikb_ref[pl.ds(off, tkc), :]
        acc = jnp.zeros((tq, tkc), F32)
        for h in range(H_IDX):
            d = lax.dot_general(iqs_ref[h], kc, _NT, preferred_element_type=F32)
            acc = acc + sgn[:, h:h + 1] * jnp.maximum(d, 0.0)
        kpos = off + lax.broadcasted_iota(jnp.int32, (tq, tkc), 1)
        key_ref[:, pl.ds(off, tkc)] = jnp.where(kpos <= qpos, _sort_key(acc), INT_MIN)
        return carry

    lax.fori_loop(0, n_chunks, score_chunk, 0)

    def count_ge(cand):
        def body(c, cnt):
            off = pl.multiple_of(c * tkc, tkc)
            hit = jnp.where(key_ref[:, pl.ds(off, tkc)] >= cand, 1, 0)
            for u in range(tkc // 128):
                cnt = cnt + hit[:, u * 128:(u + 1) * 128]
            return cnt

        cnt = lax.fori_loop(0, n_chunks, body, jnp.zeros((tq, 128), jnp.int32))
        return jnp.sum(cnt, axis=-1, keepdims=True)

    thr = _bisect_threshold(count_ge, (tq, 1), k_top)
    thr = jnp.maximum(thr, INT_MIN + 1)

    def write_chunk(c, carry):
        off = pl.multiple_of(c * tkc, tkc)
        bias_ref[:, pl.ds(off, tkc)] = jnp.where(key_ref[:, pl.ds(off, tkc)] >= thr, 0.0, NEG)
        return carry

    lax.fori_loop(0, n_chunks, write_chunk, 0)

    def fill_chunk(c, carry):
        off = pl.multiple_of(c * tkc, tkc)
        bias_ref[:, pl.ds(off, tkc)] = jnp.full((tq, tkc), NEG, F32)
        return carry

    lax.fori_loop(n_chunks, s_len // tkc, fill_chunk, 0)


def _idx_prompt(z, k_top, tq=128, tkc=256):
    s_len = z.shape[0]
    return pl.pallas_call(
        functools.partial(_idx_kernel, tq=tq, tkc=tkc, k_top=k_top),
        grid=(s_len // tq,),
        in_specs=[pl.BlockSpec((tq, IQ_W), lambda i: (i, Z_IQ // IQ_W)),
                  pl.BlockSpec((tq, 128), lambda i: (i, Z_SM // 128)),
                  pl.BlockSpec((s_len, 128), lambda i: (0, Z_SM // 128))],
        out_specs=pl.BlockSpec((tq, s_len), lambda i: (i, 0)),
        out_shape=jax.ShapeDtypeStruct((s_len, s_len), F32),
        scratch_shapes=[pltpu.VMEM((s_len, IDX_DIM), BF16), pltpu.VMEM((H_IDX, tq, IDX_DIM), BF16),
                        pltpu.VMEM((tq, s_len), jnp.int32)],
        compiler_params=_cparams(("arbitrary",), 56),
        name="idx_prompt",
    )(z, z, z)


def _alibi_slope(h):
    return 2.0 ** (-8.0 * (h + 1) / H_DSA)


def _dsa_kernel(q_ref, k_ref, v_ref, bias_ref, o_ref, qs_ref, m_ref, l_ref, acc_ref, *, tq, tk):
    qi = pl.program_id(0)
    ki = pl.program_id(1)
    ki_last = (qi * tq + tq - 1) // tk

    @pl.when(ki == 0)
    def _():
        for h in range(H_DSA):
            g, j = divmod(h, DSA_GROUP)
            qs_ref[g, j * tq:(j + 1) * tq, :] = (
                q_ref[:, h * HEAD_DIM:(h + 1) * HEAD_DIM] * (HEAD_DIM ** -0.5)).astype(BF16)
        m_ref[...] = jnp.full_like(m_ref, NEG)
        l_ref[...] = jnp.zeros_like(l_ref)
        acc_ref[...] = jnp.zeros_like(acc_ref)

    @pl.when(ki <= ki_last)
    def _():
        bias = bias_ref[...]
        rel = (ki * tk - qi * tq + lax.broadcasted_iota(jnp.int32, (1, tk), 1)).astype(F32)
        for g in range(H_DSA_KV):
            kg = k_ref[:, g * HEAD_DIM:(g + 1) * HEAD_DIM].astype(BF16)
            vg = v_ref[:, g * HEAD_DIM:(g + 1) * HEAD_DIM].astype(BF16)
            s = lax.dot_general(qs_ref[g], kg, _NT, preferred_element_type=F32)
            s = jnp.concatenate(
                [s[j * tq:(j + 1) * tq] + (bias + _alibi_slope(g * DSA_GROUP + j) * rel) for j in range(DSA_GROUP)],
                axis=0)
            m_prev = m_ref[g]
            m_new = jnp.maximum(m_prev, jnp.max(s, axis=-1, keepdims=True))
            alpha = jnp.exp(m_prev - m_new)
            p = jnp.exp(s - m_new)
            l_ref[g] = alpha * l_ref[g] + jnp.sum(p, axis=-1, keepdims=True)
            acc_ref[g] = alpha * acc_ref[g] + jnp.dot(p.astype(BF16), vg, preferred_element_type=F32)
            m_ref[g] = m_new

    @pl.when(ki == pl.num_programs(1) - 1)
    def _():
        for h in range(H_DSA):
            g, j = divmod(h, DSA_GROUP)
            o_ref[:, h * HEAD_DIM:(h + 1) * HEAD_DIM] = (
                acc_ref[g, j * tq:(j + 1) * tq, :] / l_ref[g, j * tq:(j + 1) * tq, :])


def _dsa_prompt(z, bias, tq=256, tk=512):
    s_len = z.shape[0]

    def kv_blk(qi, ki):
        return jnp.minimum(ki, (qi * tq + tq - 1) // tk)

    return pl.pallas_call(
        functools.partial(_dsa_kernel, tq=tq, tk=tk),
        grid=(s_len // tq, s_len // tk),
        in_specs=[pl.BlockSpec((tq, DSA_W), lambda qi, ki: (qi, Z_DQ // DSA_W)),
                  pl.BlockSpec((tk, DSA_KV_W), lambda qi, ki: (kv_blk(qi, ki), Z_DK // DSA_KV_W)),
                  pl.BlockSpec((tk, DSA_KV_W), lambda qi, ki: (kv_blk(qi, ki), Z_DV // DSA_KV_W)),
                  pl.BlockSpec((tq, tk), lambda qi, ki: (qi, kv_blk(qi, ki)))],
        out_specs=pl.BlockSpec((tq, DSA_W), lambda qi, ki: (qi, 0)),
        out_shape=jax.ShapeDtypeStruct((s_len, DSA_W), F32),
        scratch_shapes=[pltpu.VMEM((H_DSA_KV, DSA_GROUP * tq, HEAD_DIM), BF16),
                        pltpu.VMEM((H_DSA_KV, DSA_GROUP * tq, 1), F32),
                        pltpu.VMEM((H_DSA_KV, DSA_GROUP * tq, 1), F32),
                        pltpu.VMEM((H_DSA_KV, DSA_GROUP * tq, HEAD_DIM), F32)],
        compiler_params=_cparams(("parallel", "arbitrary")),
        name="dsa_prompt",
    )(z, z, z, bias)


def _dec_idx_kernel(pt_ref, *refs, n_pages, t_new, k_top):
    del pt_ref
    ik_refs = refs[:n_pages]
    lf_refs = refs[n_pages:2 * n_pages]
    iq_ref, w_ref, ikn_ref, lfn_ref, dbias_ref, fneg_ref, key_ref = refs[2 * n_pages:]

    row = lax.broadcasted_iota(jnp.int32, (PAGE_SIZE, PAGE_SIZE), 0)
    col = lax.broadcasted_iota(jnp.int32, (PAGE_SIZE, PAGE_SIZE), 1)
    tri = (row <= col).astype(F32)
    carry = jnp.zeros((H_FOX, 1), F32)
    for p in range(n_pages + 1):
        lf = lf_refs[p][...] if p < n_pages else lfn_ref[...]
        cum = jnp.dot(lf, tri, preferred_element_type=F32, precision=lax.Precision.HIGHEST) + carry
        fneg_ref[:, p * PAGE_SIZE:(p + 1) * PAGE_SIZE] = -cum
        carry = cum[:, PAGE_SIZE - 1:PAGE_SIZE]

    w = w_ref[...] * (H_IDX ** -0.5)
    iqs = (iq_ref[...] * (jnp.abs(w) * (IDX_DIM ** -0.5))).astype(BF16)
    sgn = jnp.where(w > 0, 1.0, -1.0)
    qrow = lax.broadcasted_iota(jnp.int32, (16, PAGE_SIZE), 0) % t_new
    lane = lax.broadcasted_iota(jnp.int32, (16, PAGE_SIZE), 1)
    for p in range(n_pages + 1):
        if p < n_pages:
            kp = ik_refs[p][...].astype(BF16)
        else:
            kp = jnp.concatenate([ikn_ref[...], jnp.zeros((PAGE_SIZE - 8, IDX_DIM), F32)], axis=0).astype(BF16)
        d = lax.dot_general(iqs, kp, _NT, preferred_element_type=F32)
        sc = jnp.sum((sgn * jnp.maximum(d, 0.0)).reshape(H_IDX, 16, PAGE_SIZE), axis=0)
        key = _sort_key(sc)
        if p == n_pages:
            key = jnp.where((lane <= qrow) & (lane < t_new), key, INT_MIN)
        key_ref[:, p * PAGE_SIZE:(p + 1) * PAGE_SIZE] = key

    def count_ge(cand):
        return jnp.sum(jnp.where(key_ref[...] >= cand, 1, 0), axis=-1, keepdims=True)

    thr = jnp.maximum(_bisect_threshold(count_ge, (16, 1), k_top), INT_MIN + 1)
    dbias_ref[...] = jnp.where(key_ref[...] >= thr, 0.0, NEG)


def _dec_idx(page_table, idx_pages, logf_pages_t, iq16, w16, ik_new8, logf_new_t, k_top, t_new):
    b, n_pages = page_table.shape
    width = (n_pages + 1) * PAGE_SIZE
    page = lambda p: (lambda bi, pt: (pt[bi, p], 0, 0))
    per_b = lambda bi, pt: (bi, 0, 0)
    in_specs = ([pl.BlockSpec((None, PAGE_SIZE, IDX_DIM), page(p)) for p in range(n_pages)]
                + [pl.BlockSpec((None, H_FOX, PAGE_SIZE), page(p)) for p in range(n_pages)]
                + [pl.BlockSpec((None, H_IDX * 16, IDX_DIM), per_b), pl.BlockSpec((None, H_IDX * 16, 1), per_b),
                   pl.BlockSpec((None, 8, IDX_DIM), per_b), pl.BlockSpec((None, H_FOX, PAGE_SIZE), per_b)])
    return pl.pallas_call(
        functools.partial(_dec_idx_kernel, n_pages=n_pages, t_new=t_new, k_top=k_top),
        grid_spec=pltpu.PrefetchScalarGridSpec(
            num_scalar_prefetch=1,
            grid=(b,),
            in_specs=in_specs,
            out_specs=[pl.BlockSpec((None, 16, width), per_b), pl.BlockSpec((None, H_FOX, width), per_b)],
            scratch_shapes=[pltpu.VMEM((16, width), jnp.int32)]),
        out_shape=[jax.ShapeDtypeStruct((b, 16, width), F32), jax.ShapeDtypeStruct((b, H_FOX, width), F32)],
        compiler_params=_cparams(("arbitrary",)),
        name="dec_idx",
    )(page_table, *([idx_pages] * n_pages), *([logf_pages_t] * n_pages), iq16, w16, ik_new8, logf_new_t)


def _dec_attn_kernel(pt_ref, fk_ref, fv_ref, dk_ref, dv_ref, fkn_ref, fvn_ref, dkn_ref, dvn_ref, qf_ref, qd_ref,
                     fneg_ref, dbias_ref, slope_ref, of_ref, od_ref,
                     mf_ref, lf_ref, accf_ref, md_ref, ld_ref, accd_ref, *, n_pages, t_new):
    del pt_ref
    p_id = pl.program_id(1)

    @pl.when(p_id == 0)
    def _():
        mf_ref[...] = jnp.full_like(mf_ref, NEG)
        lf_ref[...] = jnp.zeros_like(lf_ref)
        accf_ref[...] = jnp.zeros_like(accf_ref)
        md_ref[...] = jnp.full_like(md_ref, NEG)
        ld_ref[...] = jnp.zeros_like(ld_ref)
        accd_ref[...] = jnp.zeros_like(accd_ref)

    def online(s, m_ref, l_ref):
        m_prev = m_ref[...]
        m_new = jnp.maximum(m_prev, jnp.max(s, axis=-1, keepdims=True))
        alpha = jnp.exp(m_prev - m_new)
        p = jnp.exp(s - m_new)
        l_ref[...] = alpha * l_ref[...] + jnp.sum(p, axis=-1, keepdims=True)
        m_ref[...] = m_new
        return alpha, p.astype(BF16)

    def process(kf, vf, kd, vd, new):
        qf = (qf_ref[...] * (HEAD_DIM ** -0.5)).astype(BF16)
        s = jnp.concatenate(
            [lax.dot_general(qf[16 * h:16 * h + 16], kf[:, h * HEAD_DIM:(h + 1) * HEAD_DIM], _NT,
                             preferred_element_type=F32) for h in range(H_FOX)], axis=0)
        fb = fneg_ref[...]
        s = s + jnp.broadcast_to(fb[:, None, :], (H_FOX, 16, PAGE_SIZE)).reshape(H_FOX * 16, PAGE_SIZE)
        if new:
            qrow = lax.broadcasted_iota(jnp.int32, (H_FOX * 16, PAGE_SIZE), 0) % 16
            lane = lax.broadcasted_iota(jnp.int32, (H_FOX * 16, PAGE_SIZE), 1)
            s = jnp.where((lane <= qrow) & (lane < t_new), s, NEG)
        alpha, p = online(s, mf_ref, lf_ref)
        pv = jnp.concatenate(
            [jnp.dot(p[16 * h:16 * h + 16], vf[:, h * HEAD_DIM:(h + 1) * HEAD_DIM], preferred_element_type=F32)
             for h in range(H_FOX)], axis=0)
        accf_ref[...] = alpha * accf_ref[...] + pv

        rows = DSA_GROUP * t_new
        qd = (qd_ref[...] * (HEAD_DIM ** -0.5)).astype(BF16)
        s = jnp.concatenate(
            [lax.dot_general(qd[rows * g:rows * (g + 1)], kd[:, g * HEAD_DIM:(g + 1) * HEAD_DIM], _NT,
                             preferred_element_type=F32) for g in range(H_DSA_KV)], axis=0)
        rel = ((p_id - n_pages) * PAGE_SIZE + lax.broadcasted_iota(jnp.int32, (1, PAGE_SIZE), 1)).astype(F32)
        bias = jnp.concatenate([dbias_ref[...]] * H_DSA_KV, axis=0)
        s = s + bias + slope_ref[...] * rel
        alpha, p = online(s, md_ref, ld_ref)
        pv = jnp.concatenate(
            [jnp.dot(p[rows * g:rows * (g + 1)], vd[:, g * HEAD_DIM:(g + 1) * HEAD_DIM], preferred_element_type=F32)
             for g in range(H_DSA_KV)], axis=0)
        accd_ref[...] = alpha * accd_ref[...] + pv

    @pl.when(p_id < n_pages)
    def _():
        process(fk_ref[...].astype(BF16), fv_ref[...].astype(BF16), dk_ref[...].astype(BF16),
                dv_ref[...].astype(BF16), False)

    @pl.when(p_id == n_pages)
    def _():
        def pad(ref):
            x = ref[...]
            return jnp.concatenate([x, jnp.zeros((PAGE_SIZE - x.shape[0], x.shape[1]), F32)], axis=0).astype(BF16)

        process(pad(fkn_ref), pad(fvn_ref), pad(dkn_ref), pad(dvn_ref), True)
        of = accf_ref[...] / lf_ref[...]
        for h in range(H_FOX):
            of_ref[:, h * HEAD_DIM:(h + 1) * HEAD_DIM] = of[16 * h:16 * h + t_new]
        od = accd_ref[...] / ld_ref[...]
        for h in range(H_DSA):
            od_ref[:, h * HEAD_DIM:(h + 1) * HEAD_DIM] = od[t_new * h:t_new * (h + 1)]


def _dec_attn(page_table, fox_k, fox_v, dsa_k, dsa_v, fk_new8, fv_new8, dk_new8, dv_new8, qf16, qd, fneg, dbias,
              slopes, t_new):
    b, n_pages = page_table.shape

    def page(bi, p, pt):
        return (pt[bi, jnp.minimum(p, n_pages - 1)], 0, 0)

    per_b = lambda bi, p, pt: (bi, 0, 0)
    per_bp = lambda bi, p, pt: (bi, 0, p)
    rows_d = H_DSA * t_new
    in_specs = [pl.BlockSpec((None, PAGE_SIZE, FOX_W), page), pl.BlockSpec((None, PAGE_SIZE, FOX_W), page),
                pl.BlockSpec((None, PAGE_SIZE, DSA_KV_W), page), pl.BlockSpec((None, PAGE_SIZE, DSA_KV_W), page),
                pl.BlockSpec((None, 8, FOX_W), per_b), pl.BlockSpec((None, 8, FOX_W), per_b),
                pl.BlockSpec((None, 8, DSA_KV_W), per_b), pl.BlockSpec((None, 8, DSA_KV_W), per_b),
                pl.BlockSpec((None, H_FOX * 16, HEAD_DIM), per_b), pl.BlockSpec((None, rows_d, HEAD_DIM), per_b),
                pl.BlockSpec((None, H_FOX, PAGE_SIZE), per_bp), pl.BlockSpec((None, 16, PAGE_SIZE), per_bp),
                pl.BlockSpec((rows_d, 1), lambda bi, p, pt: (0, 0))]
    out = jax.ShapeDtypeStruct((b, t_new, FOX_W), F32)
    return pl.pallas_call(
        functools.partial(_dec_attn_kernel, n_pages=n_pages, t_new=t_new),
        grid_spec=pltpu.PrefetchScalarGridSpec(
            num_scalar_prefetch=1,
            grid=(b, n_pages + 1),
            in_specs=in_specs,
            out_specs=[pl.BlockSpec((None, t_new, FOX_W), per_b), pl.BlockSpec((None, t_new, DSA_W), per_b)],
            scratch_shapes=[pltpu.VMEM((H_FOX * 16, 1), F32), pltpu.VMEM((H_FOX * 16, 1), F32),
                            pltpu.VMEM((H_FOX * 16, HEAD_DIM), F32),
                            pltpu.VMEM((rows_d, 1), F32), pltpu.VMEM((rows_d, 1), F32),
                            pltpu.VMEM((rows_d, HEAD_DIM), F32)]),
        out_shape=[out, out],
        compiler_params=_cparams(("parallel", "arbitrary")),
        name="dec_attn",
    )(page_table, fox_k, fox_v, dsa_k, dsa_v, fk_new8, fv_new8, dk_new8, dv_new8, qf16, qd, fneg, dbias, slopes)


def _ffn_up_kernel(h_ref, hprev_ref, wg_ref, wu_ref, wc_ref, bc_ref, a_ref, tail_ref, *, tm):
    i = pl.program_id(0)
    h = h_ref[...]
    g = jnp.dot(h, wg_ref[...], preferred_element_type=F32)
    u = jnp.dot(h, wu_ref[...], preferred_element_type=F32)
    gh = jnp.dot(hprev_ref[...], wg_ref[...], preferred_element_type=F32)
    gh = jnp.where(i > 0, gh, 0.0)
    ext = jnp.concatenate([gh, g], axis=0)
    g1 = pltpu.roll(ext, 1, 0)[16:]
    g2 = pltpu.roll(ext, 2, 0)[16:]
    wc = wc_ref[...]
    conv = bc_ref[...] + wc[0:1] * g2 + wc[1:2] * g1 + wc[2:3] * g
    a_ref[...] = (conv * _sigmoid(conv) * u).astype(a_ref.dtype)
    tail_ref[...] = g[tm - 8:]


def _ffn_up_prompt(h2, w_gate, w_up, w_conv, b_conv, tm=1024, tn=256):
    m, d = h2.shape
    d_ff = w_gate.shape[1]
    tm = min(tm, m)
    return pl.pallas_call(
        functools.partial(_ffn_up_kernel, tm=tm),
        grid=(m // tm, d_ff // tn),
        in_specs=[pl.BlockSpec((tm, d), lambda i, j: (i, 0)),
                  pl.BlockSpec((16, d), lambda i, j: (jnp.maximum(i * (tm // 16) - 1, 0), 0)),
                  pl.BlockSpec((d, tn), lambda i, j: (0, j)), pl.BlockSpec((d, tn), lambda i, j: (0, j)),
                  pl.BlockSpec((CONV_W, tn), lambda i, j: (0, j)), pl.BlockSpec((1, tn), lambda i, j: (0, j))],
        out_specs=[pl.BlockSpec((tm, tn), lambda i, j: (i, j)), pl.BlockSpec((8, tn), lambda i, j: (i, j))],
        out_shape=[jax.ShapeDtypeStruct((m, d_ff), BF16), jax.ShapeDtypeStruct((m // tm * 8, d_ff), F32)],
        compiler_params=_cparams(("parallel", "parallel")),
        name="ffn_up_prompt",
    )(h2, h2, w_gate, w_up, w_conv, b_conv.reshape(1, d_ff))


def _ffn_up_dec_kernel(h_ref, wg_ref, wu_ref, wc_ref, bc_ref, s0_ref, s1_ref, a_ref, g_ref, *, t_new):
    h = h_ref[...]
    g = jnp.dot(h, wg_ref[...], preferred_element_type=F32)
    u = jnp.dot(h, wu_ref[...], preferred_element_type=F32)
    g_ref[...] = g
    t = lax.broadcasted_iota(jnp.int32, g.shape, 0) % t_new
    g1 = jnp.where(t >= 1, pltpu.roll(g, 1, 0), 0.0) + s1_ref[...]
    g2 = jnp.where(t >= 2, pltpu.roll(g, 2, 0), 0.0) + s0_ref[...]
    wc = wc_ref[...]
    conv = bc_ref[...] + wc[0:1] * g2 + wc[1:2] * g1 + wc[2:3] * g
    a_ref[...] = (conv * _sigmoid(conv) * u).astype(a_ref.dtype)


def _ffn_up_dec(h2, w_gate, w_up, w_conv, b_conv, tap0, tap1, t_new, tn=256):
    m, d = h2.shape
    d_ff = w_gate.shape[1]
    col = lambda j: (0, j)
    return pl.pallas_call(
        functools.partial(_ffn_up_dec_kernel, t_new=t_new),
        grid=(d_ff // tn,),
        in_specs=[pl.BlockSpec((m, d), lambda j: (0, 0)), pl.BlockSpec((d, tn), col), pl.BlockSpec((d, tn), col),
                  pl.BlockSpec((CONV_W, tn), col), pl.BlockSpec((1, tn), col),
                  pl.BlockSpec((m, tn), col), pl.BlockSpec((m, tn), col)],
        out_specs=[pl.BlockSpec((m, tn), col), pl.BlockSpec((m, tn), col)],
        out_shape=[jax.ShapeDtypeStruct((m, d_ff), BF16), jax.ShapeDtypeStruct((m, d_ff), F32)],
        compiler_params=_cparams(("parallel",)),
        name="ffn_up_dec",
    )(h2, w_gate, w_up, w_conv, b_conv.reshape(1, d_ff), tap0, tap1)


def _split_mod(mod):
    return jnp.split(mod, 6, axis=-1)


def _attn_out_and_ffn_in(x, o_f, o_d, mods, p):
    _, _, gt1, sh2, sc2, _ = mods
    a = _attn_norm(o_f, o_d, p["g_fox_out"], p["g_dsa_out"])
    o = _matmul(a, p["w_out"], 512, 512, "out_proj")
    return _res_pre(o, x, gt1, p["g_post_attn"], p["g_pre_ffn"], sc2, sh2)


def _ffn_down_and_res(a, x1, mods, p):
    d_ff = a.shape[1]
    f = _matmul_ksplit(a, p["w_down"], 1024, 512, d_ff // 2, "ffn_down")
    return _res(f, x1, mods[5], p["g_post_ffn"])


def _project(x, mods, p):
    sh1, sc1 = mods[0], mods[1]
    h = _prenorm(x, p["g_pre_attn"], sc1, sh1)
    return _matmul(h, p["w_in"], 512, Z_TN, "in_proj")


def _prompt_layer(x, mods, p):
    s_len = x.shape[0]
    z = _project(x, mods, p)
    logf, cum = _logf_cumsum(z, p["b_f"])
    o_f = _fox_prompt(z, cum.T.reshape(H_FOX, 1, s_len))
    bias = _idx_prompt(z, min(TOPK_MAX, s_len // 4))
    o_d = _dsa_prompt(z, bias)
    x1, h2 = _attn_out_and_ffn_in(x, o_f, o_d, mods, p)
    a, tails = _ffn_up_prompt(h2, p["w_gate"], p["w_up"], p["w_conv"], p["b_conv"])
    y = _ffn_down_and_res(a, x1, mods, p)
    states = (z[:, Z_FK:Z_FK + FOX_W], z[:, Z_FV:Z_FV + FOX_W], logf, z[:, Z_DK:Z_DK + DSA_KV_W],
              z[:, Z_DV:Z_DV + DSA_KV_W], z[:, Z_SM + SM_IK:Z_SM + SM_IK + IDX_DIM], tails[-(CONV_W - 1):])
    return y, states


def _pad_rows(x, rows):
    return jnp.pad(x, ((0, 0), (0, rows - x.shape[1]), (0, 0)))


def _decode_layer(x, mods, p, page_table, caches, state_conv, t_new):
    m = x.shape[0]
    b = m // t_new
    n_pages = page_table.shape[1]
    cache_fox_k, cache_fox_v, cache_fox_logf, cache_dsa_k, cache_dsa_v, cache_idx_k = caches
    n_phys = cache_fox_k.shape[0]
    z = _project(x, mods, p)
    logf, _ = _logf_cumsum(z, p["b_f"])
    z3 = z.reshape(b, t_new, Z_W)

    rep = 16 // t_new
    iq16 = jnp.tile(z3[:, :, Z_IQ:Z_IQ + IQ_W].reshape(b, t_new, H_IDX, IDX_DIM).transpose(0, 2, 1, 3),
                    (1, 1, rep, 1)).reshape(b, H_IDX * 16, IDX_DIM)
    w16 = jnp.tile(z3[:, :, Z_SM + SM_IW:Z_SM + SM_IW + H_IDX].transpose(0, 2, 1), (1, 1, rep)).reshape(
        b, H_IDX * 16, 1)
    ik_new8 = _pad_rows(z3[:, :, Z_SM + SM_IK:Z_SM + SM_IK + IDX_DIM], 8)
    logf_new_t = jnp.pad(logf.reshape(b, t_new, H_FOX).transpose(0, 2, 1), ((0, 0), (0, 0), (0, PAGE_SIZE - t_new)))
    k_top = min(TOPK_MAX, (n_pages * PAGE_SIZE + t_new) // 4)
    dbias, fneg = _dec_idx(page_table, cache_idx_k, cache_fox_logf.transpose(0, 2, 1), iq16, w16, ik_new8,
                           logf_new_t, k_top, t_new)

    qf16 = _pad_rows(z3[:, :, Z_FQ:Z_FQ + FOX_W].reshape(b, t_new, H_FOX, HEAD_DIM).transpose(0, 2, 1, 3)
                     .reshape(b * H_FOX, t_new, HEAD_DIM), 16).reshape(b, H_FOX * 16, HEAD_DIM)
    qd = z3[:, :, Z_DQ:Z_DQ + DSA_W].reshape(b, t_new, H_DSA, HEAD_DIM).transpose(0, 2, 1, 3).reshape(
        b, H_DSA * t_new, HEAD_DIM)
    slopes = jnp.repeat(2.0 ** (-8.0 * jnp.arange(1, H_DSA + 1, dtype=F32) / H_DSA), t_new).reshape(H_DSA * t_new, 1)
    o_f, o_d = _dec_attn(
        page_table, cache_fox_k.reshape(n_phys, PAGE_SIZE, FOX_W), cache_fox_v.reshape(n_phys, PAGE_SIZE, FOX_W),
        cache_dsa_k.reshape(n_phys, PAGE_SIZE, DSA_KV_W), cache_dsa_v.reshape(n_phys, PAGE_SIZE, DSA_KV_W),
        _pad_rows(z3[:, :, Z_FK:Z_FK + FOX_W], 8), _pad_rows(z3[:, :, Z_FV:Z_FV + FOX_W], 8),
        _pad_rows(z3[:, :, Z_DK:Z_DK + DSA_KV_W], 8), _pad_rows(z3[:, :, Z_DV:Z_DV + DSA_KV_W], 8),
        qf16, qd, fneg, dbias, slopes, t_new)

    x1, h2 = _attn_out_and_ffn_in(x, o_f.reshape(m, FOX_W), o_d.reshape(m, DSA_W), mods, p)
    d_ff = state_conv.shape[-1]
    zero = jnp.zeros((b, 1, d_ff), F32)
    s0, s1 = state_conv[:, 0:1], state_conv[:, 1:2]
    tap0 = jnp.concatenate([s0, s1] + [zero] * (t_new - 2), axis=1).reshape(m, d_ff)
    tap1 = jnp.concatenate([s1] + [zero] * (t_new - 1), axis=1).reshape(m, d_ff)
    a, g = _ffn_up_dec(h2, p["w_gate"], p["w_up"], p["w_conv"], p["b_conv"], tap0, tap1, t_new)
    y = _ffn_down_and_res(a, x1, mods, p)
    conv_new = g.reshape(b, t_new, d_ff)[:, t_new - (CONV_W - 1):]
    states = (z[:, Z_FK:Z_FK + FOX_W], z[:, Z_FV:Z_FV + FOX_W], logf, z[:, Z_DK:Z_DK + DSA_KV_W],
              z[:, Z_DV:Z_DV + DSA_KV_W], z[:, Z_SM + SM_IK:Z_SM + SM_IK + IDX_DIM], conv_new)
    return y, states


def _reorder_w_in(w_in):
    d = w_in.shape[0]
    sizes = (FOX_W, FOX_W, FOX_W, H_FOX, DSA_W, DSA_KV_W, DSA_KV_W, IQ_W, IDX_DIM, H_IDX)
    offs = [0]
    for s in sizes:
        offs.append(offs[-1] + s)
    fq, fk, fv, fg, dq, dk, dv, iq, ik, iw = [w_in[:, offs[i]:offs[i + 1]] for i in range(len(sizes))]
    pad = jnp.zeros((d, Z_W - (Z_SM + IDX_DIM + H_FOX + H_IDX)), w_in.dtype)
    return jnp.concatenate([fq, fk, fv, dq, dk, dv, iq, ik, fg, iw, pad], axis=1).astype(BF16)


def kernel(x_prompt, x_sample, c_prompt, c_sample, page_table, cache_fox_k, cache_fox_v, cache_fox_logf, cache_dsa_k, cache_dsa_v, cache_idx_k, state_conv, w_in, b_f, w_out, g_fox_out, g_dsa_out, w_mod, b_mod, g_pre_attn, g_post_attn, g_pre_ffn, g_post_ffn, w_gate, w_up, w_conv, b_conv, w_down):
    depth = w_in.shape[0]
    bp, s_len, d = x_prompt.shape
    bs, t_new, _ = x_sample.shape
    assert bp == 1 and t_new >= CONV_W - 1 and 16 % t_new == 0

    xp = x_prompt.reshape(s_len, d)
    xs = x_sample.reshape(bs * t_new, d)
    n_c = bp + bs
    c_all = jnp.pad(jnp.concatenate([c_prompt, c_sample], axis=0), ((0, -n_c % 8), (0, 0)))
    p_states, s_states = [], []
    for l in range(depth):
        p = dict(w_in=_reorder_w_in(w_in[l]), b_f=b_f[l], w_out=w_out[l].astype(BF16),
                 g_fox_out=g_fox_out[l][None], g_dsa_out=g_dsa_out[l][None],
                 g_pre_attn=g_pre_attn[l][None], g_post_attn=g_post_attn[l][None],
                 g_pre_ffn=g_pre_ffn[l][None], g_post_ffn=g_post_ffn[l][None],
                 w_gate=w_gate[l].astype(BF16), w_up=w_up[l].astype(BF16), w_conv=w_conv[l], b_conv=b_conv[l],
                 w_down=w_down[l].astype(BF16))
        mod = _modulation(c_all, w_mod[l], b_mod[l])
        mods_p = _split_mod(mod[:bp])
        mods_s = [jnp.repeat(v, t_new, axis=0) for v in _split_mod(mod[bp:n_c])]
        xp, st_p = _prompt_layer(xp, mods_p, p)
        caches = (cache_fox_k[l], cache_fox_v[l], cache_fox_logf[l], cache_dsa_k[l], cache_dsa_v[l], cache_idx_k[l])
        xs, st_s = _decode_layer(xs, mods_s, p, page_table, caches, state_conv[l], t_new)
        p_states.append(st_p)
        s_states.append(st_s)

    def stack(states, i, shape):
        return jnp.stack([st[i].reshape(shape) for st in states])

    d_ff = state_conv.shape[-1]
    outs = [xp.reshape(bp, s_len, d), xs.reshape(bs, t_new, d)]
    for states, (bb, tt) in ((p_states, (bp, s_len)), (s_states, (bs, t_new))):
        outs += [stack(states, 0, (bb, tt, H_FOX, HEAD_DIM)), stack(states, 1, (bb, tt, H_FOX, HEAD_DIM)),
                 stack(states, 2, (bb, tt, H_FOX)), stack(states, 3, (bb, tt, H_DSA_KV, HEAD_DIM)),
                 stack(states, 4, (bb, tt, H_DSA_KV, HEAD_DIM)), stack(states, 5, (bb, tt, IDX_DIM)),
                 stack(states, 6, (bb, CONV_W - 1, d_ff))]
    return tuple(outs)
```

```python
import functools

import jax
import jax.numpy as jnp
import numpy as np
from jax import lax
from jax.experimental import pallas as pl
from jax.experimental.pallas import tpu as pltpu

HEAD_DIM = 128
H_FOX = 16
H_DSA = 16
H_DSA_KV = 4
DSA_GROUP = H_DSA // H_DSA_KV
H_IDX = 16
IDX_DIM = 64
TOPK_MAX = 256
CONV_W = 3
PAGE_SIZE = 128
RMS_EPS = 1e-6
FOX_W = H_FOX * HEAD_DIM
DSA_W = H_DSA * HEAD_DIM
DSA_KV_W = H_DSA_KV * HEAD_DIM
IQ_W = H_IDX * IDX_DIM

Z_FQ = 0
Z_FK = Z_FQ + FOX_W
Z_FV = Z_FK + FOX_W
Z_DQ = Z_FV + FOX_W
Z_DK = Z_DQ + DSA_W
Z_DV = Z_DK + DSA_KV_W
Z_IQ = Z_DV + DSA_KV_W
Z_SM = Z_IQ + IQ_W
SM_IK = 0
SM_FG = IDX_DIM
SM_IW = SM_FG + H_FOX
Z_USED = Z_SM + 128
Z_TN = 768
Z_W = -(-Z_USED // Z_TN) * Z_TN

NEG = -1e30
LOG2E = 1.4426950408889634
VT_ROWS = HEAD_DIM + 16
INT_MIN = -2 ** 31
MIB = 1024 * 1024
BF16 = jnp.bfloat16
F32 = jnp.float32

_NT = (((1,), (1,)), ((), ()))


def _cparams(sem, vmem_mib=48):
    return pltpu.CompilerParams(dimension_semantics=sem, vmem_limit_bytes=vmem_mib * MIB)


def _rms(x, g):
    return x * lax.rsqrt(jnp.mean(x * x, axis=-1, keepdims=True) + RMS_EPS) * g


def _sigmoid(x):
    return 1.0 / (1.0 + jnp.exp(-x))


def _sort_key(x):
    b = pltpu.bitcast(x, jnp.int32)
    return b ^ ((b >> 31) & jnp.int32(0x7FFFFFFF))


def _row_spec(arr, tm):
    d = arr.shape[1]
    if arr.shape[0] == 1:
        return pl.BlockSpec((1, d), lambda i: (0, 0))
    return pl.BlockSpec((tm, d), lambda i: (i, 0))


def _mod_kernel(c_ref, w_ref, b_ref, o_ref):
    c = c_ref[...]
    a = (c * _sigmoid(c)).astype(BF16)
    o_ref[...] = jnp.dot(a, w_ref[...].astype(BF16), preferred_element_type=F32) + b_ref[...]


def _modulation(c, w_mod, b_mod):
    r, d = c.shape
    n = w_mod.shape[1]
    tn = 512
    return pl.pallas_call(
        _mod_kernel,
        grid=(n // tn,),
        in_specs=[pl.BlockSpec((r, d), lambda j: (0, 0)),
                  pl.BlockSpec((d, tn), lambda j: (0, j)),
                  pl.BlockSpec((1, tn), lambda j: (0, j))],
        out_specs=pl.BlockSpec((r, tn), lambda j: (0, j)),
        out_shape=jax.ShapeDtypeStruct((r, n), F32),
        compiler_params=_cparams(("parallel",)),
        name="modulation",
    )(c, w_mod, b_mod.reshape(1, n))


def _prenorm_kernel(x_ref, g_ref, sc_ref, sh_ref, o_ref):
    y = _rms(x_ref[...], g_ref[...])
    o_ref[...] = (y * (1.0 + sc_ref[...]) + sh_ref[...]).astype(o_ref.dtype)


def _prenorm(x, g, sc, sh, tm=256):
    m, d = x.shape
    tm = min(tm, m)
    return pl.pallas_call(
        _prenorm_kernel,
        grid=(m // tm,),
        in_specs=[pl.BlockSpec((tm, d), lambda i: (i, 0)), _row_spec(g, tm), _row_spec(sc, tm), _row_spec(sh, tm)],
        out_specs=pl.BlockSpec((tm, d), lambda i: (i, 0)),
        out_shape=jax.ShapeDtypeStruct((m, d), BF16),
        compiler_params=_cparams(("parallel",)),
        name="prenorm",
    )(x, g, sc, sh)


def _attn_norm_kernel(of_ref, od_ref, gf_ref, gd_ref, o_ref):
    o_ref[:, :FOX_W] = _rms(of_ref[...], gf_ref[...]).astype(o_ref.dtype)
    o_ref[:, FOX_W:] = _rms(od_ref[...], gd_ref[...]).astype(o_ref.dtype)


def _attn_norm(o_f, o_d, g_f, g_d, tm=256):
    m = o_f.shape[0]
    tm = min(tm, m)
    return pl.pallas_call(
        _attn_norm_kernel,
        grid=(m // tm,),
        in_specs=[pl.BlockSpec((tm, FOX_W), lambda i: (i, 0)), pl.BlockSpec((tm, DSA_W), lambda i: (i, 0)),
                  _row_spec(g_f, tm), _row_spec(g_d, tm)],
        out_specs=pl.BlockSpec((tm, FOX_W + DSA_W), lambda i: (i, 0)),
        out_shape=jax.ShapeDtypeStruct((m, FOX_W + DSA_W), BF16),
        compiler_params=_cparams(("parallel",)),
        name="attn_norm",
    )(o_f, o_d, g_f, g_d)


def _res_pre_kernel(o_ref, x_ref, gt_ref, gpost_ref, gpre_ref, sc_ref, sh_ref, x1_ref, h2_ref):
    x1 = x_ref[...] + gt_ref[...] * _rms(o_ref[...], gpost_ref[...])
    x1_ref[...] = x1
    h2_ref[...] = (_rms(x1, gpre_ref[...]) * (1.0 + sc_ref[...]) + sh_ref[...]).astype(h2_ref.dtype)


def _res_pre(o, x, gt, g_post, g_pre, sc, sh, tm=128):
    m, d = x.shape
    tm = min(tm, m)
    blk =pl.BlockSpec((tm, d), lambda i: (i, 0))
    return pl.pallas_call(
        _res_pre_kernel,
        grid=(m // tm,),
        in_specs=[blk, blk, _row_spec(gt, tm), _row_spec(g_post, tm), _row_spec(g_pre, tm),
                  _row_spec(sc, tm), _row_spec(sh, tm)],
        out_specs=[blk, blk],
        out_shape=[jax.ShapeDtypeStruct((m, d), F32), jax.ShapeDtypeStruct((m, d), BF16)],
        compiler_params=_cparams(("parallel",)),
        name="res_pre",
    )(o, x, gt, g_post, g_pre, sc, sh)


def _res_kernel(f_ref, x_ref, gt_ref, gpost_ref, y_ref):
    y_ref[...] = x_ref[...] + gt_ref[...] * _rms(f_ref[...], gpost_ref[...])


def _res(f, x, gt, g_post, tm=256):
    m, d = x.shape
    tm = min(tm, m)
    blk =pl.BlockSpec((tm, d), lambda i: (i, 0))
    return pl.pallas_call(
        _res_kernel,
        grid=(m // tm,),
        in_specs=[blk, blk, _row_spec(gt, tm), _row_spec(g_post, tm)],
        out_specs=blk,
        out_shape=jax.ShapeDtypeStruct((m, d), F32),
        compiler_params=_cparams(("parallel",)),
        name="res",
    )(f, x, gt, g_post)


def _mm_kernel(a_ref, w_ref, o_ref):
    o_ref[...] = jnp.dot(a_ref[...], w_ref[...], preferred_element_type=F32)


def _matmul(a, w, tm, tn, name):
    m, k = a.shape
    n = w.shape[1]
    tm = min(tm, m)
    return pl.pallas_call(
        _mm_kernel,
        grid=(m // tm, n // tn),
        in_specs=[pl.BlockSpec((tm, k), lambda i, j: (i, 0)), pl.BlockSpec((k, tn), lambda i, j: (0, j))],
        out_specs=pl.BlockSpec((tm, tn), lambda i, j: (i, j)),
        out_shape=jax.ShapeDtypeStruct((m, n), F32),
        compiler_params=_cparams(("parallel", "parallel")),
        name=name,
    )(a, w)


def _mmk_kernel(a_ref, w_ref, o_ref):
    @pl.when(pl.program_id(2) == 0)
    def _():
        o_ref[...] = jnp.zeros_like(o_ref)

    o_ref[...] += jnp.dot(a_ref[...], w_ref[...], preferred_element_type=F32)


def _matmul_ksplit(a, w, tm, tn, tk, name):
    m, k = a.shape
    n = w.shape[1]
    tm = min(tm, m)
    return pl.pallas_call(
        _mmk_kernel,
        grid=(m // tm, n // tn, k // tk),
        in_specs=[pl.BlockSpec((tm, tk), lambda i, j, kk: (i, kk)), pl.BlockSpec((tk, tn), lambda i, j, kk: (kk, j))],
        out_specs=pl.BlockSpec((tm, tn), lambda i, j, kk: (i, j)),
        out_shape=jax.ShapeDtypeStruct((m, n), F32),
        compiler_params=_cparams(("parallel", "parallel", "arbitrary")),
        name=name,
    )(a, w)


def _logf_kernel(zs_ref, bf_ref, logf_ref, cum_ref, carry_ref, *, tm):
    @pl.when(pl.program_id(0) == 0)
    def _():
        carry_ref[...] = jnp.zeros_like(carry_ref)

    x = zs_ref[:, SM_FG:SM_FG + H_FOX] + bf_ref[...]
    lf = jnp.minimum(x, 0.0) - jnp.log1p(jnp.exp(-jnp.abs(x)))
    logf_ref[...] = lf
    row = lax.broadcasted_iota(jnp.int32, (tm, tm), 0)
    col = lax.broadcasted_iota(jnp.int32, (tm, tm), 1)
    tri = (col <= row).astype(F32)
    cum = jnp.dot(tri, lf, preferred_element_type=F32, precision=lax.Precision.HIGHEST) + carry_ref[...]
    cum_ref[...] = cum
    carry_ref[...] = cum[tm - 1:tm, :]


def _logf_cumsum(z, b_f, tm=256):
    m = z.shape[0]
    tm = min(tm, m)
    out = jax.ShapeDtypeStruct((m, H_FOX), F32)
    return pl.pallas_call(
        functools.partial(_logf_kernel, tm=tm),
        grid=(m // tm,),
        in_specs=[pl.BlockSpec((tm, 128), lambda i: (i, Z_SM // 128)), pl.BlockSpec((1, H_FOX), lambda i: (0, 0))],
        out_specs=[pl.BlockSpec((tm, H_FOX), lambda i: (i, 0)), pl.BlockSpec((tm, H_FOX), lambda i: (i, 0))],
        out_shape=[out, out],
        scratch_shapes=[pltpu.VMEM((1, H_FOX), F32)],
        compiler_params=_cparams(("arbitrary",)),
        name="logf_cumsum",
    )(z, b_f.reshape(1, H_FOX))


def _split3(x):
    hi = x.astype(BF16).astype(F32)
    mid = (x - hi).astype(BF16).astype(F32)
    lo = (x - hi - mid).astype(BF16).astype(F32)
    return hi, mid, lo


def _lane_row(shape, values):
    lane = lax.broadcasted_iota(jnp.int32, shape, len(shape) - 1)
    out = jnp.zeros(shape, F32)
    for i, v in enumerate(values):
        out = jnp.where(lane == i, v, out)
    return out


def _prep_kernel(fk_ref, fv_ref, dk_ref, dv_ref, iq_ref, sm_ref, cum_ref,
                 fka_ref, fvt_ref, dka_ref, dvt_ref, iqs_ref, ikb_ref, sgn_ref, *, tm):
    i = pl.program_id(0)
    ones_rows = jnp.where(lax.broadcasted_iota(jnp.int32, (VT_ROWS - HEAD_DIM, tm), 0) == 0, 1.0, 0.0).astype(BF16)
    nf = cum_ref[...] * (-LOG2E)
    for h in range(H_FOX):
        sl = slice(h * HEAD_DIM, (h + 1) * HEAD_DIM)
        fka_ref[h, :, :HEAD_DIM] = fk_ref[:, sl].astype(BF16)
        fka_ref[h, :, HEAD_DIM:] = _lane_row((tm, HEAD_DIM), _split3(nf[:, h:h + 1])).astype(BF16)
        fvt_ref[h, :HEAD_DIM, :] = fv_ref[:, sl].T.astype(BF16)
        fvt_ref[h, HEAD_DIM:, :] = ones_rows
    kpos = i * tm + lax.broadcasted_iota(jnp.int32, (tm, 1), 0)
    a = (kpos // 64).astype(F32)
    b = (kpos % 64).astype(F32)
    pos_aug = _lane_row((tm, HEAD_DIM), (a, a, a, b, b, b)).astype(BF16)
    for g in range(H_DSA_KV):
        sl = slice(g * HEAD_DIM, (g + 1) * HEAD_DIM)
        dka_ref[g, :, :HEAD_DIM] = dk_ref[:, sl].astype(BF16)
        dka_ref[g, :, HEAD_DIM:] = pos_aug
        dvt_ref[g, :HEAD_DIM, :] = dv_ref[:, sl].T.astype(BF16)
        dvt_ref[g, HEAD_DIM:, :] = ones_rows
    sm = sm_ref[...]
    w = sm[:, SM_IW:SM_IW + H_IDX] * (H_IDX ** -0.5)
    wabs = jnp.abs(w) * (IDX_DIM ** -0.5)
    for h in range(H_IDX):
        iqs_ref[h] = (iq_ref[:, h * IDX_DIM:(h + 1) * IDX_DIM] * wabs[:, h:h + 1]).astype(BF16)
    ikb_ref[...] = sm[:, SM_IK:SM_IK + IDX_DIM].astype(BF16)
    sgn_ref[...] = jnp.where(sm > 0, 1.0, -1.0).T[SM_IW:SM_IW + H_IDX, :]


def _attn_prep(z, cum, tm=256):
    s_len = z.shape[0]
    blk = lambda w, off: pl.BlockSpec((tm, w), lambda i: (i, off // w))
    return pl.pallas_call(
        functools.partial(_prep_kernel, tm=tm),
        grid=(s_len // tm,),
        in_specs=[blk(FOX_W, Z_FK), blk(FOX_W, Z_FV), blk(DSA_KV_W, Z_DK), blk(DSA_KV_W, Z_DV), blk(IQ_W, Z_IQ),
                  blk(128, Z_SM), pl.BlockSpec((tm, H_FOX), lambda i: (i, 0))],
        out_specs=[pl.BlockSpec((H_FOX, tm, 2 * HEAD_DIM), lambda i: (0, i, 0)),
                   pl.BlockSpec((H_FOX, VT_ROWS, tm), lambda i: (0, 0, i)),
                   pl.BlockSpec((H_DSA_KV, tm, 2 * HEAD_DIM), lambda i: (0, i, 0)),
                   pl.BlockSpec((H_DSA_KV, VT_ROWS, tm), lambda i: (0, 0, i)),
                   pl.BlockSpec((H_IDX, tm, IDX_DIM), lambda i: (0, i, 0)),
                   pl.BlockSpec((tm, IDX_DIM), lambda i: (i, 0)),
                   pl.BlockSpec((H_IDX, tm), lambda i: (0, i))],
        out_shape=[jax.ShapeDtypeStruct((H_FOX, s_len, 2 * HEAD_DIM), BF16),
                   jax.ShapeDtypeStruct((H_FOX, VT_ROWS, s_len), BF16),
                   jax.ShapeDtypeStruct((H_DSA_KV, s_len, 2 * HEAD_DIM), BF16),
                   jax.ShapeDtypeStruct((H_DSA_KV, VT_ROWS, s_len), BF16),
                   jax.ShapeDtypeStruct((H_IDX, s_len, IDX_DIM), BF16),
                   jax.ShapeDtypeStruct((s_len, IDX_DIM), BF16),
                   jax.ShapeDtypeStruct((H_IDX, s_len), F32)],
        compiler_params=_cparams(("parallel",)),
        name="attn_prep",
    )(z, z, z, z, z, z, cum)


def _flash_update(s, vt, m_ref, acc_ref):
    m_prev = m_ref[...]
    m_new = jnp.maximum(m_prev, jnp.max(s, axis=0, keepdims=True))
    alpha = jnp.exp2(m_prev - m_new)
    p = jnp.exp2((s - m_new).astype(BF16))
    acc_ref[...] = alpha * acc_ref[...] + jnp.dot(vt, p, preferred_element_type=F32)
    m_ref[...] = m_new


def _flash_finish(acc):
    return (acc[:HEAD_DIM] / acc[HEAD_DIM:HEAD_DIM + 1]).T


FOX_HEADS_PER_STEP = 2


def _fox_kernel(q_ref, ka_ref, vt_ref, o_ref, qa_ref, m_ref, acc_ref, *, t):
    qi = pl.program_id(1)
    ki = pl.program_id(2)
    heads = range(FOX_HEADS_PER_STEP)

    @pl.when(ki == 0)
    def _():
        for h in heads:
            qa_ref[h, :, :HEAD_DIM] = (
                q_ref[:, h * HEAD_DIM:(h + 1) * HEAD_DIM] * (HEAD_DIM ** -0.5 * LOG2E)).astype(BF16)
            qa_ref[h, :, HEAD_DIM:] = _lane_row((t, HEAD_DIM), (1.0, 1.0, 1.0)).astype(BF16)
        m_ref[...] = jnp.full_like(m_ref, NEG)
        acc_ref[...] = jnp.zeros_like(acc_ref)

    def step(diagonal):
        for h in heads:
            s = lax.dot_general(ka_ref[h], qa_ref[h], _NT, preferred_element_type=F32)
            if diagonal:
                krow = lax.broadcasted_iota(jnp.int32, (t, t), 0)
                qcol = lax.broadcasted_iota(jnp.int32, (t, t), 1)
                s = jnp.where(krow <= qcol, s, NEG)
            _flash_update(s, vt_ref[h], m_ref.at[h], acc_ref.at[h])

    @pl.when(ki < qi)
    def _():
        step(False)

    @pl.when(ki == qi)
    def _():
        step(True)

    @pl.when(ki == pl.num_programs(2) - 1)
    def _():
        for h in heads:
            o_ref[:, h * HEAD_DIM:(h + 1) * HEAD_DIM] = _flash_finish(acc_ref[h])


def _fox_prompt(z, fka, fvt, t=512):
    s_len = z.shape[0]
    n = s_len // t
    hp = FOX_HEADS_PER_STEP
    w = hp * HEAD_DIM
    return pl.pallas_call(
        functools.partial(_fox_kernel, t=t),
        grid=(H_FOX // hp, n, n),
        in_specs=[pl.BlockSpec((t, w), lambda h, qi, ki: (qi, Z_FQ // w + h)),
                  pl.BlockSpec((hp, t, 2 * HEAD_DIM), lambda h, qi, ki: (h, jnp.minimum(ki, qi), 0)),
                  pl.BlockSpec((hp, VT_ROWS, t), lambda h, qi, ki: (h, 0, jnp.minimum(ki, qi)))],
        out_specs=pl.BlockSpec((t, w), lambda h, qi, ki: (qi, h)),
        out_shape=jax.ShapeDtypeStruct((s_len, FOX_W), F32),
        scratch_shapes=[pltpu.VMEM((hp, t, 2 * HEAD_DIM), BF16), pltpu.VMEM((hp, 1, t), F32),
                        pltpu.VMEM((hp, VT_ROWS, t), F32)],
        compiler_params=_cparams(("parallel", "parallel", "arbitrary")),
        name="fox_prompt",
    )(z, fka, fvt)


def _bisect_threshold(count_ge, shape, k_top):
    def body(it, thr):
        cand = thr + lax.shift_left(jnp.int32(1), jnp.int32(31) - it)
        return jnp.where(count_ge(cand) >= k_top, cand, thr)

    return lax.fori_loop(0, 32, body, jnp.full(shape, INT_MIN, jnp.int32))


def _idx_kernel(iqs_ref, sgn_ref, ik_ref, bias_ref, key_ref, *, tq, tkc, k_top):
    i = pl.program_id(0)
    s_len = key_ref.shape[0]
    n_chunks = ((i + 1) * tq + tkc - 1) // tkc
    qpos = i * tq + lax.broadcasted_iota(jnp.int32, (tkc, tq), 1)

    def score_chunk(c, carry):
        off = pl.multiple_of(c * tkc, tkc)
        kc = ik_ref[pl.ds(off, tkc), :]
        acc = jnp.zeros((tkc, tq), F32)
        for h in range(H_IDX):
            d = lax.dot_general(kc, iqs_ref[h], _NT, preferred_element_type=F32)
            acc = acc + sgn_ref[h:h + 1, :] * jnp.maximum(d, 0.0)
        kpos = off + lax.broadcasted_iota(jnp.int32, (tkc, tq), 0)
        key_ref[pl.ds(off, tkc), :] = jnp.where(kpos <= qpos, _sort_key(acc), INT_MIN)
        return carry

    lax.fori_loop(0, n_chunks, score_chunk, 0)

    def count_ge(cand):
        def body(c, cnt):
            off = pl.multiple_of(c * tkc, tkc)
            hit = jnp.where(key_ref[pl.ds(off, tkc), :] >= cand, 1, 0)
            return cnt + jnp.sum(hit.reshape(tkc // 8, 8, tq), axis=0)

        cnt = lax.fori_loop(0, n_chunks, body, jnp.zeros((8, tq), jnp.int32))
        return jnp.sum(cnt, axis=0, keepdims=True)

    thr = _bisect_threshold(count_ge, (1, tq), k_top)
    thr = jnp.maximum(thr, INT_MIN + 1)

    def write_chunk(c, carry):
        off = pl.multiple_of(c * tkc, tkc)
        bias_ref[pl.ds(off, tkc), :] = jnp.where(key_ref[pl.ds(off, tkc), :] >= thr, 0.0, NEG).astype(BF16)
        return carry

    lax.fori_loop(0, n_chunks, write_chunk, 0)

    def fill_chunk(c, carry):
        off = pl.multiple_of(c * tkc, tkc)
        bias_ref[pl.ds(off, tkc), :] = jnp.full((tkc, tq), NEG, BF16)
        return carry

    lax.fori_loop(n_chunks, s_len // tkc, fill_chunk, 0)


def _idx_prompt(iqs, sgn_t, ikb, k_top, tq=256, tkc=128):
    s_len = ikb.shape[0]
    return pl.pallas_call(
        functools.partial(_idx_kernel, tq=tq, tkc=tkc, k_top=k_top),
        grid=(s_len // tq,),
        in_specs=[pl.BlockSpec((H_IDX, tq, IDX_DIM), lambda i: (0, i, 0)),
                  pl.BlockSpec((H_IDX, tq), lambda i: (0, i)),
                  pl.BlockSpec((s_len, IDX_DIM), lambda i: (0, 0))],
        out_specs=pl.BlockSpec((s_len, tq), lambda i: (0, i)),
        out_shape=jax.ShapeDtypeStruct((s_len, s_len), BF16),
        scratch_shapes=[pltpu.VMEM((s_len, tq), jnp.int32)],
        compiler_params=_cparams(("parallel",), 56),
        name="idx_prompt",
    )(iqs, sgn_t, ikb)


def _alibi_slope(h):
    return 2.0 ** (-8.0 * (h + 1) / H_DSA)


def _bf16_terms(x):
    out = []
    for _ in range(3):
        t = float(np.float32(x).astype(BF16))
        out.append(t)
        x = x - t
    return out


def _dsa_kernel(q_ref, ka_ref, vt_ref, bias_ref, o_ref, qa_ref, m_ref, acc_ref, *, tq, tk):
    qi = pl.program_id(0)
    ki = pl.program_id(1)
    ki_last = (qi * tq + tq - 1) // tk

    @pl.when(ki == 0)
    def _():
        for h in range(H_DSA):
            g, j = divmod(h, DSA_GROUP)
            rows = slice(j * tq, (j + 1) * tq)
            qa_ref[g, rows, :HEAD_DIM] = (
                q_ref[:, h * HEAD_DIM:(h + 1) * HEAD_DIM] * (HEAD_DIM ** -0.5 * LOG2E)).astype(BF16)
            sl = _bf16_terms(_alibi_slope(h) * LOG2E)
            qa_ref[g, rows, HEAD_DIM:] = _lane_row((tq, HEAD_DIM), [64.0 * t for t in sl] + sl).astype(BF16)
        m_ref[...] = jnp.full_like(m_ref, NEG)
        acc_ref[...] = jnp.zeros_like(acc_ref)

    @pl.when(ki <= ki_last)
    def _():
        mask = jnp.concatenate([bias_ref[...].astype(F32)] * DSA_GROUP, axis=1)
        for g in range(H_DSA_KV):
            s = lax.dot_general(ka_ref[g], qa_ref[g], _NT, preferred_element_type=F32) + mask
            _flash_update(s, vt_ref[g], m_ref.at[g], acc_ref.at[g])

    @pl.when(ki == pl.num_programs(1) - 1)
    def _():
        for h in range(H_DSA):
            g, j = divmod(h, DSA_GROUP)
            o_ref[:, h * HEAD_DIM:(h + 1) * HEAD_DIM] = _flash_finish(acc_ref[g, :, j * tq:(j + 1) * tq])


def _dsa_prompt(z, dka, dvt, bias_t, tq=256, tk=512):
    s_len = z.shape[0]

    def kv_blk(qi, ki):
        return jnp.minimum(ki, (qi * tq + tq - 1) // tk)

    return pl.pallas_call(
        functools.partial(_dsa_kernel, tq=tq, tk=tk),
        grid=(s_len // tq, s_len // tk),
        in_specs=[pl.BlockSpec((tq, DSA_W), lambda qi, ki: (qi, Z_DQ // DSA_W)),
                  pl.BlockSpec((H_DSA_KV, tk, 2 * HEAD_DIM), lambda qi, ki: (0, kv_blk(qi, ki), 0)),
                  pl.BlockSpec((H_DSA_KV, VT_ROWS, tk), lambda qi, ki: (0, 0, kv_blk(qi, ki))),
                  pl.BlockSpec((tk, tq), lambda qi, ki: (kv_blk(qi, ki), qi))],
        out_specs=pl.BlockSpec((tq, DSA_W), lambda qi, ki: (qi, 0)),
        out_shape=jax.ShapeDtypeStruct((s_len, DSA_W), F32),
        scratch_shapes=[pltpu.VMEM((H_DSA_KV, DSA_GROUP * tq, 2 * HEAD_DIM), BF16),
                        pltpu.VMEM((H_DSA_KV, 1, DSA_GROUP * tq), F32),
                        pltpu.VMEM((H_DSA_KV, VT_ROWS, DSA_GROUP * tq), F32)],
        compiler_params=_cparams(("parallel", "arbitrary")),
        name="dsa_prompt",
    )(z, dka, dvt, bias_t)


def _dec_idx_kernel(pt_ref, *refs, n_pages, t_new, k_top):
    del pt_ref
    ik_refs = refs[:n_pages]
    lf_refs = refs[n_pages:2 * n_pages]
    iq_ref, w_ref, ikn_ref, lfn_ref, dbias_ref, fneg_ref, key_ref = refs[2 * n_pages:]

    row = lax.broadcasted_iota(jnp.int32, (PAGE_SIZE, PAGE_SIZE), 0)
    col = lax.broadcasted_iota(jnp.int32, (PAGE_SIZE, PAGE_SIZE), 1)
    tri = (row <= col).astype(F32)
    carry = jnp.zeros((H_FOX, 1), F32)
    for p in range(n_pages + 1):
        lf = lf_refs[p][...] if p < n_pages else lfn_ref[...]
        cum = jnp.dot(lf, tri, preferred_element_type=F32, precision=lax.Precision.HIGHEST) + carry
        fneg_ref[:, p * PAGE_SIZE:(p + 1) * PAGE_SIZE] = -cum
        carry = cum[:, PAGE_SIZE - 1:PAGE_SIZE]

    w = w_ref[...] * (H_IDX ** -0.5)
    iqs = (iq_ref[...] * (jnp.abs(w) * (IDX_DIM ** -0.5))).astype(BF16)
    sgn = jnp.where(w > 0, 1.0, -1.0)
    qrow = lax.broadcasted_iota(jnp.int32, (16, PAGE_SIZE), 0) % t_new
    lane = lax.broadcasted_iota(jnp.int32, (16, PAGE_SIZE), 1)
    for p in range(n_pages + 1):
        if p < n_pages:
            kp = ik_refs[p][...].astype(BF16)
        else:
            kp = jnp.concatenate([ikn_ref[...], jnp.zeros((PAGE_SIZE - 8, IDX_DIM), F32)], axis=0).astype(BF16)
        d = lax.dot_general(iqs, kp, _NT, preferred_element_type=F32)
        sc = jnp.sum((sgn * jnp.maximum(d, 0.0)).reshape(H_IDX, 16, PAGE_SIZE), axis=0)
        key = _sort_key(sc)
        if p == n_pages:
            key = jnp.where((lane <= qrow) & (lane < t_new), key, INT_MIN)
        key_ref[:, p * PAGE_SIZE:(p + 1) * PAGE_SIZE] = key

    def count_ge(cand):
        return jnp.sum(jnp.where(key_ref[...] >= cand, 1, 0), axis=-1, keepdims=True)

    thr = jnp.maximum(_bisect_threshold(count_ge, (16, 1), k_top), INT_MIN + 1)
    dbias_ref[...] = jnp.where(key_ref[...] >= thr, 0.0, NEG)


def _dec_idx(page_table, idx_pages, logf_pages_t, iq16, w16, ik_new8, logf_new_t, k_top, t_new):
    b, n_pages = page_table.shape
    width = (n_pages + 1) * PAGE_SIZE
    page = lambda p: (lambda bi, pt: (pt[bi, p], 0, 0))
    per_b = lambda bi, pt: (bi, 0, 0)
    in_specs = ([pl.BlockSpec((None, PAGE_SIZE, IDX_DIM), page(p)) for p in range(n_pages)]
                + [pl.BlockSpec((None, H_FOX, PAGE_SIZE), page(p)) for p in range(n_pages)]
                + [pl.BlockSpec((None, H_IDX * 16, IDX_DIM), per_b), pl.BlockSpec((None, H_IDX * 16, 1), per_b),
                   pl.BlockSpec((None, 8, IDX_DIM), per_b), pl.BlockSpec((None, H_FOX, PAGE_SIZE), per_b)])
    return pl.pallas_call(
        functools.partial(_dec_idx_kernel, n_pages=n_pages, t_new=t_new, k_top=k_top),
        grid_spec=pltpu.PrefetchScalarGridSpec(
            num_scalar_prefetch=1,
            grid=(b,),
            in_specs=in_specs,
            out_specs=[pl.BlockSpec((None, 16, width), per_b), pl.BlockSpec((None, H_FOX, width), per_b)],
            scratch_shapes=[pltpu.VMEM((16, width), jnp.int32)]),
        out_shape=[jax.ShapeDtypeStruct((b, 16, width), F32), jax.ShapeDtypeStruct((b, H_FOX, width), F32)],
        compiler_params=_cparams(("arbitrary",)),
        name="dec_idx",
    )(page_table, *([idx_pages] * n_pages), *([logf_pages_t] * n_pages), iq16, w16, ik_new8, logf_new_t)


def _dec_attn_kernel(pt_ref, *refs, n_pages, kp, t_new):
    del pt_ref
    fk_refs, fv_refs, dk_refs, dv_refs = (refs[i * kp:(i + 1) * kp] for i in range(4))
    (fkn_ref, fvn_ref, dkn_ref, dvn_ref, qf_ref, qd_ref, fneg_ref, dbias_ref, fneg_new_ref, dbias_new_ref, slope_ref,
     of_ref, od_ref, mf_ref, lf_ref, accf_ref, md_ref, ld_ref, accd_ref, pairf_ref, paird_ref) = refs[4 * kp:]
    p_id = pl.program_id(1)
    n_steps = n_pages // kp
    rows_q = t_new * H_FOX
    wf = PAGE_SIZE * H_FOX
    wd = PAGE_SIZE * H_DSA_KV

    def pair_mask(n_keys, heads_per_key_head, n_kv):
        qh = lax.broadcasted_iota(jnp.int32, (rows_q, n_keys), 0) % H_FOX
        kh = lax.broadcasted_iota(jnp.int32, (rows_q, n_keys), 1) % n_kv
        return jnp.where(qh // heads_per_key_head == kh, 0.0, NEG)

    @pl.when(p_id == 0)
    def _():
        mf_ref[...] = jnp.full_like(mf_ref, NEG)
        lf_ref[...] = jnp.zeros_like(lf_ref)
        accf_ref[...] = jnp.zeros_like(accf_ref)
        md_ref[...] = jnp.full_like(md_ref, NEG)
        ld_ref[...] = jnp.zeros_like(ld_ref)
        accd_ref[...] = jnp.zeros_like(accd_ref)
        pairf_ref[...] = pair_mask(wf, 1, H_FOX)
        paird_ref[...] = pair_mask(wd, DSA_GROUP, H_DSA_KV)

    def attend(q_ref, k, v, bias, m_ref, l_ref, acc_ref):
        q = (q_ref[...] * (HEAD_DIM ** -0.5)).astype(BF16)
        s = lax.dot_general(q, k, _NT, preferred_element_type=F32) + bias
        m_prev = m_ref[...]
        m_new = jnp.maximum(m_prev, jnp.max(s, axis=-1, keepdims=True))
        alpha = jnp.exp(m_prev - m_new)
        p = jnp.exp(s - m_new)
        l_ref[...] = alpha * l_ref[...] + jnp.sum(p, axis=-1, keepdims=True)
        acc_ref[...] = alpha * acc_ref[...] + jnp.dot(p.astype(BF16), v, preferred_element_type=F32)
        m_ref[...] = m_new

    def sel_rows(db):
        return jnp.concatenate([jnp.broadcast_to(db[q:q + 1], (H_DSA, db.shape[1])) for q in range(t_new)], axis=0)

    def alibi(n_keys, tok0):
        tok = tok0 + lax.broadcasted_iota(jnp.int32, (1, n_keys), 1) // H_DSA_KV
        return slope_ref[...] * tok.astype(F32)

    @pl.when(p_id < n_steps)
    def _():
        cat = lambda page_refs: jnp.concatenate([r[...].astype(BF16) for r in page_refs], axis=0)
        bias = jnp.concatenate([pairf_ref[...]] * kp, axis=1) + fneg_ref[...]
        attend(qf_ref, cat(fk_refs), cat(fv_refs), bias, mf_ref, lf_ref, accf_ref)
        bias = (jnp.concatenate([paird_ref[...]] * kp, axis=1) + sel_rows(dbias_ref[...])
                + alibi(kp * wd, (p_id * kp - n_pages) * PAGE_SIZE))
        attend(qd_ref, cat(dk_refs), cat(dv_refs), bias, md_ref, ld_ref, accd_ref)

    @pl.when(p_id == n_steps)
    def _():
        def causal(n_keys, n_kv):
            q = lax.broadcasted_iota(jnp.int32, (rows_q, n_keys), 0) // H_FOX
            tok = lax.broadcasted_iota(jnp.int32, (rows_q, n_keys), 1) // n_kv
            return jnp.where(tok <= q, 0.0, NEG)

        nf = t_new * H_FOX
        bias = pairf_ref[:, :nf] + fneg_new_ref[:, :nf] + causal(nf, H_FOX)
        attend(qf_ref, fkn_ref[...].astype(BF16), fvn_ref[...].astype(BF16), bias, mf_ref, lf_ref, accf_ref)
        nd = t_new * H_DSA_KV
        bias = paird_ref[:, :nd] + sel_rows(dbias_new_ref[:, :nd]) + alibi(nd, 0)
        attend(qd_ref, dkn_ref[...].astype(BF16), dvn_ref[...].astype(BF16), bias, md_ref, ld_ref, accd_ref)
        of_ref[...] = accf_ref[...] / lf_ref[...]
        od_ref[...] = accd_ref[...] / ld_ref[...]


def _dec_attn(page_table, fox_k, fox_v, dsa_k, dsa_v, fk_new, fv_new, dk_new, dv_new, qf, qd, fneg_rows, dbias_rows,
              slopes, t_new):
    b, n_pages = page_table.shape
    kp = 4 if n_pages % 4 == 0 else 1
    n_steps = n_pages // kp
    wf = PAGE_SIZE * H_FOX
    wd = PAGE_SIZE * H_DSA_KV
    rows_q = t_new * H_FOX

    def page(j):
        return lambda bi, p, pt: (pt[bi, jnp.minimum(p, n_steps - 1) * kp + j], 0, 0)

    per_b = lambda bi, p, pt: (bi, 0, 0)
    past = lambda bi, p, pt: (bi, 0, jnp.minimum(p, n_steps - 1))
    new = lambda bi, p, pt: (bi, 0, n_pages)
    in_specs = ([pl.BlockSpec((None, wf, HEAD_DIM), page(j)) for j in range(kp)] * 2
                + [pl.BlockSpec((None, wd, HEAD_DIM), page(j)) for j in range(kp)] * 2
                + [pl.BlockSpec((None, t_new * H_FOX, HEAD_DIM), per_b)] * 2
                + [pl.BlockSpec((None, t_new * H_DSA_KV, HEAD_DIM), per_b)] * 2
                + [pl.BlockSpec((None, rows_q, HEAD_DIM), per_b)] * 2
                + [pl.BlockSpec((None, 1, kp * wf), past), pl.BlockSpec((None, 8, kp * wd), past),
                   pl.BlockSpec((None, 1, wf), new), pl.BlockSpec((None, 8, wd), new),
                   pl.BlockSpec((rows_q, 1), lambda bi, p, pt: (0, 0))])
    out = jax.ShapeDtypeStruct((b, rows_q, HEAD_DIM), F32)
    col = pltpu.VMEM((rows_q, 1), F32)
    acc = pltpu.VMEM((rows_q, HEAD_DIM), F32)
    return pl.pallas_call(
        functools.partial(_dec_attn_kernel, n_pages=n_pages, kp=kp, t_new=t_new),
        grid_spec=pltpu.PrefetchScalarGridSpec(
            num_scalar_prefetch=1,
            grid=(b, n_steps + 1),
            in_specs=in_specs,
            out_specs=[pl.BlockSpec((None, rows_q, HEAD_DIM), per_b)] * 2,
            scratch_shapes=[col, col, acc, col, col, acc, pltpu.VMEM((rows_q, wf), F32),
                            pltpu.VMEM((rows_q, wd), F32)]),
        out_shape=[out, out],
        compiler_params=_cparams(("parallel", "arbitrary")),
        name="dec_attn",
    )(page_table, *([fox_k] * kp), *([fox_v] * kp), *([dsa_k] * kp), *([dsa_v] * kp),
      fk_new, fv_new, dk_new, dv_new, qf, qd, fneg_rows, dbias_rows, fneg_rows, dbias_rows, slopes)


def _ffn_up_kernel(h_ref, hprev_ref, wg_ref, wu_ref, wc_ref, bc_ref, a_ref, tail_ref, *, tm):
    i = pl.program_id(0)
    h = h_ref[...]
    g = jnp.dot(h, wg_ref[...], preferred_element_type=F32)
    u = jnp.dot(h, wu_ref[...], preferred_element_type=F32)
    gh = jnp.dot(hprev_ref[...], wg_ref[...], preferred_element_type=F32)
    gh = jnp.where(i > 0, gh, 0.0)
    ext = jnp.concatenate([gh, g], axis=0)
    g1 = pltpu.roll(ext, 1, 0)[16:]
    g2 = pltpu.roll(ext, 2, 0)[16:]
    wc = wc_ref[...]
    conv = bc_ref[...] + wc[0:1] * g2 + wc[1:2] * g1 + wc[2:3] * g
    a_ref[...] = (conv * _sigmoid(conv) * u).astype(a_ref.dtype)
    tail_ref[...] = g[tm - 8:]


def _ffn_up_prompt(h2, w_gate, w_up, w_conv, b_conv, tm=1024, tn=256):
    m, d = h2.shape
    d_ff = w_gate.shape[1]
    tm = min(tm, m)
    return pl.pallas_call(
        functools.partial(_ffn_up_kernel, tm=tm),
        grid=(m // tm, d_ff // tn),
        in_specs=[pl.BlockSpec((tm, d), lambda i, j: (i, 0)),
                  pl.BlockSpec((16, d), lambda i, j: (jnp.maximum(i * (tm // 16) - 1, 0), 0)),
                  pl.BlockSpec((d, tn), lambda i, j: (0, j)), pl.BlockSpec((d, tn), lambda i, j: (0, j)),
                  pl.BlockSpec((CONV_W, tn), lambda i, j: (0, j)), pl.BlockSpec((1, tn), lambda i, j: (0, j))],
        out_specs=[pl.BlockSpec((tm, tn), lambda i, j: (i, j)), pl.BlockSpec((8, tn), lambda i, j: (i, j))],
        out_shape=[jax.ShapeDtypeStruct((m, d_ff), BF16), jax.ShapeDtypeStruct((m // tm * 8, d_ff), F32)],
        compiler_params=_cparams(("parallel", "parallel")),
        name="ffn_up_prompt",
    )(h2, h2, w_gate, w_up, w_conv, b_conv.reshape(1, d_ff))


def _ffn_up_dec_kernel(h_ref, wg_ref, wu_ref, wc_ref, bc_ref, s0_ref, s1_ref, a_ref, g_ref, *, t_new):
    h = h_ref[...]
    g = jnp.dot(h, wg_ref[...], preferred_element_type=F32)
    u = jnp.dot(h, wu_ref[...], preferred_element_type=F32)
    g_ref[...] = g
    t = lax.broadcasted_iota(jnp.int32, g.shape, 0) % t_new
    g1 = jnp.where(t >= 1, pltpu.roll(g, 1, 0), 0.0) + s1_ref[...]
    g2 = jnp.where(t >= 2, pltpu.roll(g, 2, 0), 0.0) + s0_ref[...]
    wc = wc_ref[...]
    conv = bc_ref[...] + wc[0:1] * g2 + wc[1:2] * g1 + wc[2:3] * g
    a_ref[...] = (conv * _sigmoid(conv) * u).astype(a_ref.dtype)


def _ffn_up_dec(h2, w_gate, w_up, w_conv, b_conv, tap0, tap1, t_new, tn=256):
    m, d = h2.shape
    d_ff = w_gate.shape[1]
    col = lambda j: (0, j)
    return pl.pallas_call(
        functools.partial(_ffn_up_dec_kernel, t_new=t_new),
        grid=(d_ff // tn,),
        in_specs=[pl.BlockSpec((m, d), lambda j: (0, 0)), pl.BlockSpec((d, tn), col), pl.BlockSpec((d, tn), col),
                  pl.BlockSpec((CONV_W, tn), col), pl.BlockSpec((1, tn), col),
                  pl.BlockSpec((m, tn), col), pl.BlockSpec((m, tn), col)],
        out_specs=[pl.BlockSpec((m, tn), col), pl.BlockSpec((m, tn), col)],
        out_shape=[jax.ShapeDtypeStruct((m, d_ff), BF16), jax.ShapeDtypeStruct((m, d_ff), F32)],
        compiler_params=_cparams(("parallel",)),
        name="ffn_up_dec",
    )(h2, w_gate, w_up, w_conv, b_conv.reshape(1, d_ff), tap0, tap1)


def _split_mod(mod):
    return jnp.split(mod, 6, axis=-1)


def _attn_out_and_ffn_in(x, o_f, o_d, mods, p):
    _, _, gt1, sh2, sc2, _ = mods
    a = _attn_norm(o_f, o_d, p["g_fox_out"], p["g_dsa_out"])
    o = _matmul(a, p["w_out"], 512, 512, "out_proj")
    return _res_pre(o, x, gt1, p["g_post_attn"], p["g_pre_ffn"], sc2, sh2)


def _ffn_down_and_res(a, x1, mods, p):
    d_ff = a.shape[1]
    f = _matmul_ksplit(a, p["w_down"], 1024, 512, d_ff // 2, "ffn_down")
    return _res(f, x1, mods[5], p["g_post_ffn"])


def _project(x, mods, p):
    sh1, sc1 = mods[0], mods[1]
    h = _prenorm(x, p["g_pre_attn"], sc1, sh1)
    return _matmul(h, p["w_in"], 512, Z_TN, "in_proj")


def _prompt_layer(x, mods, p):
    s_len = x.shape[0]
    z = _project(x, mods, p)
    logf, cum = _logf_cumsum(z, p["b_f"])
    fka, fvt, dka, dvt, iqs, ikb, sgn_t = _attn_prep(z, cum)
    o_f = _fox_prompt(z, fka, fvt)
    bias_t = _idx_prompt(iqs, sgn_t, ikb, min(TOPK_MAX, s_len // 4))
    o_d = _dsa_prompt(z, dka, dvt, bias_t)
    x1, h2 = _attn_out_and_ffn_in(x, o_f, o_d, mods, p)
    a, tails = _ffn_up_prompt(h2, p["w_gate"], p["w_up"], p["w_conv"], p["b_conv"])
    y = _ffn_down_and_res(a, x1, mods, p)
    states = (z[:, Z_FK:Z_FK + FOX_W], z[:, Z_FV:Z_FV + FOX_W], logf, z[:, Z_DK:Z_DK + DSA_KV_W],
              z[:, Z_DV:Z_DV + DSA_KV_W], z[:, Z_SM + SM_IK:Z_SM + SM_IK + IDX_DIM], tails[-(CONV_W - 1):])
    return y, states


def _pad_rows(x, rows):
    return jnp.pad(x, ((0, 0), (0, rows - x.shape[1]), (0, 0)))


def _decode_layer(x, mods, p, page_table, caches, state_conv, t_new):
    m = x.shape[0]
    b = m // t_new
    n_pages = page_table.shape[1]
    cache_fox_k, cache_fox_v, cache_fox_logf_t, cache_dsa_k, cache_dsa_v, cache_idx_k = caches
    z = _project(x, mods, p)
    logf, _ = _logf_cumsum(z, p["b_f"])
    z3 = z.reshape(b, t_new, Z_W)

    rep = 16 // t_new
    iq16 = jnp.tile(z3[:, :, Z_IQ:Z_IQ + IQ_W].reshape(b, t_new, H_IDX, IDX_DIM).transpose(0, 2, 1, 3),
                    (1, 1, rep, 1)).reshape(b, H_IDX * 16, IDX_DIM)
    w16 = jnp.tile(z3[:, :, Z_SM + SM_IW:Z_SM + SM_IW + H_IDX].transpose(0, 2, 1), (1, 1, rep)).reshape(
        b, H_IDX * 16, 1)
    ik_new8 = _pad_rows(z3[:, :, Z_SM + SM_IK:Z_SM + SM_IK + IDX_DIM], 8)
    logf_new_t = jnp.pad(logf.reshape(b, t_new, H_FOX).transpose(0, 2, 1), ((0, 0), (0, 0), (0, PAGE_SIZE - t_new)))
    k_top = min(TOPK_MAX, (n_pages * PAGE_SIZE + t_new) // 4)
    dbias, fneg = _dec_idx(page_table, cache_idx_k, cache_fox_logf_t, iq16, w16, ik_new8, logf_new_t, k_top, t_new)

    heads = lambda off, w: z3[:, :, off:off + w].reshape(b, t_new * (w // HEAD_DIM), HEAD_DIM)
    fneg_rows = fneg.transpose(0, 2, 1).reshape(b, 1, -1)
    dbias_rows = jnp.repeat(dbias[:, :8], H_DSA_KV, axis=-1)
    slopes = jnp.tile(2.0 ** (-8.0 * jnp.arange(1, H_DSA + 1, dtype=F32) / H_DSA), t_new).reshape(H_DSA * t_new, 1)
    o_f, o_d = _dec_attn(
        page_table, cache_fox_k, cache_fox_v, cache_dsa_k, cache_dsa_v,
        heads(Z_FK, FOX_W), heads(Z_FV, FOX_W), heads(Z_DK, DSA_KV_W), heads(Z_DV, DSA_KV_W),
        heads(Z_FQ, FOX_W), heads(Z_DQ, DSA_W), fneg_rows, dbias_rows, slopes, t_new)

    x1, h2 = _attn_out_and_ffn_in(x, o_f.reshape(m, FOX_W), o_d.reshape(m, DSA_W), mods, p)
    d_ff = state_conv.shape[-1]
    zero = jnp.zeros((b, 1, d_ff), F32)
    s0, s1 = state_conv[:, 0:1], state_conv[:, 1:2]
    tap0 = jnp.concatenate([s0, s1] + [zero] * (t_new - 2), axis=1).reshape(m, d_ff)
    tap1 = jnp.concatenate([s1] + [zero] * (t_new - 1), axis=1).reshape(m, d_ff)
    a, g = _ffn_up_dec(h2, p["w_gate"], p["w_up"], p["w_conv"], p["b_conv"], tap0, tap1, t_new)
    y = _ffn_down_and_res(a, x1, mods, p)
    conv_new = g.reshape(b, t_new, d_ff)[:, t_new - (CONV_W - 1):]
    states = (z[:, Z_FK:Z_FK + FOX_W], z[:, Z_FV:Z_FV + FOX_W], logf, z[:, Z_DK:Z_DK + DSA_KV_W],
              z[:, Z_DV:Z_DV + DSA_KV_W], z[:, Z_SM + SM_IK:Z_SM + SM_IK + IDX_DIM], conv_new)
    return y, states


def _reorder_w_in(w_in):
    d = w_in.shape[0]
    sizes = (FOX_W, FOX_W, FOX_W, H_FOX, DSA_W, DSA_KV_W, DSA_KV_W, IQ_W, IDX_DIM, H_IDX)
    offs = [0]
    for s in sizes:
        offs.append(offs[-1] + s)
    fq, fk, fv, fg, dq, dk, dv, iq, ik, iw = [w_in[:, offs[i]:offs[i + 1]] for i in range(len(sizes))]
    pad = jnp.zeros((d, Z_W - (Z_SM + IDX_DIM + H_FOX + H_IDX)), w_in.dtype)
    return jnp.concatenate([fq, fk, fv, dq, dk, dv, iq, ik, fg, iw, pad], axis=1).astype(BF16)


def kernel(x_prompt, x_sample, c_prompt, c_sample, page_table, cache_fox_k, cache_fox_v, cache_fox_logf, cache_dsa_k, cache_dsa_v, cache_idx_k, state_conv, w_in, b_f, w_out, g_fox_out, g_dsa_out, w_mod, b_mod, g_pre_attn, g_post_attn, g_pre_ffn, g_post_ffn, w_gate, w_up, w_conv, b_conv, w_down):
    depth = w_in.shape[0]
    bp, s_len, d = x_prompt.shape
    bs, t_new, _ = x_sample.shape
    assert bp == 1 and t_new >= CONV_W - 1 and 16 % t_new == 0

    xp = x_prompt.reshape(s_len, d)
    xs = x_sample.reshape(bs * t_new, d)
    n_c = bp + bs
    c_all = jnp.pad(jnp.concatenate([c_prompt, c_sample], axis=0), ((0, -n_c % 8), (0, 0)))
    n_phys = cache_fox_k.shape[1]
    fold = lambda c: c.reshape((depth * n_phys,) + c.shape[2:])
    rows = lambda c: c.reshape(depth * n_phys, PAGE_SIZE * c.shape[3], HEAD_DIM)
    caches = (rows(cache_fox_k), rows(cache_fox_v), fold(cache_fox_logf).transpose(0, 2, 1), rows(cache_dsa_k),
              rows(cache_dsa_v), fold(cache_idx_k))
    p_states, s_states = [], []
    for l in range(depth):
        p = dict(w_in=_reorder_w_in(w_in[l]), b_f=b_f[l], w_out=w_out[l].astype(BF16),
                 g_fox_out=g_fox_out[l][None], g_dsa_out=g_dsa_out[l][None],
                 g_pre_attn=g_pre_attn[l][None], g_post_attn=g_post_attn[l][None],
                 g_pre_ffn=g_pre_ffn[l][None], g_post_ffn=g_post_ffn[l][None],
                 w_gate=w_gate[l].astype(BF16), w_up=w_up[l].astype(BF16), w_conv=w_conv[l], b_conv=b_conv[l],
                 w_down=w_down[l].astype(BF16))
        mod = _modulation(c_all, w_mod[l], b_mod[l])
        mods_p = _split_mod(mod[:bp])
        mods_s = [jnp.repeat(v, t_new, axis=0) for v in _split_mod(mod[bp:n_c])]
        xp, st_p = _prompt_layer(xp, mods_p, p)
        xs, st_s = _decode_layer(xs, mods_s, p, page_table + l * n_phys, caches, state_conv[l], t_new)
        p_states.append(st_p)
        s_states.append(st_s)

    def stack(states, i, shape):
        return jnp.stack([st[i].reshape(shape) for st in states])

    d_ff = state_conv.shape[-1]
    outs = [xp.reshape(bp, s_len, d), xs.reshape(bs, t_new, d)]
    for states, (bb, tt) in ((p_states, (bp, s_len)), (s_states, (bs, t_new))):
        outs += [stack(states, 0, (bb, tt, H_FOX, HEAD_DIM)), stack(states, 1, (bb, tt, H_FOX, HEAD_DIM)),
                 stack(states, 2, (bb, tt, H_FOX)), stack(states, 3, (bb, tt, H_DSA_KV, HEAD_DIM)),
                 stack(states, 4, (bb, tt, H_DSA_KV, HEAD_DIM)), stack(states, 5, (bb, tt, IDX_DIM)),
                 stack(states, 6, (bb, CONV_W - 1, d_ff))]
    return tuple(outs)
```

```python
import functools

import jax
import jax.numpy as jnp
import numpy as np
from jax import lax
from jax.experimental import pallas as pl
from jax.experimental.pallas import tpu as pltpu

HEAD_DIM = 128
H_FOX = 16
H_DSA = 16
H_DSA_KV = 4
DSA_GROUP = H_DSA // H_DSA_KV
H_IDX = 16
IDX_DIM = 64
TOPK_MAX = 256
CONV_W = 3
PAGE_SIZE = 128
RMS_EPS = 1e-6
FOX_W = H_FOX * HEAD_DIM
DSA_W = H_DSA * HEAD_DIM
DSA_KV_W = H_DSA_KV * HEAD_DIM
IQ_W = H_IDX * IDX_DIM

Z_FQ = 0
Z_FK = Z_FQ + FOX_W
Z_FV = Z_FK + FOX_W
Z_DQ = Z_FV + FOX_W
Z_DK = Z_DQ + DSA_W
Z_DV = Z_DK + DSA_KV_W
Z_IQ = Z_DV + DSA_KV_W
Z_SM = Z_IQ + IQ_W
SM_IK = 0
SM_FG = IDX_DIM
SM_IW = SM_FG + H_FOX
Z_USED = Z_SM + 128
Z_TN = 768
Z_W = -(-Z_USED // Z_TN) * Z_TN

NEG = -1e30
LOG2E = 1.4426950408889634
VT_ROWS = HEAD_DIM + 16
INT_MIN = -2 ** 31
MIB = 1024 * 1024
BF16 = jnp.bfloat16
F32 = jnp.float32

_NT = (((1,), (1,)), ((), ()))


def _cparams(sem, vmem_mib=48):
    return pltpu.CompilerParams(dimension_semantics=sem, vmem_limit_bytes=vmem_mib * MIB)


def _rms(x, g):
    return x * lax.rsqrt(jnp.mean(x * x, axis=-1, keepdims=True) + RMS_EPS) * g


def _sigmoid(x):
    return 1.0 / (1.0 + jnp.exp(-x))


def _sort_key(x):
    b = pltpu.bitcast(x, jnp.int32)
    return b ^ ((b >> 31) & jnp.int32(0x7FFFFFFF))


def _row_spec(arr, tm):
    d = arr.shape[1]
    if arr.shape[0] == 1:
        return pl.BlockSpec((1, d), lambda i: (0, 0))
    return pl.BlockSpec((tm, d), lambda i: (i, 0))


def _mod_kernel(c_ref, w_ref, b_ref, o_ref):
    c = c_ref[...]
    a = (c * _sigmoid(c)).astype(BF16)
    o_ref[...] = jnp.dot(a, w_ref[...].astype(BF16), preferred_element_type=F32) + b_ref[...]


def _modulation(c, w_mod, b_mod):
    r, d = c.shape
    n = w_mod.shape[1]
    tn = 512
    return pl.pallas_call(
        _mod_kernel,
        grid=(n // tn,),
        in_specs=[pl.BlockSpec((r, d), lambda j: (0, 0)),
                  pl.BlockSpec((d, tn), lambda j: (0, j)),
                  pl.BlockSpec((1, tn), lambda j: (0, j))],
        out_specs=pl.BlockSpec((r, tn), lambda j: (0, j)),
        out_shape=jax.ShapeDtypeStruct((r, n), F32),
        compiler_params=_cparams(("parallel",)),
        name="modulation",
    )(c, w_mod, b_mod.reshape(1, n))


def _prenorm_kernel(x_ref, g_ref, sc_ref, sh_ref, o_ref):
    y = _rms(x_ref[...], g_ref[...])
    o_ref[...] = (y * (1.0 + sc_ref[...]) + sh_ref[...]).astype(o_ref.dtype)


def _prenorm(x, g, sc, sh, tm=256):
    m, d = x.shape
    tm = min(tm, m)
    return pl.pallas_call(
        _prenorm_kernel,
        grid=(m // tm,),
        in_specs=[pl.BlockSpec((tm, d), lambda i: (i, 0)), _row_spec(g, tm), _row_spec(sc, tm), _row_spec(sh, tm)],
        out_specs=pl.BlockSpec((tm, d), lambda i: (i, 0)),
        out_shape=jax.ShapeDtypeStruct((m, d), BF16),
        compiler_params=_cparams(("parallel",)),
        name="prenorm",
    )(x, g, sc, sh)


def _attn_norm_kernel(of_ref, od_ref, gf_ref, gd_ref, o_ref):
    o_ref[:, :FOX_W] = _rms(of_ref[...], gf_ref[...]).astype(o_ref.dtype)
    o_ref[:, FOX_W:] = _rms(od_ref[...], gd_ref[...]).astype(o_ref.dtype)


def _attn_norm(o_f, o_d, g_f, g_d, tm=256):
    m = o_f.shape[0]
    tm = min(tm, m)
    return pl.pallas_call(
        _attn_norm_kernel,
        grid=(m // tm,),
        in_specs=[pl.BlockSpec((tm, FOX_W), lambda i: (i, 0)), pl.BlockSpec((tm, DSA_W), lambda i: (i, 0)),
                  _row_spec(g_f, tm), _row_spec(g_d, tm)],
        out_specs=pl.BlockSpec((tm, FOX_W + DSA_W), lambda i: (i, 0)),
        out_shape=jax.ShapeDtypeStruct((m, FOX_W + DSA_W), BF16),
        compiler_params=_cparams(("parallel",)),
        name="attn_norm",
    )(o_f, o_d, g_f, g_d)


def _res_pre_kernel(o_ref, x_ref, gt_ref, gpost_ref, gpre_ref, sc_ref, sh_ref, x1_ref, h2_ref):
    x1 = x_ref[...] + gt_ref[...] * _rms(o_ref[...], gpost_ref[...])
    x1_ref[...] = x1
    h2_ref[...] = (_rms(x1, gpre_ref[...]) * (1.0 + sc_ref[...]) + sh_ref[...]).astype(h2_ref.dtype)


def _res_pre(o, x, gt, g_post, g_pre, sc, sh, tm=128):
    m, d = x.shape
    tm = min(tm, m)
    blk =pl.BlockSpec((tm, d), lambda i: (i, 0))
    return pl.pallas_call(
        _res_pre_kernel,
        grid=(m // tm,),
        in_specs=[blk, blk, _row_spec(gt, tm), _row_spec(g_post, tm), _row_spec(g_pre, tm),
                  _row_spec(sc, tm), _row_spec(sh, tm)],
        out_specs=[blk, blk],
        out_shape=[jax.ShapeDtypeStruct((m, d), F32), jax.ShapeDtypeStruct((m, d), BF16)],
        compiler_params=_cparams(("parallel",)),
        name="res_pre",
    )(o, x, gt, g_post, g_pre, sc, sh)


def _res_kernel(f_ref, x_ref, gt_ref, gpost_ref, y_ref):
    y_ref[...] = x_ref[...] + gt_ref[...] * _rms(f_ref[...], gpost_ref[...])


def _res(f, x, gt, g_post, tm=256):
    m, d = x.shape
    tm = min(tm, m)
    blk =pl.BlockSpec((tm, d), lambda i: (i, 0))
    return pl.pallas_call(
        _res_kernel,
        grid=(m // tm,),
        in_specs=[blk, blk, _row_spec(gt, tm), _row_spec(g_post, tm)],
        out_specs=blk,
        out_shape=jax.ShapeDtypeStruct((m, d), F32),
        compiler_params=_cparams(("parallel",)),
        name="res",
    )(f, x, gt, g_post)


def _mm_kernel(a_ref, w_ref, o_ref):
    o_ref[...] = jnp.dot(a_ref[...], w_ref[...], preferred_element_type=F32)


def _matmul(a, w, tm, tn, name):
    m, k = a.shape
    n = w.shape[1]
    tm = min(tm, m)
    return pl.pallas_call(
        _mm_kernel,
        grid=(m // tm, n // tn),
        in_specs=[pl.BlockSpec((tm, k), lambda i, j: (i, 0)), pl.BlockSpec((k, tn), lambda i, j: (0, j))],
        out_specs=pl.BlockSpec((tm, tn), lambda i, j: (i, j)),
        out_shape=jax.ShapeDtypeStruct((m, n), F32),
        compiler_params=_cparams(("parallel", "parallel")),
        name=name,
    )(a, w)


def _mmk_kernel(a_ref, w_ref, o_ref):
    @pl.when(pl.program_id(2) == 0)
    def _():
        o_ref[...] = jnp.zeros_like(o_ref)

    o_ref[...] += jnp.dot(a_ref[...], w_ref[...], preferred_element_type=F32)


def _matmul_ksplit(a, w, tm, tn, tk, name):
    m, k = a.shape
    n = w.shape[1]
    tm = min(tm, m)
    return pl.pallas_call(
        _mmk_kernel,
        grid=(m // tm, n // tn, k // tk),
        in_specs=[pl.BlockSpec((tm, tk), lambda i, j, kk: (i, kk)), pl.BlockSpec((tk, tn), lambda i, j, kk: (kk, j))],
        out_specs=pl.BlockSpec((tm, tn), lambda i, j, kk: (i, j)),
        out_shape=jax.ShapeDtypeStruct((m, n), F32),
        compiler_params=_cparams(("parallel", "parallel", "arbitrary")),
        name=name,
    )(a, w)


def _logf_kernel(zs_ref, bf_ref, logf_ref, cum_ref, carry_ref, *, tm):
    @pl.when(pl.program_id(0) == 0)
    def _():
        carry_ref[...] = jnp.zeros_like(carry_ref)

    x = zs_ref[:, SM_FG:SM_FG + H_FOX] + bf_ref[...]
    lf = jnp.minimum(x, 0.0) - jnp.log1p(jnp.exp(-jnp.abs(x)))
    logf_ref[...] = lf
    row = lax.broadcasted_iota(jnp.int32, (tm, tm), 0)
    col = lax.broadcasted_iota(jnp.int32, (tm, tm), 1)
    tri = (col <= row).astype(F32)
    cum = jnp.dot(tri, lf, preferred_element_type=F32, precision=lax.Precision.HIGHEST) + carry_ref[...]
    cum_ref[...] = cum
    carry_ref[...] = cum[tm - 1:tm, :]


def _logf_cumsum(z, b_f, tm=256):
    m = z.shape[0]
    tm = min(tm, m)
    out = jax.ShapeDtypeStruct((m, H_FOX), F32)
    return pl.pallas_call(
        functools.partial(_logf_kernel, tm=tm),
        grid=(m // tm,),
        in_specs=[pl.BlockSpec((tm, 128), lambda i: (i, Z_SM // 128)), pl.BlockSpec((1, H_FOX), lambda i: (0, 0))],
        out_specs=[pl.BlockSpec((tm, H_FOX), lambda i: (i, 0)), pl.BlockSpec((tm, H_FOX), lambda i: (i, 0))],
        out_shape=[out, out],
        scratch_shapes=[pltpu.VMEM((1, H_FOX), F32)],
        compiler_params=_cparams(("arbitrary",)),
        name="logf_cumsum",
    )(z, b_f.reshape(1, H_FOX))


def _split3(x):
    hi = x.astype(BF16).astype(F32)
    mid = (x - hi).astype(BF16).astype(F32)
    lo = (x - hi - mid).astype(BF16).astype(F32)
    return hi, mid, lo


def _lane_row(shape, values):
    lane = lax.broadcasted_iota(jnp.int32, shape, len(shape) - 1)
    out = jnp.zeros(shape, F32)
    for i, v in enumerate(values):
        out = jnp.where(lane == i, v, out)
    return out


def _prep_kernel(fk_ref, fv_ref, dk_ref, dv_ref, iq_ref, sm_ref, cum_ref,
                 fka_ref, fvt_ref, dka_ref, dvt_ref, iqs_ref, ikb_ref, sgn_ref, *, tm):
    i = pl.program_id(0)
    ones_rows = jnp.where(lax.broadcasted_iota(jnp.int32, (VT_ROWS - HEAD_DIM, tm), 0) == 0, 1.0, 0.0).astype(BF16)
    nf = cum_ref[...] * (-LOG2E)
    for h in range(H_FOX):
        sl = slice(h * HEAD_DIM, (h + 1) * HEAD_DIM)
        fka_ref[h, :, :HEAD_DIM] = fk_ref[:, sl].astype(BF16)
        fka_ref[h, :, HEAD_DIM:] = _lane_row((tm, HEAD_DIM), _split3(nf[:, h:h + 1])).astype(BF16)
        fvt_ref[h, :HEAD_DIM, :] = fv_ref[:, sl].T.astype(BF16)
        fvt_ref[h, HEAD_DIM:, :] = ones_rows
    kpos = i * tm + lax.broadcasted_iota(jnp.int32, (tm, 1), 0)
    a = (kpos // 64).astype(F32)
    b = (kpos % 64).astype(F32)
    pos_aug = _lane_row((tm, HEAD_DIM), (a, a, a, b, b, b)).astype(BF16)
    for g in range(H_DSA_KV):
        sl = slice(g * HEAD_DIM, (g + 1) * HEAD_DIM)
        dka_ref[g, :, :HEAD_DIM] = dk_ref[:, sl].astype(BF16)
        dka_ref[g, :, HEAD_DIM:] = pos_aug
        dvt_ref[g, :HEAD_DIM, :] = dv_ref[:, sl].T.astype(BF16)
        dvt_ref[g, HEAD_DIM:, :] = ones_rows
    sm = sm_ref[...]
    w = sm[:, SM_IW:SM_IW + H_IDX] * (H_IDX ** -0.5)
    wabs = jnp.abs(w) * (IDX_DIM ** -0.5)
    for h in range(H_IDX):
        iqs_ref[h] = (iq_ref[:, h * IDX_DIM:(h + 1) * IDX_DIM] * wabs[:, h:h + 1]).astype(BF16)
    ikb_ref[...] = sm[:, SM_IK:SM_IK + IDX_DIM].astype(BF16)
    sgn_ref[...] = jnp.where(sm > 0, 1.0, -1.0).T[SM_IW:SM_IW + H_IDX, :]


def _attn_prep(z, cum, tm=256):
    s_len = z.shape[0]
    blk = lambda w, off: pl.BlockSpec((tm, w), lambda i: (i, off // w))
    return pl.pallas_call(
        functools.partial(_prep_kernel, tm=tm),
        grid=(s_len // tm,),
        in_specs=[blk(FOX_W, Z_FK), blk(FOX_W, Z_FV), blk(DSA_KV_W, Z_DK), blk(DSA_KV_W, Z_DV), blk(IQ_W, Z_IQ),
                  blk(128, Z_SM), pl.BlockSpec((tm, H_FOX), lambda i: (i, 0))],
        out_specs=[pl.BlockSpec((H_FOX, tm, 2 * HEAD_DIM), lambda i: (0, i, 0)),
                   pl.BlockSpec((H_FOX, VT_ROWS, tm), lambda i: (0, 0, i)),
                   pl.BlockSpec((H_DSA_KV, tm, 2 * HEAD_DIM), lambda i: (0, i, 0)),
                   pl.BlockSpec((H_DSA_KV, VT_ROWS, tm), lambda i: (0, 0, i)),
                   pl.BlockSpec((H_IDX, tm, IDX_DIM), lambda i: (0, i, 0)),
                   pl.BlockSpec((tm, IDX_DIM), lambda i: (i, 0)),
                   pl.BlockSpec((H_IDX, tm), lambda i: (0, i))],
        out_shape=[jax.ShapeDtypeStruct((H_FOX, s_len, 2 * HEAD_DIM), BF16),
                   jax.ShapeDtypeStruct((H_FOX, VT_ROWS, s_len), BF16),
                   jax.ShapeDtypeStruct((H_DSA_KV, s_len, 2 * HEAD_DIM), BF16),
                   jax.ShapeDtypeStruct((H_DSA_KV, VT_ROWS, s_len), BF16),
                   jax.ShapeDtypeStruct((H_IDX, s_len, IDX_DIM), BF16),
                   jax.ShapeDtypeStruct((s_len, IDX_DIM), BF16),
                   jax.ShapeDtypeStruct((H_IDX, s_len), F32)],
        compiler_params=_cparams(("parallel",)),
        name="attn_prep",
    )(z, z, z, z, z, z, cum)


def _flash_update(s, vt, m_ref, acc_ref):
    m_prev = m_ref[...]
    m_new = jnp.maximum(m_prev, jnp.max(s, axis=0, keepdims=True))
    alpha = jnp.exp2(m_prev - m_new)
    p = jnp.exp2((s - m_new).astype(BF16))
    acc_ref[...] = alpha * acc_ref[...] + jnp.dot(vt, p, preferred_element_type=F32)
    m_ref[...] = m_new


def _flash_finish(acc):
    return (acc[:HEAD_DIM] / acc[HEAD_DIM:HEAD_DIM + 1]).T


FOX_HEADS_PER_STEP = 4


def _fox_kernel(q_ref, ka_ref, vt_ref, o_ref, qa_ref, m_ref, acc_ref, *, t):
    qi = pl.program_id(1)
    ki = pl.program_id(2)
    heads = range(FOX_HEADS_PER_STEP)

    @pl.when(ki == 0)
    def _():
        for h in heads:
            qa_ref[h, :, :HEAD_DIM] = (
                q_ref[:, h * HEAD_DIM:(h + 1) * HEAD_DIM] * (HEAD_DIM ** -0.5 * LOG2E)).astype(BF16)
            qa_ref[h, :, HEAD_DIM:] = _lane_row((t, HEAD_DIM), (1.0, 1.0, 1.0)).astype(BF16)
        m_ref[...] = jnp.full_like(m_ref, NEG)
        acc_ref[...] = jnp.zeros_like(acc_ref)

    def step(diagonal):
        for h in heads:
            s = lax.dot_general(ka_ref[h], qa_ref[h], _NT, preferred_element_type=F32)
            if diagonal:
                krow = lax.broadcasted_iota(jnp.int32, (t, t), 0)
                qcol = lax.broadcasted_iota(jnp.int32, (t, t), 1)
                s = jnp.where(krow <= qcol, s, NEG)
            _flash_update(s, vt_ref[h], m_ref.at[h], acc_ref.at[h])

    @pl.when(ki < qi)
    def _():
        step(False)

    @pl.when(ki == qi)
    def _():
        step(True)

    @pl.when(ki == pl.num_programs(2) - 1)
    def _():
        for h in heads:
            o_ref[:, h * HEAD_DIM:(h + 1) * HEAD_DIM] = _flash_finish(acc_ref[h])


def _fox_prompt(z, fka, fvt, t=512):
    s_len = z.shape[0]
    n = s_len // t
    hp = FOX_HEADS_PER_STEP
    w = hp * HEAD_DIM
    return pl.pallas_call(
        functools.partial(_fox_kernel, t=t),
        grid=(H_FOX // hp, n, n),
        in_specs=[pl.BlockSpec((t, w), lambda h, qi, ki: (qi, Z_FQ // w + h)),
                  pl.BlockSpec((hp, t, 2 * HEAD_DIM), lambda h, qi, ki: (h, jnp.minimum(ki, qi), 0)),
                  pl.BlockSpec((hp, VT_ROWS, t), lambda h, qi, ki: (h, 0, jnp.minimum(ki, qi)))],
        out_specs=pl.BlockSpec((t, w), lambda h, qi, ki: (qi, h)),
        out_shape=jax.ShapeDtypeStruct((s_len, FOX_W), F32),
        scratch_shapes=[pltpu.VMEM((hp, t, 2 * HEAD_DIM), BF16), pltpu.VMEM((hp, 1, t), F32),
                        pltpu.VMEM((hp, VT_ROWS, t), F32)],
        compiler_params=_cparams(("parallel", "parallel", "arbitrary")),
        name="fox_prompt",
    )(z, fka, fvt)


def _bisect_threshold(count_ge, n_valid, k_top):
    def cond(carry):
        it, _, cnt = carry
        unsettled = jnp.sum(jnp.where(cnt > k_top, 1, 0))
        return (it < 32) & (unsettled > 0)

    def body(carry):
        it, thr, cnt = carry
        cand = thr + lax.shift_left(jnp.int32(1), jnp.int32(31) - it)
        c = count_ge(cand)
        take = c >= k_top
        return it + 1, jnp.where(take, cand, thr), jnp.where(take, c, cnt)

    init = (jnp.int32(0), jnp.full(n_valid.shape, INT_MIN, jnp.int32), n_valid)
    return lax.while_loop(cond, body, init)[1]


def _idx_kernel(iqs_ref, sgn_ref, ik_ref, bias_ref, key_ref, *, tq, tkc, k_top):
    i = pl.program_id(0)
    s_len = key_ref.shape[0]
    n_chunks = ((i + 1) * tq + tkc - 1) // tkc
    qpos = i * tq + lax.broadcasted_iota(jnp.int32, (tkc, tq), 1)

    def score_chunk(c, carry):
        off = pl.multiple_of(c * tkc, tkc)
        kc = ik_ref[pl.ds(off, tkc), :]
        acc = jnp.zeros((tkc, tq), F32)
        for h in range(H_IDX):
            d = lax.dot_general(kc, iqs_ref[h], _NT, preferred_element_type=F32)
            acc = acc + sgn_ref[h:h + 1, :] * jnp.maximum(d, 0.0)
        kpos = off + lax.broadcasted_iota(jnp.int32, (tkc, tq), 0)
        key_ref[pl.ds(off, tkc), :] = jnp.where(kpos <= qpos, _sort_key(acc), INT_MIN)
        return carry

    lax.fori_loop(0, n_chunks, score_chunk, 0)

    def count_ge(cand):
        def body(c, cnt):
            off = pl.multiple_of(c * tq, tq)
            hit = jnp.where(key_ref[pl.ds(off, tq), :] >= cand, 1, 0)
            return cnt + jnp.sum(hit.reshape(tq // 8, 8, tq), axis=0)

        cnt = lax.fori_loop(0, i + 1, body, jnp.zeros((8, tq), jnp.int32))
        return jnp.sum(cnt, axis=0, keepdims=True)

    n_valid = i * tq + lax.broadcasted_iota(jnp.int32, (1, tq), 1) + 1
    thr = _bisect_threshold(count_ge, n_valid, k_top)
    thr = jnp.maximum(thr, INT_MIN + 1)

    def write_chunk(c, carry):
        off = pl.multiple_of(c * tkc, tkc)
        bias_ref[pl.ds(off, tkc), :] = jnp.where(key_ref[pl.ds(off, tkc), :] >= thr, 0.0, NEG).astype(BF16)
        return carry

    lax.fori_loop(0, n_chunks, write_chunk, 0)

    def fill_chunk(c, carry):
        off = pl.multiple_of(c * tkc, tkc)
        bias_ref[pl.ds(off, tkc), :] = jnp.full((tkc, tq), NEG, BF16)
        return carry

    lax.fori_loop(n_chunks, s_len // tkc, fill_chunk, 0)


def _idx_prompt(iqs, sgn_t, ikb, k_top, tq=256, tkc=128):
    s_len = ikb.shape[0]
    return pl.pallas_call(
        functools.partial(_idx_kernel, tq=tq, tkc=tkc, k_top=k_top),
        grid=(s_len // tq,),
        in_specs=[pl.BlockSpec((H_IDX, tq, IDX_DIM), lambda i: (0, i, 0)),
                  pl.BlockSpec((H_IDX, tq), lambda i: (0, i)),
                  pl.BlockSpec((s_len, IDX_DIM), lambda i: (0, 0))],
        out_specs=pl.BlockSpec((s_len, tq), lambda i: (0, i)),
        out_shape=jax.ShapeDtypeStruct((s_len, s_len), BF16),
        scratch_shapes=[pltpu.VMEM((s_len, tq), jnp.int32)],
        compiler_params=_cparams(("parallel",), 56),
        name="idx_prompt",
    )(iqs, sgn_t, ikb)


def _alibi_slope(h):
    return 2.0 ** (-8.0 * (h + 1) / H_DSA)


def _bf16_terms(x):
    out = []
    for _ in range(3):
        t = float(np.float32(x).astype(BF16))
        out.append(t)
        x = x - t
    return out


def _dsa_kernel(q_ref, ka_ref, vt_ref, bias_ref, o_ref, qa_ref, m_ref, acc_ref, *, tq, tk):
    qi = pl.program_id(0)
    ki = pl.program_id(1)
    ki_last = (qi * tq + tq - 1) // tk

    @pl.when(ki == 0)
    def _():
        for h in range(H_DSA):
            g, j = divmod(h, DSA_GROUP)
            rows = slice(j * tq, (j + 1) * tq)
            qa_ref[g, rows, :HEAD_DIM] = (
                q_ref[:, h * HEAD_DIM:(h + 1) * HEAD_DIM] * (HEAD_DIM ** -0.5 * LOG2E)).astype(BF16)
            sl = _bf16_terms(_alibi_slope(h) * LOG2E)
            qa_ref[g, rows, HEAD_DIM:] = _lane_row((tq, HEAD_DIM), [64.0 * t for t in sl] + sl).astype(BF16)
        m_ref[...] = jnp.full_like(m_ref, NEG)
        acc_ref[...] = jnp.zeros_like(acc_ref)

    @pl.when(ki <= ki_last)
    def _():
        mask = jnp.concatenate([bias_ref[...].astype(F32)] * DSA_GROUP, axis=1)
        for g in range(H_DSA_KV):
            s = lax.dot_general(ka_ref[g], qa_ref[g], _NT, preferred_element_type=F32) + mask
            _flash_update(s, vt_ref[g], m_ref.at[g], acc_ref.at[g])

    @pl.when(ki == pl.num_programs(1) - 1)
    def _():
        for h in range(H_DSA):
            g, j = divmod(h, DSA_GROUP)
            o_ref[:, h * HEAD_DIM:(h + 1) * HEAD_DIM] = _flash_finish(acc_ref[g, :, j * tq:(j + 1) * tq])


def _dsa_prompt(z, dka, dvt, bias_t, tq=256, tk=512):
    s_len = z.shape[0]

    def kv_blk(qi, ki):
        return jnp.minimum(ki, (qi * tq + tq - 1) // tk)

    return pl.pallas_call(
        functools.partial(_dsa_kernel, tq=tq, tk=tk),
        grid=(s_len // tq, s_len // tk),
        in_specs=[pl.BlockSpec((tq, DSA_W), lambda qi, ki: (qi, Z_DQ // DSA_W)),
                  pl.BlockSpec((H_DSA_KV, tk, 2 * HEAD_DIM), lambda qi, ki: (0, kv_blk(qi, ki), 0)),
                  pl.BlockSpec((H_DSA_KV, VT_ROWS, tk), lambda qi, ki: (0, 0, kv_blk(qi, ki))),
                  pl.BlockSpec((tk, tq), lambda qi, ki: (kv_blk(qi, ki), qi))],
        out_specs=pl.BlockSpec((tq, DSA_W), lambda qi, ki: (qi, 0)),
        out_shape=jax.ShapeDtypeStruct((s_len, DSA_W), F32),
        scratch_shapes=[pltpu.VMEM((H_DSA_KV, DSA_GROUP * tq, 2 * HEAD_DIM), BF16),
                        pltpu.VMEM((H_DSA_KV, 1, DSA_GROUP * tq), F32),
                        pltpu.VMEM((H_DSA_KV, VT_ROWS, DSA_GROUP * tq), F32)],
        compiler_params=_cparams(("parallel", "arbitrary")),
        name="dsa_prompt",
    )(z, dka, dvt, bias_t)


def _dec_idx_kernel(pt_ref, *refs, n_pages, t_new):
    del pt_ref
    ik_refs = refs[:n_pages]
    lf_refs = refs[n_pages:2 * n_pages]
    iq_ref, w_ref, ikn_ref, lfn_ref, key_ref, fneg_ref = refs[2 * n_pages:]

    row = lax.broadcasted_iota(jnp.int32, (PAGE_SIZE, PAGE_SIZE), 0)
    col = lax.broadcasted_iota(jnp.int32, (PAGE_SIZE, PAGE_SIZE), 1)
    tri = (row <= col).astype(F32)
    carry = jnp.zeros((H_FOX, 1), F32)
    for p in range(n_pages + 1):
        lf = lf_refs[p][...] if p < n_pages else lfn_ref[...]
        cum = jnp.dot(lf, tri, preferred_element_type=F32, precision=lax.Precision.HIGHEST) + carry
        fneg_ref[:, p * PAGE_SIZE:(p + 1) * PAGE_SIZE] = -cum
        carry = cum[:, PAGE_SIZE - 1:PAGE_SIZE]

    w = w_ref[...] * (H_IDX ** -0.5)
    iqs = (iq_ref[...] * (jnp.abs(w) * (IDX_DIM ** -0.5))).astype(BF16)
    sgn = jnp.where(w > 0, 1.0, -1.0)
    qrow = lax.broadcasted_iota(jnp.int32, (16, PAGE_SIZE), 0) % t_new
    lane = lax.broadcasted_iota(jnp.int32, (16, PAGE_SIZE), 1)
    for p in range(n_pages + 1):
        if p < n_pages:
            kp = ik_refs[p][...].astype(BF16)
        else:
            kp = jnp.concatenate([ikn_ref[...], jnp.zeros((PAGE_SIZE - 8, IDX_DIM), F32)], axis=0).astype(BF16)
        d = lax.dot_general(iqs, kp, _NT, preferred_element_type=F32)
        sc = jnp.sum((sgn * jnp.maximum(d, 0.0)).reshape(H_IDX, 16, PAGE_SIZE), axis=0)
        key = _sort_key(sc)
        if p == n_pages:
            key = jnp.where((lane <= qrow) & (lane < t_new), key, INT_MIN)
        key_ref[:, p * PAGE_SIZE:(p + 1) * PAGE_SIZE] = key


def _dec_idx(page_table, idx_pages, logf_pages_t, iq16, w16, ik_new8, logf_new_t, t_new):
    b, n_pages = page_table.shape
    width = (n_pages + 1) * PAGE_SIZE
    page = lambda p: (lambda bi, pt: (pt[bi, p], 0, 0))
    per_b = lambda bi, pt: (bi, 0, 0)
    in_specs = ([pl.BlockSpec((None, PAGE_SIZE, IDX_DIM), page(p)) for p in range(n_pages)]
                + [pl.BlockSpec((None, H_FOX, PAGE_SIZE), page(p)) for p in range(n_pages)]
                + [pl.BlockSpec((None, H_IDX * 16, IDX_DIM), per_b), pl.BlockSpec((None, H_IDX * 16, 1), per_b),
                   pl.BlockSpec((None, 8, IDX_DIM), per_b), pl.BlockSpec((None, H_FOX, PAGE_SIZE), per_b)])
    return pl.pallas_call(
        functools.partial(_dec_idx_kernel, n_pages=n_pages, t_new=t_new),
        grid_spec=pltpu.PrefetchScalarGridSpec(
            num_scalar_prefetch=1,
            grid=(b,),
            in_specs=in_specs,
            out_specs=[pl.BlockSpec((None, 16, width), per_b), pl.BlockSpec((None, H_FOX, width), per_b)]),
        out_shape=[jax.ShapeDtypeStruct((b, 16, width), jnp.int32), jax.ShapeDtypeStruct((b, H_FOX, width), F32)],
        compiler_params=_cparams(("parallel",)),
        name="dec_idx",
    )(page_table, *([idx_pages] * n_pages), *([logf_pages_t] * n_pages), iq16, w16, ik_new8, logf_new_t)


def _dec_select_kernel(key_ref, nv_ref, sel_ref, *, k_top):
    def count_ge(cand):
        return jnp.sum(jnp.where(key_ref[...] >= cand, 1, 0), axis=-1, keepdims=True)

    thr = jnp.maximum(_bisect_threshold(count_ge, nv_ref[...], k_top), INT_MIN + 1)
    sel_ref[...] = jnp.where(key_ref[...] >= thr, 0.0, NEG)


def _dec_select(keys, n_valid, k_top, tm=256):
    rows, width = keys.shape
    tm = min(tm, rows)
    return pl.pallas_call(
        functools.partial(_dec_select_kernel, k_top=k_top),
        grid=(rows // tm,),
        in_specs=[pl.BlockSpec((tm, width), lambda i: (i, 0)), pl.BlockSpec((tm, 1), lambda i: (i, 0))],
        out_specs=pl.BlockSpec((tm, width), lambda i: (i, 0)),
        out_shape=jax.ShapeDtypeStruct((rows, width), F32),
        compiler_params=_cparams(("parallel",)),
        name="dec_select",
    )(keys, n_valid)


def _dec_attn_kernel(pt_ref, *refs, n_pages, kp, t_new):
    del pt_ref
    fk_refs, fv_refs, dk_refs, dv_refs = (refs[i * kp:(i + 1) * kp] for i in range(4))
    (fkn_ref, fvn_ref, dkn_ref, dvn_ref, qf_ref, qd_ref, fneg_ref, dbias_ref, fneg_new_ref, dbias_new_ref, slope_ref,
     of_ref, od_ref, mf_ref, lf_ref, accf_ref, md_ref, ld_ref, accd_ref, pairf_ref, paird_ref) = refs[4 * kp:]
    p_id = pl.program_id(1)
    n_steps = n_pages // kp
    rows_q = t_new * H_FOX
    wf = PAGE_SIZE * H_FOX
    wd = PAGE_SIZE * H_DSA_KV

    def pair_mask(n_keys, heads_per_key_head, n_kv):
        qh = lax.broadcasted_iota(jnp.int32, (rows_q, n_keys), 0) % H_FOX
        kh = lax.broadcasted_iota(jnp.int32, (rows_q, n_keys), 1) % n_kv
        return jnp.where(qh // heads_per_key_head == kh, 0.0, NEG)

    @pl.when(p_id == 0)
    def _():
        mf_ref[...] = jnp.full_like(mf_ref, NEG)
        lf_ref[...] = jnp.zeros_like(lf_ref)
        accf_ref[...] = jnp.zeros_like(accf_ref)
        md_ref[...] = jnp.full_like(md_ref, NEG)
        ld_ref[...] = jnp.zeros_like(ld_ref)
        accd_ref[...] = jnp.zeros_like(accd_ref)
        pairf_ref[...] = pair_mask(wf, 1, H_FOX)
        paird_ref[...] = pair_mask(wd, DSA_GROUP, H_DSA_KV)

    def attend(q_ref, k, v, bias, m_ref, l_ref, acc_ref):
        q = (q_ref[...] * (HEAD_DIM ** -0.5)).astype(BF16)
        s = lax.dot_general(q, k, _NT, preferred_element_type=F32) + bias
        m_prev = m_ref[...]
        m_new = jnp.maximum(m_prev, jnp.max(s, axis=-1, keepdims=True))
        alpha = jnp.exp(m_prev - m_new)
        p = jnp.exp(s - m_new)
        l_ref[...] = alpha * l_ref[...] + jnp.sum(p, axis=-1, keepdims=True)
        acc_ref[...] = alpha * acc_ref[...] + jnp.dot(p.astype(BF16), v, preferred_element_type=F32)
        m_ref[...] = m_new

    def sel_rows(db):
        return jnp.concatenate([jnp.broadcast_to(db[q:q + 1], (H_DSA, db.shape[1])) for q in range(t_new)], axis=0)

    def alibi(n_keys, tok0):
        tok = tok0 + lax.broadcasted_iota(jnp.int32, (1, n_keys), 1) // H_DSA_KV
        return slope_ref[...] * tok.astype(F32)

    @pl.when(p_id < n_steps)
    def _():
        cat = lambda page_refs: jnp.concatenate([r[...].astype(BF16) for r in page_refs], axis=0)
        bias = jnp.concatenate([pairf_ref[...]] * kp, axis=1) + fneg_ref[...]
        attend(qf_ref, cat(fk_refs), cat(fv_refs), bias, mf_ref, lf_ref, accf_ref)
        bias = (jnp.concatenate([paird_ref[...]] * kp, axis=1) + sel_rows(dbias_ref[...])
                + alibi(kp * wd, (p_id * kp - n_pages) * PAGE_SIZE))
        attend(qd_ref, cat(dk_refs), cat(dv_refs), bias, md_ref, ld_ref, accd_ref)

    @pl.when(p_id == n_steps)
    def _():
        def causal(n_keys, n_kv):
            q = lax.broadcasted_iota(jnp.int32, (rows_q, n_keys), 0) // H_FOX
            tok = lax.broadcasted_iota(jnp.int32, (rows_q, n_keys), 1) // n_kv
            return jnp.where(tok <= q, 0.0, NEG)

        nf = t_new * H_FOX
        bias = pairf_ref[:, :nf] + fneg_new_ref[:, :nf] + causal(nf, H_FOX)
        attend(qf_ref, fkn_ref[...].astype(BF16), fvn_ref[...].astype(BF16), bias, mf_ref, lf_ref, accf_ref)
        nd = t_new * H_DSA_KV
        bias = paird_ref[:, :nd] + sel_rows(dbias_new_ref[:, :nd]) + alibi(nd, 0)
        attend(qd_ref, dkn_ref[...].astype(BF16), dvn_ref[...].astype(BF16), bias, md_ref, ld_ref, accd_ref)
        of_ref[...] = accf_ref[...] / lf_ref[...]
        od_ref[...] = accd_ref[...] / ld_ref[...]


def _dec_attn(page_table, fox_k, fox_v, dsa_k, dsa_v, fk_new, fv_new, dk_new, dv_new, qf, qd, fneg_rows, dbias_rows,
              slopes, t_new):
    b, n_pages = page_table.shape
    kp = 4 if n_pages % 4 == 0 else 1
    n_steps = n_pages // kp
    wf = PAGE_SIZE * H_FOX
    wd = PAGE_SIZE * H_DSA_KV
    rows_q = t_new * H_FOX

    def page(j):
        return lambda bi, p, pt: (pt[bi, jnp.minimum(p, n_steps - 1) * kp + j], 0, 0)

    per_b = lambda bi, p, pt: (bi, 0, 0)
    past = lambda bi, p, pt: (bi, 0, jnp.minimum(p, n_steps - 1))
    new = lambda bi, p, pt: (bi, 0, n_pages)
    in_specs = ([pl.BlockSpec((None, wf, HEAD_DIM), page(j)) for j in range(kp)] * 2
                + [pl.BlockSpec((None, wd, HEAD_DIM), page(j)) for j in range(kp)] * 2
                + [pl.BlockSpec((None, t_new * H_FOX, HEAD_DIM), per_b)] * 2
                + [pl.BlockSpec((None, t_new * H_DSA_KV, HEAD_DIM), per_b)] * 2
                + [pl.BlockSpec((None, rows_q, HEAD_DIM), per_b)] * 2
                + [pl.BlockSpec((None, 1, kp * wf), past), pl.BlockSpec((None, 8, kp * wd), past),
                   pl.BlockSpec((None, 1, wf), new), pl.BlockSpec((None, 8, wd), new),
                   pl.BlockSpec((rows_q, 1), lambda bi, p, pt: (0, 0))])
    out = jax.ShapeDtypeStruct((b, rows_q, HEAD_DIM), F32)
    col = pltpu.VMEM((rows_q, 1), F32)
    acc = pltpu.VMEM((rows_q, HEAD_DIM), F32)
    return pl.pallas_call(
        functools.partial(_dec_attn_kernel, n_pages=n_pages, kp=kp, t_new=t_new),
        grid_spec=pltpu.PrefetchScalarGridSpec(
            num_scalar_prefetch=1,
            grid=(b, n_steps + 1),
            in_specs=in_specs,
            out_specs=[pl.BlockSpec((None, rows_q, HEAD_DIM), per_b)] * 2,
            scratch_shapes=[col, col, acc, col, col, acc, pltpu.VMEM((rows_q, wf), F32),
                            pltpu.VMEM((rows_q, wd), F32)]),
        out_shape=[out, out],
        compiler_params=_cparams(("parallel", "arbitrary")),
        name="dec_attn",
    )(page_table, *([fox_k] * kp), *([fox_v] * kp), *([dsa_k] * kp), *([dsa_v] * kp),
      fk_new, fv_new, dk_new, dv_new, qf, qd, fneg_rows, dbias_rows, fneg_rows, dbias_rows, slopes)


def _ffn_up_kernel(h_ref, hprev_ref, wg_ref, wu_ref, wc_ref, bc_ref, a_ref, tail_ref, *, tm):
    i = pl.program_id(0)
    h = h_ref[...]
    g = jnp.dot(h, wg_ref[...], preferred_element_type=F32)
    u = jnp.dot(h, wu_ref[...], preferred_element_type=F32)
    gh = jnp.dot(hprev_ref[...], wg_ref[...], preferred_element_type=F32)
    gh = jnp.where(i > 0, gh, 0.0)
    ext = jnp.concatenate([gh, g], axis=0)
    g1 = pltpu.roll(ext, 1, 0)[16:]
    g2 = pltpu.roll(ext, 2, 0)[16:]
    wc = wc_ref[...]
    conv = bc_ref[...] + wc[0:1] * g2 + wc[1:2] * g1 + wc[2:3] * g
    a_ref[...] = (conv * _sigmoid(conv) * u).astype(a_ref.dtype)
    tail_ref[...] = g[tm - 8:]


def _ffn_up_prompt(h2, w_gate, w_up, w_conv, b_conv, tm=1024, tn=256):
    m, d = h2.shape
    d_ff = w_gate.shape[1]
    tm = min(tm, m)
    return pl.pallas_call(
        functools.partial(_ffn_up_kernel, tm=tm),
        grid=(m // tm, d_ff // tn),
        in_specs=[pl.BlockSpec((tm, d), lambda i, j: (i, 0)),
                  pl.BlockSpec((16, d), lambda i, j: (jnp.maximum(i * (tm // 16) - 1, 0), 0)),
                  pl.BlockSpec((d, tn), lambda i, j: (0, j)), pl.BlockSpec((d, tn), lambda i, j: (0, j)),
                  pl.BlockSpec((CONV_W, tn), lambda i, j: (0, j)), pl.BlockSpec((1, tn), lambda i, j: (0, j))],
        out_specs=[pl.BlockSpec((tm, tn), lambda i, j: (i, j)), pl.BlockSpec((8, tn), lambda i, j: (i, j))],
        out_shape=[jax.ShapeDtypeStruct((m, d_ff), BF16), jax.ShapeDtypeStruct((m // tm * 8, d_ff), F32)],
        compiler_params=_cparams(("parallel", "parallel")),
        name="ffn_up_prompt",
    )(h2, h2, w_gate, w_up, w_conv, b_conv.reshape(1, d_ff))


def _ffn_up_dec_kernel(h_ref, wg_ref, wu_ref, wc_ref, bc_ref, s0_ref, s1_ref, a_ref, g_ref, *, t_new):
    h = h_ref[...]
    g = jnp.dot(h, wg_ref[...], preferred_element_type=F32)
    u = jnp.dot(h, wu_ref[...], preferred_element_type=F32)
    g_ref[...] = g
    t = lax.broadcasted_iota(jnp.int32, g.shape, 0) % t_new
    g1 = jnp.where(t >= 1, pltpu.roll(g, 1, 0), 0.0) + s1_ref[...]
    g2 = jnp.where(t >= 2, pltpu.roll(g, 2, 0), 0.0) + s0_ref[...]
    wc = wc_ref[...]
    conv = bc_ref[...] + wc[0:1] * g2 + wc[1:2] * g1 + wc[2:3] * g
    a_ref[...] = (conv * _sigmoid(conv) * u).astype(a_ref.dtype)


def _ffn_up_dec(h2, w_gate, w_up, w_conv, b_conv, tap0, tap1, t_new, tn=256):
    m, d = h2.shape
    d_ff = w_gate.shape[1]
    col = lambda j: (0, j)
    return pl.pallas_call(
        functools.partial(_ffn_up_dec_kernel, t_new=t_new),
        grid=(d_ff // tn,),
        in_specs=[pl.BlockSpec((m, d), lambda j: (0, 0)), pl.BlockSpec((d, tn), col), pl.BlockSpec((d, tn), col),
                  pl.BlockSpec((CONV_W, tn), col), pl.BlockSpec((1, tn), col),
                  pl.BlockSpec((m, tn), col), pl.BlockSpec((m, tn), col)],
        out_specs=[pl.BlockSpec((m, tn), col), pl.BlockSpec((m, tn), col)],
        out_shape=[jax.ShapeDtypeStruct((m, d_ff), BF16), jax.ShapeDtypeStruct((m, d_ff), F32)],
        compiler_params=_cparams(("parallel",)),
        name="ffn_up_dec",
    )(h2, w_gate, w_up, w_conv, b_conv.reshape(1, d_ff), tap0, tap1)


def _split_mod(mod):
    return jnp.split(mod, 6, axis=-1)


def _attn_out_and_ffn_in(x, o_f, o_d, mods, p):
    _, _, gt1, sh2, sc2, _ = mods
    a = _attn_norm(o_f, o_d, p["g_fox_out"], p["g_dsa_out"])
    o = _matmul(a, p["w_out"], 1024, 512, "out_proj")
    return _res_pre(o, x, gt1, p["g_post_attn"], p["g_pre_ffn"], sc2, sh2)


def _ffn_down_and_res(a, x1, mods, p):
    d_ff = a.shape[1]
    f = _matmul_ksplit(a, p["w_down"], 1024, 512, d_ff // 2, "ffn_down")
    return _res(f, x1, mods[5], p["g_post_ffn"])


def _project(x, mods, p):
    sh1, sc1 = mods[0], mods[1]
    h = _prenorm(x, p["g_pre_attn"], sc1, sh1)
    return _matmul(h, p["w_in"], 1024, Z_TN, "in_proj")


def _prompt_layer(x, mods, p):
    s_len = x.shape[0]
    z = _project(x, mods, p)
    logf, cum = _logf_cumsum(z, p["b_f"])
    fka, fvt, dka, dvt, iqs, ikb, sgn_t = _attn_prep(z, cum)
    o_f = _fox_prompt(z, fka, fvt)
    bias_t = _idx_prompt(iqs, sgn_t, ikb, min(TOPK_MAX, s_len // 4))
    o_d = _dsa_prompt(z, dka, dvt, bias_t)
    x1, h2 = _attn_out_and_ffn_in(x, o_f, o_d, mods, p)
    a, tails = _ffn_up_prompt(h2, p["w_gate"], p["w_up"], p["w_conv"], p["b_conv"])
    y = _ffn_down_and_res(a, x1, mods, p)
    states = (z[:, Z_FK:Z_FK + FOX_W], z[:, Z_FV:Z_FV + FOX_W], logf, z[:, Z_DK:Z_DK + DSA_KV_W],
              z[:, Z_DV:Z_DV + DSA_KV_W], z[:, Z_SM + SM_IK:Z_SM + SM_IK + IDX_DIM], tails[-(CONV_W - 1):])
    return y, states


def _pad_rows(x, rows):
    return jnp.pad(x, ((0, 0), (0, rows - x.shape[1]), (0, 0)))


def _decode_layer(x, mods, p, page_table, caches, state_conv, t_new):
    m = x.shape[0]
    b = m // t_new
    n_pages = page_table.shape[1]
    cache_fox_k, cache_fox_v, cache_fox_logf_t, cache_dsa_k, cache_dsa_v, cache_idx_k = caches
    z = _project(x, mods, p)
    logf, _ = _logf_cumsum(z, p["b_f"])
    z3 = z.reshape(b, t_new, Z_W)

    rep = 16 // t_new
    iq16 = jnp.tile(z3[:, :, Z_IQ:Z_IQ + IQ_W].reshape(b, t_new, H_IDX, IDX_DIM).transpose(0, 2, 1, 3),
                    (1, 1, rep, 1)).reshape(b, H_IDX * 16, IDX_DIM)
    w16 = jnp.tile(z3[:, :, Z_SM + SM_IW:Z_SM + SM_IW + H_IDX].transpose(0, 2, 1), (1, 1, rep)).reshape(
        b, H_IDX * 16, 1)
    ik_new8 = _pad_rows(z3[:, :, Z_SM + SM_IK:Z_SM + SM_IK + IDX_DIM], 8)
    logf_new_t = jnp.pad(logf.reshape(b, t_new, H_FOX).transpose(0, 2, 1), ((0, 0), (0, 0), (0, PAGE_SIZE - t_new)))
    k_top = min(TOPK_MAX, (n_pages * PAGE_SIZE + t_new) // 4)
    keys, fneg = _dec_idx(page_table, cache_idx_k, cache_fox_logf_t, iq16, w16, ik_new8, logf_new_t, t_new)
    n_valid = jnp.tile(n_pages * PAGE_SIZE + 1 + jnp.arange(8, dtype=jnp.int32) % t_new, b).reshape(b * 8, 1)
    dbias = _dec_select(keys[:, :8].reshape(b * 8, -1), n_valid, k_top).reshape(b, 8, -1)

    heads = lambda off, w: z3[:, :, off:off + w].reshape(b, t_new * (w // HEAD_DIM), HEAD_DIM)
    fneg_rows = fneg.transpose(0, 2, 1).reshape(b, 1, -1)
    dbias_rows = jnp.repeat(dbias, H_DSA_KV, axis=-1)
    slopes = jnp.tile(2.0 ** (-8.0 * jnp.arange(1, H_DSA + 1, dtype=F32) / H_DSA), t_new).reshape(H_DSA * t_new, 1)
    o_f, o_d = _dec_attn(
        page_table, cache_fox_k, cache_fox_v, cache_dsa_k, cache_dsa_v,
        heads(Z_FK, FOX_W), heads(Z_FV, FOX_W), heads(Z_DK, DSA_KV_W), heads(Z_DV, DSA_KV_W),
        heads(Z_FQ, FOX_W), heads(Z_DQ, DSA_W), fneg_rows, dbias_rows, slopes, t_new)

    x1, h2 = _attn_out_and_ffn_in(x, o_f.reshape(m, FOX_W), o_d.reshape(m, DSA_W), mods, p)
    d_ff = state_conv.shape[-1]
    zero = jnp.zeros((b, 1, d_ff), F32)
    s0, s1 = state_conv[:, 0:1], state_conv[:, 1:2]
    tap0 = jnp.concatenate([s0, s1] + [zero] * (t_new - 2), axis=1).reshape(m, d_ff)
    tap1 = jnp.concatenate([s1] + [zero] * (t_new - 1), axis=1).reshape(m, d_ff)
    a, g = _ffn_up_dec(h2, p["w_gate"], p["w_up"], p["w_conv"], p["b_conv"], tap0, tap1, t_new)
    y = _ffn_down_and_res(a, x1, mods, p)
    conv_new = g.reshape(b, t_new, d_ff)[:, t_new - (CONV_W - 1):]
    states = (z[:, Z_FK:Z_FK + FOX_W], z[:, Z_FV:Z_FV + FOX_W], logf, z[:, Z_DK:Z_DK + DSA_KV_W],
              z[:, Z_DV:Z_DV + DSA_KV_W], z[:, Z_SM + SM_IK:Z_SM + SM_IK + IDX_DIM], conv_new)
    return y, states


def _reorder_w_in(w_in):
    d = w_in.shape[0]
    sizes = (FOX_W, FOX_W, FOX_W, H_FOX, DSA_W, DSA_KV_W, DSA_KV_W, IQ_W, IDX_DIM, H_IDX)
    offs = [0]
    for s in sizes:
        offs.append(offs[-1] + s)
    fq, fk, fv, fg, dq, dk, dv, iq, ik, iw = [w_in[:, offs[i]:offs[i + 1]].astype(BF16) for i in range(len(sizes))]
    pad = jnp.zeros((d, Z_W - (Z_SM + IDX_DIM + H_FOX + H_IDX)), BF16)
    return jnp.concatenate([fq, fk, fv, dq, dk, dv, iq, ik, fg, iw, pad], axis=1)


def kernel(x_prompt, x_sample, c_prompt, c_sample, page_table, cache_fox_k, cache_fox_v, cache_fox_logf, cache_dsa_k, cache_dsa_v, cache_idx_k, state_conv, w_in, b_f, w_out, g_fox_out, g_dsa_out, w_mod, b_mod, g_pre_attn, g_post_attn, g_pre_ffn, g_post_ffn, w_gate, w_up, w_conv, b_conv, w_down):
    depth = w_in.shape[0]
    bp, s_len, d = x_prompt.shape
    bs, t_new, _ = x_sample.shape
    assert bp == 1 and t_new >= CONV_W - 1 and 16 % t_new == 0

    xp = x_prompt.reshape(s_len, d)
    xs = x_sample.reshape(bs * t_new, d)
    n_c = bp + bs
    c_all = jnp.pad(jnp.concatenate([c_prompt, c_sample], axis=0), ((0, -n_c % 8), (0, 0)))
    n_phys = cache_fox_k.shape[1]
    fold = lambda c: c.reshape((depth * n_phys,) + c.shape[2:])
    rows = lambda c: c.reshape(depth * n_phys, PAGE_SIZE * c.shape[3], HEAD_DIM)
    caches = (rows(cache_fox_k), rows(cache_fox_v), fold(cache_fox_logf).transpose(0, 2, 1), rows(cache_dsa_k),
              rows(cache_dsa_v), fold(cache_idx_k))
    p_states, s_states = [], []
    for l in range(depth):
        p = dict(w_in=_reorder_w_in(w_in[l]), b_f=b_f[l], w_out=w_out[l].astype(BF16),
                 g_fox_out=g_fox_out[l][None], g_dsa_out=g_dsa_out[l][None],
                 g_pre_attn=g_pre_attn[l][None], g_post_attn=g_post_attn[l][None],
                 g_pre_ffn=g_pre_ffn[l][None], g_post_ffn=g_post_ffn[l][None],
                 w_gate=w_gate[l].astype(BF16), w_up=w_up[l].astype(BF16), w_conv=w_conv[l], b_conv=b_conv[l],
                 w_down=w_down[l].astype(BF16))
        mod = _modulation(c_all, w_mod[l], b_mod[l])
        mods_p = _split_mod(mod[:bp])
        mods_s = [jnp.repeat(v, t_new, axis=0) for v in _split_mod(mod[bp:n_c])]
        xp, st_p = _prompt_layer(xp, mods_p, p)
        xs, st_s = _decode_layer(xs, mods_s, p, page_table + l * n_phys, caches, state_conv[l], t_new)
        p_states.append(st_p)
        s_states.append(st_s)

    def stack(states, i, shape):
        return jnp.stack([st[i].reshape(shape) for st in states])

    d_ff = state_conv.shape[-1]
    outs = [xp.reshape(bp, s_len, d), xs.reshape(bs, t_new, d)]
    for states, (bb, tt) in ((p_states, (bp, s_len)), (s_states, (bs, t_new))):
        outs += [stack(states, 0, (bb, tt, H_FOX, HEAD_DIM)), stack(states, 1, (bb, tt, H_FOX, HEAD_DIM)),
                 stack(states, 2, (bb, tt, H_FOX)), stack(states, 3, (bb, tt, H_DSA_KV, HEAD_DIM)),
                 stack(states, 4, (bb, tt, H_DSA_KV, HEAD_DIM)), stack(states, 5, (bb, tt, IDX_DIM)),
                 stack(states, 6, (bb, CONV_W - 1, d_ff))]
    return tuple(outs)
```

```python
import functools

import jax
import jax.numpy as jnp
import numpy as np
from jax import lax
from jax.experimental import pallas as pl
from jax.experimental.pallas import tpu as pltpu

HEAD_DIM = 128
H_FOX = 16
H_DSA = 16
H_DSA_KV = 4
DSA_GROUP = H_DSA // H_DSA_KV
H_IDX = 16
IDX_DIM = 64
TOPK_MAX = 256
CONV_W = 3
PAGE_SIZE = 128
RMS_EPS = 1e-6
FOX_W = H_FOX * HEAD_DIM
DSA_W = H_DSA * HEAD_DIM
DSA_KV_W = H_DSA_KV * HEAD_DIM
IQ_W = H_IDX * IDX_DIM

Z_FQ = 0
Z_FK = Z_FQ + FOX_W
Z_FV = Z_FK + FOX_W
Z_DQ = Z_FV + FOX_W
Z_DK = Z_DQ + DSA_W
Z_DV = Z_DK + DSA_KV_W
Z_IQ = Z_DV + DSA_KV_W
Z_SM = Z_IQ + IQ_W
SM_IK = 0
SM_FG = IDX_DIM
SM_IW = SM_FG + H_FOX
Z_USED = Z_SM + 128
Z_TN = 768
Z_W = -(-Z_USED // Z_TN) * Z_TN

NEG = -1e30
LOG2E = 1.4426950408889634
VT_ROWS = HEAD_DIM + 16
INT_MIN = -2 ** 31
MIB = 1024 * 1024
BF16 = jnp.bfloat16
F32 = jnp.float32

_NT = (((1,), (1,)), ((), ()))


def _cparams(sem, vmem_mib=48):
    return pltpu.CompilerParams(dimension_semantics=sem, vmem_limit_bytes=vmem_mib * MIB)


def _rms(x, g):
    return x * lax.rsqrt(jnp.mean(x * x, axis=-1, keepdims=True) + RMS_EPS) * g


def _sigmoid(x):
    return 1.0 / (1.0 + jnp.exp(-x))


def _sort_key(x):
    b = pltpu.bitcast(x, jnp.int32)
    return b ^ ((b >> 31) & jnp.int32(0x7FFFFFFF))


def _row_spec(arr, tm):
    d = arr.shape[1]
    if arr.shape[0] == 1:
        return pl.BlockSpec((1, d), lambda i: (0, 0))
    return pl.BlockSpec((tm, d), lambda i: (i, 0))


def _mod_kernel(c_ref, w_ref, b_ref, o_ref):
    c = c_ref[...]
    a = (c * _sigmoid(c)).astype(BF16)
    o_ref[...] = jnp.dot(a, w_ref[...].astype(BF16), preferred_element_type=F32) + b_ref[...]


def _modulation(c, w_mod, b_mod):
    r, d = c.shape
    n = w_mod.shape[1]
    tn = 512
    return pl.pallas_call(
        _mod_kernel,
        grid=(n // tn,),
        in_specs=[pl.BlockSpec((r, d), lambda j: (0, 0)),
                  pl.BlockSpec((d, tn), lambda j: (0, j)),
                  pl.BlockSpec((1, tn), lambda j: (0, j))],
        out_specs=pl.BlockSpec((r, tn), lambda j: (0, j)),
        out_shape=jax.ShapeDtypeStruct((r, n), F32),
        compiler_params=_cparams(("parallel",)),
        name="modulation",
    )(c, w_mod, b_mod.reshape(1, n))


def _prenorm_kernel(x_ref, g_ref, sc_ref, sh_ref, o_ref):
    y = _rms(x_ref[...], g_ref[...])
    o_ref[...] = (y * (1.0 + sc_ref[...]) + sh_ref[...]).astype(o_ref.dtype)


def _prenorm(x, g, sc, sh, tm=256):
    m, d = x.shape
    tm = min(tm, m)
    return pl.pallas_call(
        _prenorm_kernel,
        grid=(m // tm,),
        in_specs=[pl.BlockSpec((tm, d), lambda i: (i, 0)), _row_spec(g, tm), _row_spec(sc, tm), _row_spec(sh, tm)],
        out_specs=pl.BlockSpec((tm, d), lambda i: (i, 0)),
        out_shape=jax.ShapeDtypeStruct((m, d), BF16),
        compiler_params=_cparams(("parallel",)),
        name="prenorm",
    )(x, g, sc, sh)


def _attn_norm_kernel(of_ref, od_ref, gf_ref, gd_ref, o_ref):
    o_ref[:, :FOX_W] = _rms(of_ref[...], gf_ref[...]).astype(o_ref.dtype)
    o_ref[:, FOX_W:] = _rms(od_ref[...], gd_ref[...]).astype(o_ref.dtype)


def _attn_norm(o_f, o_d, g_f, g_d, tm=256):
    m = o_f.shape[0]
    tm = min(tm, m)
    return pl.pallas_call(
        _attn_norm_kernel,
        grid=(m // tm,),
        in_specs=[pl.BlockSpec((tm, FOX_W), lambda i: (i, 0)), pl.BlockSpec((tm, DSA_W), lambda i: (i, 0)),
                  _row_spec(g_f, tm), _row_spec(g_d, tm)],
        out_specs=pl.BlockSpec((tm, FOX_W + DSA_W), lambda i: (i, 0)),
        out_shape=jax.ShapeDtypeStruct((m, FOX_W + DSA_W), BF16),
        compiler_params=_cparams(("parallel",)),
        name="attn_norm",
    )(o_f, o_d, g_f, g_d)


def _res_pre_kernel(o_ref, x_ref, gt_ref, gpost_ref, gpre_ref, sc_ref, sh_ref, x1_ref, h2_ref):
    x1 = x_ref[...] + gt_ref[...] * _rms(o_ref[...], gpost_ref[...])
    x1_ref[...] = x1
    h2_ref[...] = (_rms(x1, gpre_ref[...]) * (1.0 + sc_ref[...]) + sh_ref[...]).astype(h2_ref.dtype)


def _res_pre(o, x, gt, g_post, g_pre, sc, sh, tm=128):
    m, d = x.shape
    tm = min(tm, m)
    blk =pl.BlockSpec((tm, d), lambda i: (i, 0))
    return pl.pallas_call(
        _res_pre_kernel,
        grid=(m // tm,),
        in_specs=[blk, blk, _row_spec(gt, tm), _row_spec(g_post, tm), _row_spec(g_pre, tm),
                  _row_spec(sc, tm), _row_spec(sh, tm)],
        out_specs=[blk, blk],
        out_shape=[jax.ShapeDtypeStruct((m, d), F32), jax.ShapeDtypeStruct((m, d), BF16)],
        compiler_params=_cparams(("parallel",)),
        name="res_pre",
    )(o, x, gt, g_post, g_pre, sc, sh)


def _res_kernel(f_ref, x_ref, gt_ref, gpost_ref, y_ref):
    y_ref[...] = x_ref[...] + gt_ref[...] * _rms(f_ref[...], gpost_ref[...])


def _res(f, x, gt, g_post, tm=256):
    m, d = x.shape
    tm = min(tm, m)
    blk =pl.BlockSpec((tm, d), lambda i: (i, 0))
    return pl.pallas_call(
        _res_kernel,
        grid=(m // tm,),
        in_specs=[blk, blk, _row_spec(gt, tm), _row_spec(g_post, tm)],
        out_specs=blk,
        out_shape=jax.ShapeDtypeStruct((m, d), F32),
        compiler_params=_cparams(("parallel",)),
        name="res",
    )(f, x, gt, g_post)


def _mm_kernel(a_ref, w_ref, o_ref):
    o_ref[...] = jnp.dot(a_ref[...], w_ref[...], preferred_element_type=F32)


def _matmul(a, w, tm, tn, name):
    m, k = a.shape
    n = w.shape[1]
    tm = min(tm, m)
    return pl.pallas_call(
        _mm_kernel,
        grid=(m // tm, n // tn),
        in_specs=[pl.BlockSpec((tm, k), lambda i, j: (i, 0)), pl.BlockSpec((k, tn), lambda i, j: (0, j))],
        out_specs=pl.BlockSpec((tm, tn), lambda i, j: (i, j)),
        out_shape=jax.ShapeDtypeStruct((m, n), F32),
        compiler_params=_cparams(("parallel", "parallel")),
        name=name,
    )(a, w)


def _mmk_kernel(a_ref, w_ref, o_ref):
    @pl.when(pl.program_id(2) == 0)
    def _():
        o_ref[...] = jnp.zeros_like(o_ref)

    o_ref[...] += jnp.dot(a_ref[...], w_ref[...], preferred_element_type=F32)


def _matmul_ksplit(a, w, tm, tn, tk, name):
    m, k = a.shape
    n = w.shape[1]
    tm = min(tm, m)
    return pl.pallas_call(
        _mmk_kernel,
        grid=(m // tm, n // tn, k // tk),
        in_specs=[pl.BlockSpec((tm, tk), lambda i, j, kk: (i, kk)), pl.BlockSpec((tk, tn), lambda i, j, kk: (kk, j))],
        out_specs=pl.BlockSpec((tm, tn), lambda i, j, kk: (i, j)),
        out_shape=jax.ShapeDtypeStruct((m, n), F32),
        compiler_params=_cparams(("parallel", "parallel", "arbitrary")),
        name=name,
    )(a, w)


def _logf_kernel(zs_ref, bf_ref, logf_ref, cum_ref, carry_ref, *, tm):
    @pl.when(pl.program_id(0) == 0)
    def _():
        carry_ref[...] = jnp.zeros_like(carry_ref)

    x = zs_ref[:, SM_FG:SM_FG + H_FOX] + bf_ref[...]
    lf = jnp.minimum(x, 0.0) - jnp.log1p(jnp.exp(-jnp.abs(x)))
    logf_ref[...] = lf
    row = lax.broadcasted_iota(jnp.int32, (tm, tm), 0)
    col = lax.broadcasted_iota(jnp.int32, (tm, tm), 1)
    tri = (col <= row).astype(F32)
    cum = jnp.dot(tri, lf, preferred_element_type=F32, precision=lax.Precision.HIGHEST) + carry_ref[...]
    cum_ref[...] = cum
    carry_ref[...] = cum[tm - 1:tm, :]


def _logf_cumsum(z, b_f, tm=256):
    m = z.shape[0]
    tm = min(tm, m)
    out = jax.ShapeDtypeStruct((m, H_FOX), F32)
    return pl.pallas_call(
        functools.partial(_logf_kernel, tm=tm),
        grid=(m // tm,),
        in_specs=[pl.BlockSpec((tm, 128), lambda i: (i, Z_SM // 128)), pl.BlockSpec((1, H_FOX), lambda i: (0, 0))],
        out_specs=[pl.BlockSpec((tm, H_FOX), lambda i: (i, 0)), pl.BlockSpec((tm, H_FOX), lambda i: (i, 0))],
        out_shape=[out, out],
        scratch_shapes=[pltpu.VMEM((1, H_FOX), F32)],
        compiler_params=_cparams(("arbitrary",)),
        name="logf_cumsum",
    )(z, b_f.reshape(1, H_FOX))


def _split3(x):
    hi = x.astype(BF16).astype(F32)
    mid = (x - hi).astype(BF16).astype(F32)
    lo = (x - hi - mid).astype(BF16).astype(F32)
    return hi, mid, lo


def _lane_row(shape, values):
    lane = lax.broadcasted_iota(jnp.int32, shape, len(shape) - 1)
    out = jnp.zeros(shape, F32)
    for i, v in enumerate(values):
        out = jnp.where(lane == i, v, out)
    return out


def _prep_kernel(fk_ref, fv_ref, dk_ref, dv_ref, iq_ref, sm_ref, cum_ref,
                 fka_ref, fvt_ref, dka_ref, dvt_ref, iqs_ref, ikb_ref, sgn_ref, *, tm):
    i = pl.program_id(0)
    ones_rows = jnp.where(lax.broadcasted_iota(jnp.int32, (VT_ROWS - HEAD_DIM, tm), 0) == 0, 1.0, 0.0).astype(BF16)
    nf = cum_ref[...] * (-LOG2E)
    for h in range(H_FOX):
        sl = slice(h * HEAD_DIM, (h + 1) * HEAD_DIM)
        fka_ref[h, :, :HEAD_DIM] = fk_ref[:, sl].astype(BF16)
        fka_ref[h, :, HEAD_DIM:] = _lane_row((tm, HEAD_DIM), _split3(nf[:, h:h + 1])).astype(BF16)
        fvt_ref[h, :HEAD_DIM, :] = fv_ref[:, sl].T.astype(BF16)
        fvt_ref[h, HEAD_DIM:, :] = ones_rows
    kpos = i * tm + lax.broadcasted_iota(jnp.int32, (tm, 1), 0)
    a = (kpos // 64).astype(F32)
    b = (kpos % 64).astype(F32)
    pos_aug = _lane_row((tm, HEAD_DIM), (a, a, a, b, b, b)).astype(BF16)
    for g in range(H_DSA_KV):
        sl = slice(g * HEAD_DIM, (g + 1) * HEAD_DIM)
        dka_ref[g, :, :HEAD_DIM] = dk_ref[:, sl].astype(BF16)
        dka_ref[g, :, HEAD_DIM:] = pos_aug
        dvt_ref[g, :HEAD_DIM, :] = dv_ref[:, sl].T.astype(BF16)
        dvt_ref[g, HEAD_DIM:, :] = ones_rows
    sm = sm_ref[...]
    w = sm[:, SM_IW:SM_IW + H_IDX] * (H_IDX ** -0.5)
    wabs = jnp.abs(w) * (IDX_DIM ** -0.5)
    for h in range(H_IDX):
        iqs_ref[h] = (iq_ref[:, h * IDX_DIM:(h + 1) * IDX_DIM] * wabs[:, h:h + 1]).astype(BF16)
    ikb_ref[...] = sm[:, SM_IK:SM_IK + IDX_DIM].astype(BF16)
    sgn_ref[...] = jnp.where(sm > 0, 1.0, -1.0).T[SM_IW:SM_IW + H_IDX, :]


def _attn_prep(z, cum, tm=256):
    s_len = z.shape[0]
    blk = lambda w, off: pl.BlockSpec((tm, w), lambda i: (i, off // w))
    return pl.pallas_call(
        functools.partial(_prep_kernel, tm=tm),
        grid=(s_len // tm,),
        in_specs=[blk(FOX_W, Z_FK), blk(FOX_W, Z_FV), blk(DSA_KV_W, Z_DK), blk(DSA_KV_W, Z_DV), blk(IQ_W, Z_IQ),
                  blk(128, Z_SM), pl.BlockSpec((tm, H_FOX), lambda i: (i, 0))],
        out_specs=[pl.BlockSpec((H_FOX, tm, 2 * HEAD_DIM), lambda i: (0, i, 0)),
                   pl.BlockSpec((H_FOX, VT_ROWS, tm), lambda i: (0, 0, i)),
                   pl.BlockSpec((H_DSA_KV, tm, 2 * HEAD_DIM), lambda i: (0, i, 0)),
                   pl.BlockSpec((H_DSA_KV, VT_ROWS, tm), lambda i: (0, 0, i)),
                   pl.BlockSpec((H_IDX, tm, IDX_DIM), lambda i: (0, i, 0)),
                   pl.BlockSpec((tm, IDX_DIM), lambda i: (i, 0)),
                   pl.BlockSpec((H_IDX, tm), lambda i: (0, i))],
        out_shape=[jax.ShapeDtypeStruct((H_FOX, s_len, 2 * HEAD_DIM), BF16),
                   jax.ShapeDtypeStruct((H_FOX, VT_ROWS, s_len), BF16),
                   jax.ShapeDtypeStruct((H_DSA_KV, s_len, 2 * HEAD_DIM), BF16),
                   jax.ShapeDtypeStruct((H_DSA_KV, VT_ROWS, s_len), BF16),
                   jax.ShapeDtypeStruct((H_IDX, s_len, IDX_DIM), BF16),
                   jax.ShapeDtypeStruct((s_len, IDX_DIM), BF16),
                   jax.ShapeDtypeStruct((H_IDX, s_len), F32)],
        compiler_params=_cparams(("parallel",)),
        name="attn_prep",
    )(z, z, z, z, z, z, cum)


def _flash_update(s, vt, m_ref, acc_ref):
    _flash_accumulate(*_flash_probs(s, m_ref), vt, acc_ref)


def _flash_probs(s, m_ref):
    m_prev = m_ref[...]
    m_new = jnp.maximum(m_prev, jnp.max(s, axis=0, keepdims=True))
    m_ref[...] = m_new
    return jnp.exp2(m_prev - m_new), jnp.exp2((s - m_new).astype(BF16))


def _flash_accumulate(alpha, p, vt, acc_ref):
    acc_ref[...] = alpha * acc_ref[...] + jnp.dot(vt, p, preferred_element_type=F32)


def _flash_finish(acc):
    return (acc[:HEAD_DIM] / acc[HEAD_DIM:HEAD_DIM + 1]).T


FOX_HEADS_PER_STEP = 8


def _fox_kernel(q_ref, ka_ref, vt_ref, o_ref, qa_ref, m_ref, acc_ref, *, t):
    qi = pl.program_id(1)
    ki = pl.program_id(2)
    heads = range(FOX_HEADS_PER_STEP)

    @pl.when(ki == 0)
    def _():
        for h in heads:
            qa_ref[h, :, :HEAD_DIM] = (
                q_ref[:, h * HEAD_DIM:(h + 1) * HEAD_DIM] * (HEAD_DIM ** -0.5 * LOG2E)).astype(BF16)
            qa_ref[h, :, HEAD_DIM:] = _lane_row((t, HEAD_DIM), (1.0, 1.0, 1.0)).astype(BF16)
        m_ref[...] = jnp.full_like(m_ref, NEG)
        acc_ref[...] = jnp.zeros_like(acc_ref)

    def step(diagonal):
        logits = [lax.dot_general(ka_ref[h], qa_ref[h], _NT, preferred_element_type=F32) for h in heads]
        if diagonal:
            visible = (lax.broadcasted_iota(jnp.int32, (t, t), 0) <= lax.broadcasted_iota(jnp.int32, (t, t), 1))
            logits = [jnp.where(visible, s, NEG) for s in logits]
        probs = [_flash_probs(logits[h], m_ref.at[h]) for h in heads]
        for h in heads:
            _flash_accumulate(*probs[h], vt_ref[h], acc_ref.at[h])

    @pl.when(ki < qi)
    def _():
        step(False)

    @pl.when(ki == qi)
    def _():
        step(True)

    @pl.when(ki == pl.num_programs(2) - 1)
    def _():
        for h in heads:
            o_ref[:, h * HEAD_DIM:(h + 1) * HEAD_DIM] = _flash_finish(acc_ref[h])


def _fox_prompt(z, fka, fvt, t=512):
    s_len = z.shape[0]
    n = s_len // t
    hp = FOX_HEADS_PER_STEP
    w = hp * HEAD_DIM
    return pl.pallas_call(
        functools.partial(_fox_kernel, t=t),
        grid=(H_FOX // hp, n, n),
        in_specs=[pl.BlockSpec((t, w), lambda h, qi, ki: (qi, Z_FQ // w + h)),
                  pl.BlockSpec((hp, t, 2 * HEAD_DIM), lambda h, qi, ki: (h, jnp.minimum(ki, qi), 0)),
                  pl.BlockSpec((hp, VT_ROWS, t), lambda h, qi, ki: (h, 0, jnp.minimum(ki, qi)))],
        out_specs=pl.BlockSpec((t, w), lambda h, qi, ki: (qi, h)),
        out_shape=jax.ShapeDtypeStruct((s_len, FOX_W), F32),
        scratch_shapes=[pltpu.VMEM((hp, t, 2 * HEAD_DIM), BF16), pltpu.VMEM((hp, 1, t), F32),
                        pltpu.VMEM((hp, VT_ROWS, t), F32)],
        compiler_params=_cparams(("parallel", "parallel", "arbitrary")),
        name="fox_prompt",
    )(z, fka, fvt)


def _bisect_threshold(count_ge, n_valid, k_top):
    def cond(carry):
        it, _, cnt = carry
        unsettled = jnp.sum(jnp.where(cnt > k_top, 1, 0))
        return (it < 32) & (unsettled > 0)

    def body(carry):
        it, thr, cnt = carry
        cand = thr + lax.shift_left(jnp.int32(1), jnp.int32(31) - it)
        c = count_ge(cand)
        take = c >= k_top
        return it + 1, jnp.where(take, cand, thr), jnp.where(take, c, cnt)

    init = (jnp.int32(0), jnp.full(n_valid.shape, INT_MIN, jnp.int32), n_valid)
    return lax.while_loop(cond, body, init)[1]


def _idx_kernel(iqs_ref, sgn_ref, ik_ref, bias_ref, key_ref, *, tq, tkc, k_top):
    i = pl.program_id(0)
    s_len = key_ref.shape[0]
    n_chunks = ((i + 1) * tq + tkc - 1) // tkc
    qpos = i * tq + lax.broadcasted_iota(jnp.int32, (tkc, tq), 1)

    def score_chunk(c, carry):
        off = pl.multiple_of(c * tkc, tkc)
        kc = ik_ref[pl.ds(off, tkc), :]
        acc = jnp.zeros((tkc, tq), F32)
        for h in range(H_IDX):
            d = lax.dot_general(kc, iqs_ref[h], _NT, preferred_element_type=F32)
            acc = acc + sgn_ref[h:h + 1, :] * jnp.maximum(d, 0.0)
        kpos = off + lax.broadcasted_iota(jnp.int32, (tkc, tq), 0)
        key_ref[pl.ds(off, tkc), :] = jnp.where(kpos <= qpos, _sort_key(acc), INT_MIN)
        return carry

    lax.fori_loop(0, n_chunks, score_chunk, 0)

    def count_ge(cand):
        def body(c, cnt):
            off = pl.multiple_of(c * tq, tq)
            hit = jnp.where(key_ref[pl.ds(off, tq), :] >= cand, 1, 0)
            return cnt + jnp.sum(hit.reshape(tq // 8, 8, tq), axis=0)

        cnt = lax.fori_loop(0, i + 1, body, jnp.zeros((8, tq), jnp.int32))
        return jnp.sum(cnt, axis=0, keepdims=True)

    n_valid = i * tq + lax.broadcasted_iota(jnp.int32, (1, tq), 1) + 1
    thr = _bisect_threshold(count_ge, n_valid, k_top)
    thr = jnp.maximum(thr, INT_MIN + 1)

    def write_chunk(c, carry):
        off = pl.multiple_of(c * tkc, tkc)
        bias_ref[pl.ds(off, tkc), :] = jnp.where(key_ref[pl.ds(off, tkc), :] >= thr, 0.0, NEG).astype(BF16)
        return carry

    lax.fori_loop(0, n_chunks, write_chunk, 0)

    def fill_chunk(c, carry):
        off = pl.multiple_of(c * tkc, tkc)
        bias_ref[pl.ds(off, tkc), :] = jnp.full((tkc, tq), NEG, BF16)
        return carry

    lax.fori_loop(n_chunks, s_len // tkc, fill_chunk, 0)


def _idx_prompt(iqs, sgn_t, ikb, k_top, tq=256, tkc=256):
    s_len = ikb.shape[0]
    return pl.pallas_call(
        functools.partial(_idx_kernel, tq=tq, tkc=tkc, k_top=k_top),
        grid=(s_len // tq,),
        in_specs=[pl.BlockSpec((H_IDX, tq, IDX_DIM), lambda i: (0, i, 0)),
                  pl.BlockSpec((H_IDX, tq), lambda i: (0, i)),
                  pl.BlockSpec((s_len, IDX_DIM), lambda i: (0, 0))],
        out_specs=pl.BlockSpec((s_len, tq), lambda i: (0, i)),
        out_shape=jax.ShapeDtypeStruct((s_len, s_len), BF16),
        scratch_shapes=[pltpu.VMEM((s_len, tq), jnp.int32)],
        compiler_params=_cparams(("parallel",), 56),
        name="idx_prompt",
    )(iqs, sgn_t, ikb)


def _alibi_slope(h):
    return 2.0 ** (-8.0 * (h + 1) / H_DSA)


def _bf16_terms(x):
    out = []
    for _ in range(3):
        t = float(np.float32(x).astype(BF16))
        out.append(t)
        x = x - t
    return out


def _dsa_kernel(q_ref, ka_ref, vt_ref, bias_ref, o_ref, qa_ref, m_ref, acc_ref, *, tq, tk):
    qi = pl.program_id(0)
    ki = pl.program_id(1)
    ki_last = (qi * tq + tq - 1) // tk

    @pl.when(ki == 0)
    def _():
        for h in range(H_DSA):
            g, j = divmod(h, DSA_GROUP)
            rows = slice(j * tq, (j + 1) * tq)
            qa_ref[g, rows, :HEAD_DIM] = (
                q_ref[:, h * HEAD_DIM:(h + 1) * HEAD_DIM] * (HEAD_DIM ** -0.5 * LOG2E)).astype(BF16)
            sl = _bf16_terms(_alibi_slope(h) * LOG2E)
            qa_ref[g, rows, HEAD_DIM:] = _lane_row((tq, HEAD_DIM), [64.0 * t for t in sl] + sl).astype(BF16)
        m_ref[...] = jnp.full_like(m_ref, NEG)
        acc_ref[...] = jnp.zeros_like(acc_ref)

    @pl.when(ki <= ki_last)
    def _():
        mask = jnp.concatenate([bias_ref[...].astype(F32)] * DSA_GROUP, axis=1)
        groups = range(H_DSA_KV)
        logits = [lax.dot_general(ka_ref[g], qa_ref[g], _NT, preferred_element_type=F32) for g in groups]
        probs = [_flash_probs(logits[g] + mask, m_ref.at[g]) for g in groups]
        for g in groups:
            _flash_accumulate(*probs[g], vt_ref[g], acc_ref.at[g])

    @pl.when(ki == pl.num_programs(1) - 1)
    def _():
        for h in range(H_DSA):
            g, j = divmod(h, DSA_GROUP)
            o_ref[:, h * HEAD_DIM:(h + 1) * HEAD_DIM] = _flash_finish(acc_ref[g, :, j * tq:(j + 1) * tq])


def _dsa_prompt(z, dka, dvt, bias_t, tq=256, tk=512):
    s_len = z.shape[0]

    def kv_blk(qi, ki):
        return jnp.minimum(ki, (qi * tq + tq - 1) // tk)

    return pl.pallas_call(
        functools.partial(_dsa_kernel, tq=tq, tk=tk),
        grid=(s_len // tq, s_len // tk),
        in_specs=[pl.BlockSpec((tq, DSA_W), lambda qi, ki: (qi, Z_DQ // DSA_W)),
                  pl.BlockSpec((H_DSA_KV, tk, 2 * HEAD_DIM), lambda qi, ki: (0, kv_blk(qi, ki), 0)),
                  pl.BlockSpec((H_DSA_KV, VT_ROWS, tk), lambda qi, ki: (0, 0, kv_blk(qi, ki))),
                  pl.BlockSpec((tk, tq), lambda qi, ki: (kv_blk(qi, ki), qi))],
        out_specs=pl.BlockSpec((tq, DSA_W), lambda qi, ki: (qi, 0)),
        out_shape=jax.ShapeDtypeStruct((s_len, DSA_W), F32),
        scratch_shapes=[pltpu.VMEM((H_DSA_KV, DSA_GROUP * tq, 2 * HEAD_DIM), BF16),
                        pltpu.VMEM((H_DSA_KV, 1, DSA_GROUP * tq), F32),
                        pltpu.VMEM((H_DSA_KV, VT_ROWS, DSA_GROUP * tq), F32)],
        compiler_params=_cparams(("parallel", "arbitrary")),
        name="dsa_prompt",
    )(z, dka, dvt, bias_t)


def _dec_idx_kernel(pt_ref, *refs, n_pages, t_new):
    del pt_ref
    ik_refs = refs[:n_pages]
    lf_refs = refs[n_pages:2 * n_pages]
    iq_ref, w_ref, ikn_ref, lfn_ref, key_ref, fneg_ref = refs[2 * n_pages:]

    row = lax.broadcasted_iota(jnp.int32, (PAGE_SIZE, PAGE_SIZE), 0)
    col = lax.broadcasted_iota(jnp.int32, (PAGE_SIZE, PAGE_SIZE), 1)
    tri = (row <= col).astype(F32)
    carry = jnp.zeros((H_FOX, 1), F32)
    for p in range(n_pages + 1):
        lf = lf_refs[p][...] if p < n_pages else lfn_ref[...]
        cum = jnp.dot(lf, tri, preferred_element_type=F32, precision=lax.Precision.HIGHEST) + carry
        fneg_ref[:, p * PAGE_SIZE:(p + 1) * PAGE_SIZE] = -cum
        carry = cum[:, PAGE_SIZE - 1:PAGE_SIZE]

    w = w_ref[...] * (H_IDX ** -0.5)
    iqs = (iq_ref[...] * (jnp.abs(w) * (IDX_DIM ** -0.5))).astype(BF16)
    sgn = jnp.where(w > 0, 1.0, -1.0)
    qrow = lax.broadcasted_iota(jnp.int32, (16, PAGE_SIZE), 0) % t_new
    lane = lax.broadcasted_iota(jnp.int32, (16, PAGE_SIZE), 1)
    for p in range(n_pages + 1):
        if p < n_pages:
            kp = ik_refs[p][...].astype(BF16)
        else:
            kp = jnp.concatenate([ikn_ref[...], jnp.zeros((PAGE_SIZE - 8, IDX_DIM), F32)], axis=0).astype(BF16)
        d = lax.dot_general(iqs, kp, _NT, preferred_element_type=F32)
        sc = jnp.sum((sgn * jnp.maximum(d, 0.0)).reshape(H_IDX, 16, PAGE_SIZE), axis=0)
        key = _sort_key(sc)
        if p == n_pages:
            key = jnp.where((lane <= qrow) & (lane < t_new), key, INT_MIN)
        key_ref[:, p * PAGE_SIZE:(p + 1) * PAGE_SIZE] = key


def _dec_idx(page_table, idx_pages, logf_pages_t, iq16, w16, ik_new8, logf_new_t, t_new):
    b, n_pages = page_table.shape
    width = (n_pages + 1) * PAGE_SIZE
    page = lambda p: (lambda bi, pt: (pt[bi, p], 0, 0))
    per_b = lambda bi, pt: (bi, 0, 0)
    in_specs = ([pl.BlockSpec((None, PAGE_SIZE, IDX_DIM), page(p)) for p in range(n_pages)]
                + [pl.BlockSpec((None, H_FOX, PAGE_SIZE), page(p)) for p in range(n_pages)]
                + [pl.BlockSpec((None, H_IDX * 16, IDX_DIM), per_b), pl.BlockSpec((None, H_IDX * 16, 1), per_b),
                   pl.BlockSpec((None, 8, IDX_DIM), per_b), pl.BlockSpec((None, H_FOX, PAGE_SIZE), per_b)])
    return pl.pallas_call(
        functools.partial(_dec_idx_kernel, n_pages=n_pages, t_new=t_new),
        grid_spec=pltpu.PrefetchScalarGridSpec(
            num_scalar_prefetch=1,
            grid=(b,),
            in_specs=in_specs,
            out_specs=[pl.BlockSpec((None, 16, width), per_b), pl.BlockSpec((None, H_FOX, width), per_b)]),
        out_shape=[jax.ShapeDtypeStruct((b, 16, width), jnp.int32), jax.ShapeDtypeStruct((b, H_FOX, width), F32)],
        compiler_params=_cparams(("parallel",)),
        name="dec_idx",
    )(page_table, *([idx_pages] * n_pages), *([logf_pages_t] * n_pages), iq16, w16, ik_new8, logf_new_t)


def _dec_select_kernel(key_ref, nv_ref, sel_ref, *, k_top):
    def count_ge(cand):
        return jnp.sum(jnp.where(key_ref[...] >= cand, 1, 0), axis=-1, keepdims=True)

    thr = jnp.maximum(_bisect_threshold(count_ge, nv_ref[...], k_top), INT_MIN + 1)
    sel_ref[...] = jnp.where(key_ref[...] >= thr, 0.0, NEG)


def _dec_select(keys, n_valid, k_top, tm=256):
    rows, width = keys.shape
    tm = min(tm, rows)
    return pl.pallas_call(
        functools.partial(_dec_select_kernel, k_top=k_top),
        grid=(rows // tm,),
        in_specs=[pl.BlockSpec((tm, width), lambda i: (i, 0)), pl.BlockSpec((tm, 1), lambda i: (i, 0))],
        out_specs=pl.BlockSpec((tm, width), lambda i: (i, 0)),
        out_shape=jax.ShapeDtypeStruct((rows, width), F32),
        compiler_params=_cparams(("parallel",)),
        name="dec_select",
    )(keys, n_valid)


def _dec_attn_kernel(pt_ref, *refs, n_pages, kp, t_new):
    del pt_ref
    fk_refs, fv_refs, dk_refs, dv_refs = (refs[i * kp:(i + 1) * kp] for i in range(4))
    (fkn_ref, fvn_ref, dkn_ref, dvn_ref, qf_ref, qd_ref, fneg_ref, dbias_ref, fneg_new_ref, dbias_new_ref, slope_ref,
     of_ref, od_ref, mf_ref, lf_ref, accf_ref, md_ref, ld_ref, accd_ref, pairf_ref, paird_ref) = refs[4 * kp:]
    p_id = pl.program_id(1)
    n_steps = n_pages // kp
    rows_q = t_new * H_FOX
    wf = PAGE_SIZE * H_FOX
    wd = PAGE_SIZE * H_DSA_KV

    def pair_mask(n_keys, heads_per_key_head, n_kv):
        qh = lax.broadcasted_iota(jnp.int32, (rows_q, n_keys), 0) % H_FOX
        kh = lax.broadcasted_iota(jnp.int32, (rows_q, n_keys), 1) % n_kv
        return jnp.where(qh // heads_per_key_head == kh, 0.0, NEG)

    @pl.when(p_id == 0)
    def _():
        mf_ref[...] = jnp.full_like(mf_ref, NEG)
        lf_ref[...] = jnp.zeros_like(lf_ref)
        accf_ref[...] = jnp.zeros_like(accf_ref)
        md_ref[...] = jnp.full_like(md_ref, NEG)
        ld_ref[...] = jnp.zeros_like(ld_ref)
        accd_ref[...] = jnp.zeros_like(accd_ref)
        pairf_ref[...] = pair_mask(wf, 1, H_FOX)
        paird_ref[...] = pair_mask(wd, DSA_GROUP, H_DSA_KV)

    def attend(q_ref, k, v, bias, m_ref, l_ref, acc_ref):
        q = (q_ref[...] * (HEAD_DIM ** -0.5)).astype(BF16)
        s = lax.dot_general(q, k, _NT, preferred_element_type=F32) + bias
        m_prev = m_ref[...]
        m_new = jnp.maximum(m_prev, jnp.max(s, axis=-1, keepdims=True))
        alpha = jnp.exp(m_prev - m_new)
        p = jnp.exp(s - m_new)
        l_ref[...] = alpha * l_ref[...] + jnp.sum(p, axis=-1, keepdims=True)
        acc_ref[...] = alpha * acc_ref[...] + jnp.dot(p.astype(BF16), v, preferred_element_type=F32)
        m_ref[...] = m_new

    def sel_rows(db):
        return jnp.concatenate([jnp.broadcast_to(db[q:q + 1], (H_DSA, db.shape[1])) for q in range(t_new)], axis=0)

    def alibi(n_keys, tok0):
        tok = tok0 + lax.broadcasted_iota(jnp.int32, (1, n_keys), 1) // H_DSA_KV
        return slope_ref[...] * tok.astype(F32)

    @pl.when(p_id < n_steps)
    def _():
        cat = lambda page_refs: jnp.concatenate([r[...].astype(BF16) for r in page_refs], axis=0)
        bias = jnp.concatenate([pairf_ref[...]] * kp, axis=1) + fneg_ref[...]
        attend(qf_ref, cat(fk_refs), cat(fv_refs), bias, mf_ref, lf_ref, accf_ref)
        bias = (jnp.concatenate([paird_ref[...]] * kp, axis=1) + sel_rows(dbias_ref[...])
                + alibi(kp * wd, (p_id * kp - n_pages) * PAGE_SIZE))
        attend(qd_ref, cat(dk_refs), cat(dv_refs), bias, md_ref, ld_ref, accd_ref)

    @pl.when(p_id == n_steps)
    def _():
        def causal(n_keys, n_kv):
            q = lax.broadcasted_iota(jnp.int32, (rows_q, n_keys), 0) // H_FOX
            tok = lax.broadcasted_iota(jnp.int32, (rows_q, n_keys), 1) // n_kv
            return jnp.where(tok <= q, 0.0, NEG)

        nf = t_new * H_FOX
        bias = pairf_ref[:, :nf] + fneg_new_ref[:, :nf] + causal(nf, H_FOX)
        attend(qf_ref, fkn_ref[...].astype(BF16), fvn_ref[...].astype(BF16), bias, mf_ref, lf_ref, accf_ref)
        nd = t_new * H_DSA_KV
        bias = paird_ref[:, :nd] + sel_rows(dbias_new_ref[:, :nd]) + alibi(nd, 0)
        attend(qd_ref, dkn_ref[...].astype(BF16), dvn_ref[...].astype(BF16), bias, md_ref, ld_ref, accd_ref)
        of_ref[...] = accf_ref[...] / lf_ref[...]
        od_ref[...] = accd_ref[...] / ld_ref[...]


def _dec_attn(page_table, fox_k, fox_v, dsa_k, dsa_v, fk_new, fv_new, dk_new, dv_new, qf, qd, fneg_rows, dbias_rows,
              slopes, t_new):
    b, n_pages = page_table.shape
    kp = 4 if n_pages % 4 == 0 else 1
    n_steps = n_pages // kp
    wf = PAGE_SIZE * H_FOX
    wd = PAGE_SIZE * H_DSA_KV
    rows_q = t_new * H_FOX

    def page(j):
        return lambda bi, p, pt: (pt[bi, jnp.minimum(p, n_steps - 1) * kp + j], 0, 0)

    per_b = lambda bi, p, pt: (bi, 0, 0)
    past = lambda bi, p, pt: (bi, 0, jnp.minimum(p, n_steps - 1))
    new = lambda bi, p, pt: (bi, 0, n_pages)
    in_specs = ([pl.BlockSpec((None, wf, HEAD_DIM), page(j)) for j in range(kp)] * 2
                + [pl.BlockSpec((None, wd, HEAD_DIM), page(j)) for j in range(kp)] * 2
                + [pl.BlockSpec((None, t_new * H_FOX, HEAD_DIM), per_b)] * 2
                + [pl.BlockSpec((None, t_new * H_DSA_KV, HEAD_DIM), per_b)] * 2
                + [pl.BlockSpec((None, rows_q, HEAD_DIM), per_b)] * 2
                + [pl.BlockSpec((None, 1, kp * wf), past), pl.BlockSpec((None, 8, kp * wd), past),
                   pl.BlockSpec((None, 1, wf), new), pl.BlockSpec((None, 8, wd), new),
                   pl.BlockSpec((rows_q, 1), lambda bi, p, pt: (0, 0))])
    out = jax.ShapeDtypeStruct((b, rows_q, HEAD_DIM), F32)
    col = pltpu.VMEM((rows_q, 1), F32)
    acc = pltpu.VMEM((rows_q, HEAD_DIM), F32)
    return pl.pallas_call(
        functools.partial(_dec_attn_kernel, n_pages=n_pages, kp=kp, t_new=t_new),
        grid_spec=pltpu.PrefetchScalarGridSpec(
            num_scalar_prefetch=1,
            grid=(b, n_steps + 1),
            in_specs=in_specs,
            out_specs=[pl.BlockSpec((None, rows_q, HEAD_DIM), per_b)] * 2,
            scratch_shapes=[col, col, acc, col, col, acc, pltpu.VMEM((rows_q, wf), F32),
                            pltpu.VMEM((rows_q, wd), F32)]),
        out_shape=[out, out],
        compiler_params=_cparams(("parallel", "arbitrary")),
        name="dec_attn",
    )(page_table, *([fox_k] * kp), *([fox_v] * kp), *([dsa_k] * kp), *([dsa_v] * kp),
      fk_new, fv_new, dk_new, dv_new, qf, qd, fneg_rows, dbias_rows, fneg_rows, dbias_rows, slopes)


def _ffn_up_kernel(h_ref, hprev_ref, wg_ref, wu_ref, wc_ref, bc_ref, a_ref, tail_ref, *, tm):
    i = pl.program_id(0)
    h = h_ref[...]
    g = jnp.dot(h, wg_ref[...], preferred_element_type=F32)
    u = jnp.dot(h, wu_ref[...], preferred_element_type=F32)
    gh = jnp.dot(hprev_ref[...], wg_ref[...], preferred_element_type=F32)
    gh = jnp.where(i > 0, gh, 0.0)
    ext = jnp.concatenate([gh, g], axis=0)
    g1 = pltpu.roll(ext, 1, 0)[16:]
    g2 = pltpu.roll(ext, 2, 0)[16:]
    wc = wc_ref[...]
    conv = bc_ref[...] + wc[0:1] * g2 + wc[1:2] * g1 + wc[2:3] * g
    a_ref[...] = (conv * _sigmoid(conv) * u).astype(a_ref.dtype)
    tail_ref[...] = g[tm - 8:]


def _ffn_up_prompt(h2, w_gate, w_up, w_conv, b_conv, tm=1024, tn=256):
    m, d = h2.shape
    d_ff = w_gate.shape[1]
    tm = min(tm, m)
    return pl.pallas_call(
        functools.partial(_ffn_up_kernel, tm=tm),
        grid=(m // tm, d_ff // tn),
        in_specs=[pl.BlockSpec((tm, d), lambda i, j: (i, 0)),
                  pl.BlockSpec((16, d), lambda i, j: (jnp.maximum(i * (tm // 16) - 1, 0), 0)),
                  pl.BlockSpec((d, tn), lambda i, j: (0, j)), pl.BlockSpec((d, tn), lambda i, j: (0, j)),
                  pl.BlockSpec((CONV_W, tn), lambda i, j: (0, j)), pl.BlockSpec((1, tn), lambda i, j: (0, j))],
        out_specs=[pl.BlockSpec((tm, tn), lambda i, j: (i, j)), pl.BlockSpec((8, tn), lambda i, j: (i, j))],
        out_shape=[jax.ShapeDtypeStruct((m, d_ff), BF16), jax.ShapeDtypeStruct((m // tm * 8, d_ff), F32)],
        compiler_params=_cparams(("parallel", "parallel")),
        name="ffn_up_prompt",
    )(h2, h2, w_gate, w_up, w_conv, b_conv.reshape(1, d_ff))


def _ffn_up_dec_kernel(h_ref, wg_ref, wu_ref, wc_ref, bc_ref, s0_ref, s1_ref, a_ref, g_ref, *, t_new):
    h = h_ref[...]
    g = jnp.dot(h, wg_ref[...], preferred_element_type=F32)
    u = jnp.dot(h, wu_ref[...], preferred_element_type=F32)
    g_ref[...] = g
    t = lax.broadcasted_iota(jnp.int32, g.shape, 0) % t_new
    g1 = jnp.where(t >= 1, pltpu.roll(g, 1, 0), 0.0) + s1_ref[...]
    g2 = jnp.where(t >= 2, pltpu.roll(g, 2, 0), 0.0) + s0_ref[...]
    wc = wc_ref[...]
    conv = bc_ref[...] + wc[0:1] * g2 + wc[1:2] * g1 + wc[2:3] * g
    a_ref[...] = (conv * _sigmoid(conv) * u).astype(a_ref.dtype)


def _ffn_up_dec(h2, w_gate, w_up, w_conv, b_conv, tap0, tap1, t_new, tn=256):
    m, d = h2.shape
    d_ff = w_gate.shape[1]
    col = lambda j: (0, j)
    return pl.pallas_call(
        functools.partial(_ffn_up_dec_kernel, t_new=t_new),
        grid=(d_ff // tn,),
        in_specs=[pl.BlockSpec((m, d), lambda j: (0, 0)), pl.BlockSpec((d, tn), col), pl.BlockSpec((d, tn), col),
                  pl.BlockSpec((CONV_W, tn), col), pl.BlockSpec((1, tn), col),
                  pl.BlockSpec((m, tn), col), pl.BlockSpec((m, tn), col)],
        out_specs=[pl.BlockSpec((m, tn), col), pl.BlockSpec((m, tn), col)],
        out_shape=[jax.ShapeDtypeStruct((m, d_ff), BF16), jax.ShapeDtypeStruct((m, d_ff), F32)],
        compiler_params=_cparams(("parallel",)),
        name="ffn_up_dec",
    )(h2, w_gate, w_up, w_conv, b_conv.reshape(1, d_ff), tap0, tap1)


def _split_mod(mod):
    return jnp.split(mod, 6, axis=-1)


def _attn_out_and_ffn_in(x, o_f, o_d, mods, p):
    _, _, gt1, sh2, sc2, _ = mods
    a = _attn_norm(o_f, o_d, p["g_fox_out"], p["g_dsa_out"])
    o = _matmul(a, p["w_out"], 1024, 512, "out_proj")
    return _res_pre(o, x, gt1, p["g_post_attn"], p["g_pre_ffn"], sc2, sh2)


def _ffn_down_and_res(a, x1, mods, p):
    d_ff = a.shape[1]
    f = _matmul_ksplit(a, p["w_down"], 1024, 512, d_ff // 2, "ffn_down")
    return _res(f, x1, mods[5], p["g_post_ffn"])


def _project(x, mods, p):
    sh1, sc1 = mods[0], mods[1]
    h = _prenorm(x, p["g_pre_attn"], sc1, sh1)
    return _matmul(h, p["w_in"], 1024, Z_TN, "in_proj")


def _prompt_layer(x, mods, p):
    s_len = x.shape[0]
    z = _project(x, mods, p)
    logf, cum = _logf_cumsum(z, p["b_f"])
    fka, fvt, dka, dvt, iqs, ikb, sgn_t = _attn_prep(z, cum)
    o_f = _fox_prompt(z, fka, fvt)
    bias_t = _idx_prompt(iqs, sgn_t, ikb, min(TOPK_MAX, s_len // 4))
    o_d = _dsa_prompt(z, dka, dvt, bias_t)
    x1, h2 = _attn_out_and_ffn_in(x, o_f, o_d, mods, p)
    a, tails = _ffn_up_prompt(h2, p["w_gate"], p["w_up"], p["w_conv"], p["b_conv"])
    y = _ffn_down_and_res(a, x1, mods, p)
    states = (z[:, Z_FK:Z_FK + FOX_W], z[:, Z_FV:Z_FV + FOX_W], logf, z[:, Z_DK:Z_DK + DSA_KV_W],
              z[:, Z_DV:Z_DV + DSA_KV_W], z[:, Z_SM + SM_IK:Z_SM + SM_IK + IDX_DIM], tails[-(CONV_W - 1):])
    return y, states


def _pad_rows(x, rows):
    return jnp.pad(x, ((0, 0), (0, rows - x.shape[1]), (0, 0)))


def _decode_layer(x, mods, p, page_table, caches, state_conv, t_new):
    m = x.shape[0]
    b = m // t_new
    n_pages = page_table.shape[1]
    cache_fox_k, cache_fox_v, cache_fox_logf_t, cache_dsa_k, cache_dsa_v, cache_idx_k = caches
    z = _project(x, mods, p)
    logf, _ = _logf_cumsum(z, p["b_f"])
    z3 = z.reshape(b, t_new, Z_W)

    rep = 16 // t_new
    iq16 = jnp.tile(z3[:, :, Z_IQ:Z_IQ + IQ_W].reshape(b, t_new, H_IDX, IDX_DIM).transpose(0, 2, 1, 3),
                    (1, 1, rep, 1)).reshape(b, H_IDX * 16, IDX_DIM)
    w16 = jnp.tile(z3[:, :, Z_SM + SM_IW:Z_SM + SM_IW + H_IDX].transpose(0, 2, 1), (1, 1, rep)).reshape(
        b, H_IDX * 16, 1)
    ik_new8 = _pad_rows(z3[:, :, Z_SM + SM_IK:Z_SM + SM_IK + IDX_DIM], 8)
    logf_new_t = jnp.pad(logf.reshape(b, t_new, H_FOX).transpose(0, 2, 1), ((0, 0), (0, 0), (0, PAGE_SIZE - t_new)))
    k_top = min(TOPK_MAX, (n_pages * PAGE_SIZE + t_new) // 4)
    keys, fneg = _dec_idx(page_table, cache_idx_k, cache_fox_logf_t, iq16, w16, ik_new8, logf_new_t, t_new)
    n_valid = jnp.tile(n_pages * PAGE_SIZE + 1 + jnp.arange(8, dtype=jnp.int32) % t_new, b).reshape(b * 8, 1)
    dbias = _dec_select(keys[:, :8].reshape(b * 8, -1), n_valid, k_top).reshape(b, 8, -1)

    heads = lambda off, w: z3[:, :, off:off + w].reshape(b, t_new * (w // HEAD_DIM), HEAD_DIM)
    fneg_rows = fneg.transpose(0, 2, 1).reshape(b, 1, -1)
    dbias_rows = jnp.repeat(dbias, H_DSA_KV, axis=-1)
    slopes = jnp.tile(2.0 ** (-8.0 * jnp.arange(1, H_DSA + 1, dtype=F32) / H_DSA), t_new).reshape(H_DSA * t_new, 1)
    o_f, o_d = _dec_attn(
        page_table, cache_fox_k, cache_fox_v, cache_dsa_k, cache_dsa_v,
        heads(Z_FK, FOX_W), heads(Z_FV, FOX_W), heads(Z_DK, DSA_KV_W), heads(Z_DV, DSA_KV_W),
        heads(Z_FQ, FOX_W), heads(Z_DQ, DSA_W), fneg_rows, dbias_rows, slopes, t_new)

    x1, h2 = _attn_out_and_ffn_in(x, o_f.reshape(m, FOX_W), o_d.reshape(m, DSA_W), mods, p)
    d_ff = state_conv.shape[-1]
    zero = jnp.zeros((b, 1, d_ff), F32)
    s0, s1 = state_conv[:, 0:1], state_conv[:, 1:2]
    tap0 = jnp.concatenate([s0, s1] + [zero] * (t_new - 2), axis=1).reshape(m, d_ff)
    tap1 = jnp.concatenate([s1] + [zero] * (t_new - 1), axis=1).reshape(m, d_ff)
    a, g = _ffn_up_dec(h2, p["w_gate"], p["w_up"], p["w_conv"], p["b_conv"], tap0, tap1, t_new)
    y = _ffn_down_and_res(a, x1, mods, p)
    conv_new = g.reshape(b, t_new, d_ff)[:, t_new - (CONV_W - 1):]
    states = (z[:, Z_FK:Z_FK + FOX_W], z[:, Z_FV:Z_FV + FOX_W], logf, z[:, Z_DK:Z_DK + DSA_KV_W],
              z[:, Z_DV:Z_DV + DSA_KV_W], z[:, Z_SM + SM_IK:Z_SM + SM_IK + IDX_DIM], conv_new)
    return y, states


def _reorder_w_in(w_in):
    d = w_in.shape[0]
    sizes = (FOX_W, FOX_W, FOX_W, H_FOX, DSA_W, DSA_KV_W, DSA_KV_W, IQ_W, IDX_DIM, H_IDX)
    offs = [0]
    for s in sizes:
        offs.append(offs[-1] + s)
    fq, fk, fv, fg, dq, dk, dv, iq, ik, iw = [w_in[:, offs[i]:offs[i + 1]].astype(BF16) for i in range(len(sizes))]
    pad = jnp.zeros((d, Z_W - (Z_SM + IDX_DIM + H_FOX + H_IDX)), BF16)
    return jnp.concatenate([fq, fk, fv, dq, dk, dv, iq, ik, fg, iw, pad], axis=1)


def kernel(x_prompt, x_sample, c_prompt, c_sample, page_table, cache_fox_k, cache_fox_v, cache_fox_logf, cache_dsa_k, cache_dsa_v, cache_idx_k, state_conv, w_in, b_f, w_out, g_fox_out, g_dsa_out, w_mod, b_mod, g_pre_attn, g_post_attn, g_pre_ffn, g_post_ffn, w_gate, w_up, w_conv, b_conv, w_down):
    depth = w_in.shape[0]
    bp, s_len, d = x_prompt.shape
    bs, t_new, _ = x_sample.shape
    assert bp == 1 and t_new >= CONV_W - 1 and 16 % t_new == 0

    xp = x_prompt.reshape(s_len, d)
    xs = x_sample.reshape(bs * t_new, d)
    n_c = bp + bs
    c_all = jnp.pad(jnp.concatenate([c_prompt, c_sample], axis=0), ((0, -n_c % 8), (0, 0)))
    n_phys = cache_fox_k.shape[1]
    fold = lambda c: c.reshape((depth * n_phys,) + c.shape[2:])
    rows = lambda c: c.reshape(depth * n_phys, PAGE_SIZE * c.shape[3], HEAD_DIM)
    caches = (rows(cache_fox_k), rows(cache_fox_v), fold(cache_fox_logf).transpose(0, 2, 1), rows(cache_dsa_k),
              rows(cache_dsa_v), fold(cache_idx_k))
    p_states, s_states = [], []
    for l in range(depth):
        p = dict(w_in=_reorder_w_in(w_in[l]), b_f=b_f[l], w_out=w_out[l].astype(BF16),
                 g_fox_out=g_fox_out[l][None], g_dsa_out=g_dsa_out[l][None],
                 g_pre_attn=g_pre_attn[l][None], g_post_attn=g_post_attn[l][None],
                 g_pre_ffn=g_pre_ffn[l][None], g_post_ffn=g_post_ffn[l][None],
                 w_gate=w_gate[l].astype(BF16), w_up=w_up[l].astype(BF16), w_conv=w_conv[l], b_conv=b_conv[l],
                 w_down=w_down[l].astype(BF16))
        mod = _modulation(c_all, w_mod[l], b_mod[l])
        mods_p = _split_mod(mod[:bp])
        mods_s = [jnp.repeat(v, t_new, axis=0) for v in _split_mod(mod[bp:n_c])]
        xp, st_p = _prompt_layer(xp, mods_p, p)
        xs, st_s = _decode_layer(xs, mods_s, p, page_table + l * n_phys, caches, state_conv[l], t_new)
        p_states.append(st_p)
        s_states.append(st_s)

    def stack(states, i, shape):
        return jnp.stack([st[i].reshape(shape) for st in states])

    d_ff = state_conv.shape[-1]
    outs = [xp.reshape(bp, s_len, d), xs.reshape(bs, t_new, d)]
    for states, (bb, tt) in ((p_states, (bp, s_len)), (s_states, (bs, t_new))):
        outs += [stack(states, 0, (bb, tt, H_FOX, HEAD_DIM)), stack(states, 1, (bb, tt, H_FOX, HEAD_DIM)),
                 stack(states, 2, (bb, tt, H_FOX)), stack(states, 3, (bb, tt, H_DSA_KV, HEAD_DIM)),
                 stack(states, 4, (bb, tt, H_DSA_KV, HEAD_DIM)), stack(states, 5, (bb, tt, IDX_DIM)),
                 stack(states, 6, (bb, CONV_W - 1, d_ff))]
    return tuple(outs)
```

```python
import functools

import jax
import jax.numpy as jnp
import numpy as np
from jax import lax
from jax.experimental import pallas as pl
from jax.experimental.pallas import tpu as pltpu

HEAD_DIM = 128
H_FOX = 16
H_DSA = 16
H_DSA_KV = 4
DSA_GROUP = H_DSA // H_DSA_KV
H_IDX = 16
IDX_DIM = 64
TOPK_MAX = 256
CONV_W = 3
PAGE_SIZE = 128
RMS_EPS = 1e-6
FOX_W = H_FOX * HEAD_DIM
DSA_W = H_DSA * HEAD_DIM
DSA_KV_W = H_DSA_KV * HEAD_DIM
IQ_W = H_IDX * IDX_DIM

Z_FQ = 0
Z_FK = Z_FQ + FOX_W
Z_FV = Z_FK + FOX_W
ZF_W = Z_FV + FOX_W
ZF_TN = 512
Z_DQ = 0
Z_DK = Z_DQ + DSA_W
Z_DV = Z_DK + DSA_KV_W
Z_IQ = Z_DV + DSA_KV_W
Z_SM = Z_IQ + IQ_W
SM_IK = 0
SM_IW = SM_IK + IDX_DIM
SM_FG = SM_IW + H_IDX
ZD_TN = 768
ZD_W = -(-(Z_SM + 128) // ZD_TN) * ZD_TN

NEG = -1e30
LOG2E = 1.4426950408889634
VT_ROWS = HEAD_DIM + 16
INT_MIN = -2 ** 31
MIB = 1024 * 1024
BF16 = jnp.bfloat16
F32 = jnp.float32

_NT = (((1,), (1,)), ((), ()))


def _cparams(sem, vmem_mib=48):
    return pltpu.CompilerParams(dimension_semantics=sem, vmem_limit_bytes=vmem_mib * MIB)


def _rms(x, g):
    return x * lax.rsqrt(jnp.mean(x * x, axis=-1, keepdims=True) + RMS_EPS) * g


def _sigmoid(x):
    return 1.0 / (1.0 + jnp.exp(-x))


def _sort_key(x):
    b = pltpu.bitcast(x, jnp.int32)
    return b ^ ((b >> 31) & jnp.int32(0x7FFFFFFF))


def _row_spec(arr, tm):
    d = arr.shape[1]
    if arr.shape[0] == 1:
        return pl.BlockSpec((1, d), lambda i: (0, 0))
    return pl.BlockSpec((tm, d), lambda i: (i, 0))


def _mod_kernel(c_ref, w_ref, b_ref, o_ref):
    c = c_ref[...]
    a = (c * _sigmoid(c)).astype(BF16)
    o_ref[...] = jnp.dot(a, w_ref[...].astype(BF16), preferred_element_type=F32) + b_ref[...]


def _modulation(c, w_mod, b_mod):
    r, d = c.shape
    n = w_mod.shape[1]
    tn = 512
    return pl.pallas_call(
        _mod_kernel,
        grid=(n // tn,),
        in_specs=[pl.BlockSpec((r, d), lambda j: (0, 0)),
                  pl.BlockSpec((d, tn), lambda j: (0, j)),
                  pl.BlockSpec((1, tn), lambda j: (0, j))],
        out_specs=pl.BlockSpec((r, tn), lambda j: (0, j)),
        out_shape=jax.ShapeDtypeStruct((r, n), F32),
        compiler_params=_cparams(("parallel",)),
        name="modulation",
    )(c, w_mod, b_mod.reshape(1, n))


def _prenorm_kernel(x_ref, g_ref, sc_ref, sh_ref, o_ref):
    y = _rms(x_ref[...], g_ref[...])
    o_ref[...] = (y * (1.0 + sc_ref[...]) + sh_ref[...]).astype(o_ref.dtype)


def _prenorm(x, g, sc, sh, tm=256):
    m, d = x.shape
    tm = min(tm, m)
    return pl.pallas_call(
        _prenorm_kernel,
        grid=(m // tm,),
        in_specs=[pl.BlockSpec((tm, d), lambda i: (i, 0)), _row_spec(g, tm), _row_spec(sc, tm), _row_spec(sh, tm)],
        out_specs=pl.BlockSpec((tm, d), lambda i: (i, 0)),
        out_shape=jax.ShapeDtypeStruct((m, d), BF16),
        compiler_params=_cparams(("parallel",)),
        name="prenorm",
    )(x, g, sc, sh)


def _attn_norm_kernel(of_ref, od_ref, gf_ref, gd_ref, o_ref):
    o_ref[:, :FOX_W] = _rms(of_ref[...], gf_ref[...]).astype(o_ref.dtype)
    o_ref[:, FOX_W:] = _rms(od_ref[...], gd_ref[...]).astype(o_ref.dtype)


def _attn_norm(o_f, o_d, g_f, g_d, tm=256):
    m = o_f.shape[0]
    tm = min(tm, m)
    return pl.pallas_call(
        _attn_norm_kernel,
        grid=(m // tm,),
        in_specs=[pl.BlockSpec((tm, FOX_W), lambda i: (i, 0)), pl.BlockSpec((tm, DSA_W), lambda i: (i, 0)),
                  _row_spec(g_f, tm), _row_spec(g_d, tm)],
        out_specs=pl.BlockSpec((tm, FOX_W + DSA_W), lambda i: (i, 0)),
        out_shape=jax.ShapeDtypeStruct((m, FOX_W + DSA_W), BF16),
        compiler_params=_cparams(("parallel",)),
        name="attn_norm",
    )(o_f, o_d, g_f, g_d)


def _res_pre_kernel(o_ref, x_ref, gt_ref, gpost_ref, gpre_ref, sc_ref, sh_ref, x1_ref, h2_ref):
    x1 = x_ref[...] + gt_ref[...] * _rms(o_ref[...], gpost_ref[...])
    x1_ref[...] = x1
    h2_ref[...] = (_rms(x1, gpre_ref[...]) * (1.0 + sc_ref[...]) + sh_ref[...]).astype(h2_ref.dtype)


def _res_pre(o, x, gt, g_post, g_pre, sc, sh, tm=128):
    m, d = x.shape
    tm = min(tm, m)
    blk =pl.BlockSpec((tm, d), lambda i: (i, 0))
    return pl.pallas_call(
        _res_pre_kernel,
        grid=(m // tm,),
        in_specs=[blk, blk, _row_spec(gt, tm), _row_spec(g_post, tm), _row_spec(g_pre, tm),
                  _row_spec(sc, tm), _row_spec(sh, tm)],
        out_specs=[blk, blk],
        out_shape=[jax.ShapeDtypeStruct((m, d), F32), jax.ShapeDtypeStruct((m, d), BF16)],
        compiler_params=_cparams(("parallel",)),
        name="res_pre",
    )(o, x, gt, g_post, g_pre, sc, sh)


def _res_kernel(f_ref, x_ref, gt_ref, gpost_ref, y_ref):
    y_ref[...] = x_ref[...] + gt_ref[...] * _rms(f_ref[...], gpost_ref[...])


def _res(f, x, gt, g_post, tm=256):
    m, d = x.shape
    tm = min(tm, m)
    blk =pl.BlockSpec((tm, d), lambda i: (i, 0))
    return pl.pallas_call(
        _res_kernel,
        grid=(m // tm,),
        in_specs=[blk, blk, _row_spec(gt, tm), _row_spec(g_post, tm)],
        out_specs=blk,
        out_shape=jax.ShapeDtypeStruct((m, d), F32),
        compiler_params=_cparams(("parallel",)),
        name="res",
    )(f, x, gt, g_post)


def _mm_kernel(a_ref, w_ref, o_ref):
    o_ref[...] = jnp.dot(a_ref[...], w_ref[...], preferred_element_type=F32)


def _matmul(a, w, tm, tn, name):
    m, k = a.shape
    n = w.shape[1]
    tm = min(tm, m)
    return pl.pallas_call(
        _mm_kernel,
        grid=(m // tm, n // tn),
        in_specs=[pl.BlockSpec((tm, k), lambda i, j: (i, 0)), pl.BlockSpec((k, tn), lambda i, j: (0, j))],
        out_specs=pl.BlockSpec((tm, tn), lambda i, j: (i, j)),
        out_shape=jax.ShapeDtypeStruct((m, n), F32),
        compiler_params=_cparams(("parallel", "parallel")),
        name=name,
    )(a, w)


def _mm_nt_kernel(a_ref, wt_ref, o_ref):
    o_ref[...] = lax.dot_general(a_ref[...], wt_ref[...].astype(BF16), _NT, preferred_element_type=F32)


def _matmul_nt(a, w_t, tm, tn, name, n=None):
    m, k = a.shape
    n = w_t.shape[0] if n is None else n
    tm = min(tm, m)
    return pl.pallas_call(
        _mm_nt_kernel,
        grid=(m // tm, n // tn),
        in_specs=[pl.BlockSpec((tm, k), lambda i, j: (i, 0)), pl.BlockSpec((tn, k), lambda i, j: (j, 0))],
        out_specs=pl.BlockSpec((tm, tn), lambda i, j: (i, j)),
        out_shape=jax.ShapeDtypeStruct((m, n), F32),
        compiler_params=_cparams(("parallel", "parallel")),
        name=name,
    )(a, w_t)


def _mmk_kernel(a_ref, w_ref, o_ref):
    @pl.when(pl.program_id(2) == 0)
    def _():
        o_ref[...] = jnp.zeros_like(o_ref)

    o_ref[...] += jnp.dot(a_ref[...], w_ref[...], preferred_element_type=F32)


def _matmul_ksplit(a, w, tm, tn, tk, name):
    m, k = a.shape
    n = w.shape[1]
    tm = min(tm, m)
    return pl.pallas_call(
        _mmk_kernel,
        grid=(m // tm, n // tn, k // tk),
        in_specs=[pl.BlockSpec((tm, tk), lambda i, j, kk: (i, kk)), pl.BlockSpec((tk, tn), lambda i, j, kk: (kk, j))],
        out_specs=pl.BlockSpec((tm, tn), lambda i, j, kk: (i, j)),
        out_shape=jax.ShapeDtypeStruct((m, n), F32),
        compiler_params=_cparams(("parallel", "parallel", "arbitrary")),
        name=name,
    )(a, w)


def _logf_kernel(zs_ref, bf_ref, logf_ref, cum_ref, carry_ref, *, tm):
    @pl.when(pl.program_id(0) == 0)
    def _():
        carry_ref[...] = jnp.zeros_like(carry_ref)

    x = zs_ref[:, SM_FG:SM_FG + H_FOX] + bf_ref[...]
    lf = jnp.minimum(x, 0.0) - jnp.log1p(jnp.exp(-jnp.abs(x)))
    logf_ref[...] = lf
    row = lax.broadcasted_iota(jnp.int32, (tm, tm), 0)
    col = lax.broadcasted_iota(jnp.int32, (tm, tm), 1)
    tri = (col <= row).astype(F32)
    cum = jnp.dot(tri, lf, preferred_element_type=F32, precision=lax.Precision.HIGHEST) + carry_ref[...]
    cum_ref[...] = cum
    carry_ref[...] = cum[tm - 1:tm, :]


def _logf_cumsum(z, b_f, tm=256):
    m = z.shape[0]
    tm = min(tm, m)
    out = jax.ShapeDtypeStruct((m, H_FOX), F32)
    return pl.pallas_call(
        functools.partial(_logf_kernel, tm=tm),
        grid=(m // tm,),
        in_specs=[pl.BlockSpec((tm, 128), lambda i: (i, Z_SM // 128)), pl.BlockSpec((1, H_FOX), lambda i: (0, 0))],
        out_specs=[pl.BlockSpec((tm, H_FOX), lambda i: (i, 0)), pl.BlockSpec((tm, H_FOX), lambda i: (i, 0))],
        out_shape=[out, out],
        scratch_shapes=[pltpu.VMEM((1, H_FOX), F32)],
        compiler_params=_cparams(("arbitrary",)),
        name="logf_cumsum",
    )(z, b_f.reshape(1, H_FOX))


def _split3(x):
    hi = x.astype(BF16).astype(F32)
    mid = (x - hi).astype(BF16).astype(F32)
    lo = (x - hi - mid).astype(BF16).astype(F32)
    return hi, mid, lo


def _lane_row(shape, values):
    lane = lax.broadcasted_iota(jnp.int32, shape, len(shape) - 1)
    out = jnp.zeros(shape, F32)
    for i, v in enumerate(values):
        out = jnp.where(lane == i, v, out)
    return out


def _prep_kernel(fk_ref, fv_ref, dk_ref, dv_ref, iq_ref, sm_ref, cum_ref,
                 fka_ref, fvt_ref, dka_ref, dvt_ref, iqs_ref, ikb_ref, sgn_ref, *, tm):
    i = pl.program_id(0)
    ones_rows = jnp.where(lax.broadcasted_iota(jnp.int32, (VT_ROWS - HEAD_DIM, tm), 0) == 0, 1.0, 0.0).astype(BF16)
    nf = cum_ref[...] * (-LOG2E)
    for h in range(H_FOX):
        sl = slice(h * HEAD_DIM, (h + 1) * HEAD_DIM)
        fka_ref[h, :, :HEAD_DIM] = fk_ref[:, sl].astype(BF16)
        fka_ref[h, :, HEAD_DIM:] = _lane_row((tm, HEAD_DIM), _split3(nf[:, h:h + 1])).astype(BF16)
        fvt_ref[h, :HEAD_DIM, :] = fv_ref[:, sl].T.astype(BF16)
        fvt_ref[h, HEAD_DIM:, :] = ones_rows
    kpos = i * tm + lax.broadcasted_iota(jnp.int32, (tm, 1), 0)
    a = (kpos // 64).astype(F32)
    b = (kpos % 64).astype(F32)
    pos_aug = _lane_row((tm, HEAD_DIM), (a, a, a, b, b, b)).astype(BF16)
    for g in range(H_DSA_KV):
        sl = slice(g * HEAD_DIM, (g + 1) * HEAD_DIM)
        dka_ref[g, :, :HEAD_DIM] = dk_ref[:, sl].astype(BF16)
        dka_ref[g, :, HEAD_DIM:] = pos_aug
        dvt_ref[g, :HEAD_DIM, :] = dv_ref[:, sl].T.astype(BF16)
        dvt_ref[g, HEAD_DIM:, :] = ones_rows
    sm = sm_ref[...]
    w = sm[:, SM_IW:SM_IW + H_IDX] * (H_IDX ** -0.5)
    wabs = jnp.abs(w) * (IDX_DIM ** -0.5)
    for h in range(H_IDX):
        iqs_ref[h] = (iq_ref[:, h * IDX_DIM:(h + 1) * IDX_DIM] * wabs[:, h:h + 1]).astype(BF16)
    ikb_ref[...] = sm[:, SM_IK:SM_IK + IDX_DIM].astype(BF16)
    sgn_ref[...] = jnp.where(sm > 0, 1.0, -1.0).T[SM_IW:SM_IW + H_IDX, :]


def _attn_prep(zf, zd, cum, tm=256):
    s_len = zf.shape[0]
    blk = lambda w, off: pl.BlockSpec((tm, w), lambda i: (i, off // w))
    return pl.pallas_call(
        functools.partial(_prep_kernel, tm=tm),
        grid=(s_len // tm,),
        in_specs=[blk(FOX_W, Z_FK), blk(FOX_W, Z_FV), blk(DSA_KV_W, Z_DK), blk(DSA_KV_W, Z_DV), blk(IQ_W, Z_IQ),
                  blk(128, Z_SM), pl.BlockSpec((tm, H_FOX), lambda i: (i, 0))],
        out_specs=[pl.BlockSpec((H_FOX, tm, 2 * HEAD_DIM), lambda i: (0, i, 0)),
                   pl.BlockSpec((H_FOX, VT_ROWS, tm), lambda i: (0, 0, i)),
                   pl.BlockSpec((H_DSA_KV, tm, 2 * HEAD_DIM), lambda i: (0, i, 0)),
                   pl.BlockSpec((H_DSA_KV, VT_ROWS, tm), lambda i: (0, 0, i)),
                   pl.BlockSpec((H_IDX, tm, IDX_DIM), lambda i: (0, i, 0)),
                   pl.BlockSpec((tm, IDX_DIM), lambda i: (i, 0)),
                   pl.BlockSpec((H_IDX, tm), lambda i: (0, i))],
        out_shape=[jax.ShapeDtypeStruct((H_FOX, s_len, 2 * HEAD_DIM), BF16),
                   jax.ShapeDtypeStruct((H_FOX, VT_ROWS, s_len), BF16),
                   jax.ShapeDtypeStruct((H_DSA_KV, s_len, 2 * HEAD_DIM), BF16),
                   jax.ShapeDtypeStruct((H_DSA_KV, VT_ROWS, s_len), BF16),
                   jax.ShapeDtypeStruct((H_IDX, s_len, IDX_DIM), BF16),
                   jax.ShapeDtypeStruct((s_len, IDX_DIM), BF16),
                   jax.ShapeDtypeStruct((H_IDX, s_len), F32)],
        compiler_params=_cparams(("parallel",)),
        name="attn_prep",
    )(zf, zf, zd, zd, zd, zd, cum)


def _flash_update(s, vt, m_ref, acc_ref):
    _flash_accumulate(*_flash_probs(s, m_ref), vt, acc_ref)


def _flash_probs(s, m_ref):
    m_prev = m_ref[...]
    m_new = jnp.maximum(m_prev, jnp.max(s, axis=0, keepdims=True))
    m_ref[...] = m_new
    return jnp.exp2(m_prev - m_new), jnp.exp2((s - m_new).astype(BF16))


def _flash_accumulate(alpha, p, vt, acc_ref):
    acc_ref[...] = alpha * acc_ref[...] + jnp.dot(vt, p, preferred_element_type=F32)


def _flash_finish(acc):
    return (acc[:HEAD_DIM] / acc[HEAD_DIM:HEAD_DIM + 1]).T


FOX_HEADS_PER_STEP = 8


def _fox_kernel(q_ref, ka_ref, vt_ref, o_ref, qa_ref, m_ref, acc_ref, *, t):
    qi = pl.program_id(1)
    ki = pl.program_id(2)
    heads = range(FOX_HEADS_PER_STEP)

    @pl.when(ki == 0)
    def _():
        for h in heads:
            qa_ref[h, :, :HEAD_DIM] = (
                q_ref[:, h * HEAD_DIM:(h + 1) * HEAD_DIM] * (HEAD_DIM ** -0.5 * LOG2E)).astype(BF16)
            qa_ref[h, :, HEAD_DIM:] = _lane_row((t, HEAD_DIM), (1.0, 1.0, 1.0)).astype(BF16)
        m_ref[...] = jnp.full_like(m_ref, NEG)
        acc_ref[...] = jnp.zeros_like(acc_ref)

    def step(diagonal):
        logits = [lax.dot_general(ka_ref[h], qa_ref[h], _NT, preferred_element_type=F32) for h in heads]
        if diagonal:
            visible = (lax.broadcasted_iota(jnp.int32, (t, t), 0) <= lax.broadcasted_iota(jnp.int32, (t, t), 1))
            logits = [jnp.where(visible, s, NEG) for s in logits]
        probs = [_flash_probs(logits[h], m_ref.at[h]) for h in heads]
        for h in heads:
            _flash_accumulate(*probs[h], vt_ref[h], acc_ref.at[h])

    @pl.when(ki < qi)
    def _():
        step(False)

    @pl.when(ki == qi)
    def _():
        step(True)

    @pl.when(ki == pl.num_programs(2) - 1)
    def _():
        for h in heads:
            o_ref[:, h * HEAD_DIM:(h + 1) * HEAD_DIM] = _flash_finish(acc_ref[h])


def _fox_prompt(z, fka, fvt, t=512):
    s_len = z.shape[0]
    n = s_len // t
    hp = FOX_HEADS_PER_STEP
    w = hp * HEAD_DIM
    return pl.pallas_call(
        functools.partial(_fox_kernel, t=t),
        grid=(H_FOX // hp, n, n),
        in_specs=[pl.BlockSpec((t, w), lambda h, qi, ki: (qi, Z_FQ // w + h)),
                  pl.BlockSpec((hp, t, 2 * HEAD_DIM), lambda h, qi, ki: (h, jnp.minimum(ki, qi), 0)),
                  pl.BlockSpec((hp, VT_ROWS, t), lambda h, qi, ki: (h, 0, jnp.minimum(ki, qi)))],
        out_specs=pl.BlockSpec((t, w), lambda h, qi, ki: (qi, h)),
        out_shape=jax.ShapeDtypeStruct((s_len, FOX_W), F32),
        scratch_shapes=[pltpu.VMEM((hp, t, 2 * HEAD_DIM), BF16), pltpu.VMEM((hp, 1, t), F32),
                        pltpu.VMEM((hp, VT_ROWS, t), F32)],
        compiler_params=_cparams(("parallel", "parallel", "arbitrary")),
        name="fox_prompt",
    )(z, fka, fvt)


def _bisect_threshold(count_ge, n_valid, k_top):
    def cond(carry):
        it, _, cnt = carry
        unsettled = jnp.sum(jnp.where(cnt > k_top, 1, 0))
        return (it < 32) & (unsettled > 0)

    def body(carry):
        it, thr, cnt = carry
        cand = thr + lax.shift_left(jnp.int32(1), jnp.int32(31) - it)
        c = count_ge(cand)
        take = c >= k_top
        return it + 1, jnp.where(take, cand, thr), jnp.where(take, c, cnt)

    init = (jnp.int32(0), jnp.full(n_valid.shape, INT_MIN, jnp.int32), n_valid)
    return lax.while_loop(cond, body, init)[1]


def _idx_kernel(iqs_ref, sgn_ref, ik_ref, bias_ref, key_ref, *, tq, tkc, k_top):
    i = pl.program_id(0)
    s_len = key_ref.shape[0]
    n_chunks = ((i + 1) * tq + tkc - 1) // tkc
    qpos = i * tq + lax.broadcasted_iota(jnp.int32, (tkc, tq), 1)

    def score_chunk(c, carry):
        off = pl.multiple_of(c * tkc, tkc)
        kc = ik_ref[pl.ds(off, tkc), :]
        acc = jnp.zeros((tkc, tq), F32)
        for h in range(H_IDX):
            d = lax.dot_general(kc, iqs_ref[h], _NT, preferred_element_type=F32)
            acc = acc + sgn_ref[h:h + 1, :] * jnp.maximum(d, 0.0)
        kpos = off + lax.broadcasted_iota(jnp.int32, (tkc, tq), 0)
        key_ref[pl.ds(off, tkc), :] = jnp.where(kpos <= qpos, _sort_key(acc), INT_MIN)
        return carry

    lax.fori_loop(0, n_chunks, score_chunk, 0)

    def count_ge(cand):
        def body(c, cnt):
            off = pl.multiple_of(c * tq, tq)
            hit = jnp.where(key_ref[pl.ds(off, tq), :] >= cand, 1, 0)
            return cnt + jnp.sum(hit.reshape(tq // 8, 8, tq), axis=0)

        cnt = lax.fori_loop(0, i + 1, body, jnp.zeros((8, tq), jnp.int32))
        return jnp.sum(cnt, axis=0, keepdims=True)

    n_valid = i * tq + lax.broadcasted_iota(jnp.int32, (1, tq), 1) + 1
    thr = _bisect_threshold(count_ge, n_valid, k_top)
    thr = jnp.maximum(thr, INT_MIN + 1)

    def write_chunk(c, carry):
        off = pl.multiple_of(c * tkc, tkc)
        bias_ref[pl.ds(off, tkc), :] = jnp.where(key_ref[pl.ds(off, tkc), :] >= thr, 0.0, NEG).astype(BF16)
        return carry

    lax.fori_loop(0, n_chunks, write_chunk, 0)

    def fill_chunk(c, carry):
        off = pl.multiple_of(c * tkc, tkc)
        bias_ref[pl.ds(off, tkc), :] = jnp.full((tkc, tq), NEG, BF16)
        return carry

    lax.fori_loop(n_chunks, s_len // tkc, fill_chunk, 0)


def _idx_prompt(iqs, sgn_t, ikb, k_top, tq=256, tkc=256):
    s_len = ikb.shape[0]
    return pl.pallas_call(
        functools.partial(_idx_kernel, tq=tq, tkc=tkc, k_top=k_top),
        grid=(s_len // tq,),
        in_specs=[pl.BlockSpec((H_IDX, tq, IDX_DIM), lambda i: (0, i, 0)),
                  pl.BlockSpec((H_IDX, tq), lambda i: (0, i)),
                  pl.BlockSpec((s_len, IDX_DIM), lambda i: (0, 0))],
        out_specs=pl.BlockSpec((s_len, tq), lambda i: (0, i)),
        out_shape=jax.ShapeDtypeStruct((s_len, s_len), BF16),
        scratch_shapes=[pltpu.VMEM((s_len, tq), jnp.int32)],
        compiler_params=_cparams(("parallel",), 56),
        name="idx_prompt",
    )(iqs, sgn_t, ikb)


def _alibi_slope(h):
    return 2.0 ** (-8.0 * (h + 1) / H_DSA)


def _bf16_terms(x):
    out = []
    for _ in range(3):
        t = float(np.float32(x).astype(BF16))
        out.append(t)
        x = x - t
    return out


def _dsa_kernel(q_ref, ka_ref, vt_ref, bias_ref, o_ref, qa_ref, m_ref, acc_ref, *, tq, tk):
    qi = pl.program_id(0)
    ki = pl.program_id(1)
    ki_last = (qi * tq + tq - 1) // tk

    @pl.when(ki == 0)
    def _():
        for h in range(H_DSA):
            g, j = divmod(h, DSA_GROUP)
            rows = slice(j * tq, (j + 1) * tq)
            qa_ref[g, rows, :HEAD_DIM] = (
                q_ref[:, h * HEAD_DIM:(h + 1) * HEAD_DIM] * (HEAD_DIM ** -0.5 * LOG2E)).astype(BF16)
            sl = _bf16_terms(_alibi_slope(h) * LOG2E)
            qa_ref[g, rows, HEAD_DIM:] = _lane_row((tq, HEAD_DIM), [64.0 * t for t in sl] + sl).astype(BF16)
        m_ref[...] = jnp.full_like(m_ref, NEG)
        acc_ref[...] = jnp.zeros_like(acc_ref)

    @pl.when(ki <= ki_last)
    def _():
        mask = jnp.concatenate([bias_ref[...].astype(F32)] * DSA_GROUP, axis=1)
        groups = range(H_DSA_KV)
        logits = [lax.dot_general(ka_ref[g], qa_ref[g], _NT, preferred_element_type=F32) for g in groups]
        probs = [_flash_probs(logits[g] + mask, m_ref.at[g]) for g in groups]
        for g in groups:
            _flash_accumulate(*probs[g], vt_ref[g], acc_ref.at[g])

    @pl.when(ki == pl.num_programs(1) - 1)
    def _():
        for h in range(H_DSA):
            g, j = divmod(h, DSA_GROUP)
            o_ref[:, h * HEAD_DIM:(h + 1) * HEAD_DIM] = _flash_finish(acc_ref[g, :, j * tq:(j + 1) * tq])


def _dsa_prompt(z, dka, dvt, bias_t, tq=256, tk=512):
    s_len = z.shape[0]

    def kv_blk(qi, ki):
        return jnp.minimum(ki, (qi * tq + tq - 1) // tk)

    return pl.pallas_call(
        functools.partial(_dsa_kernel, tq=tq, tk=tk),
        grid=(s_len // tq, s_len // tk),
        in_specs=[pl.BlockSpec((tq, DSA_W), lambda qi, ki: (qi, Z_DQ // DSA_W)),
                  pl.BlockSpec((H_DSA_KV, tk, 2 * HEAD_DIM), lambda qi, ki: (0, kv_blk(qi, ki), 0)),
                  pl.BlockSpec((H_DSA_KV, VT_ROWS, tk), lambda qi, ki: (0, 0, kv_blk(qi, ki))),
                  pl.BlockSpec((tk, tq), lambda qi, ki: (kv_blk(qi, ki), qi))],
        out_specs=pl.BlockSpec((tq, DSA_W), lambda qi, ki: (qi, 0)),
        out_shape=jax.ShapeDtypeStruct((s_len, DSA_W), F32),
        scratch_shapes=[pltpu.VMEM((H_DSA_KV, DSA_GROUP * tq, 2 * HEAD_DIM), BF16),
                        pltpu.VMEM((H_DSA_KV, 1, DSA_GROUP * tq), F32),
                        pltpu.VMEM((H_DSA_KV, VT_ROWS, DSA_GROUP * tq), F32)],
        compiler_params=_cparams(("parallel", "arbitrary")),
        name="dsa_prompt",
    )(z, dka, dvt, bias_t)


def _dec_idx_kernel(pt_ref, *refs, n_pages, t_new):
    del pt_ref
    ik_refs = refs[:n_pages]
    lf_refs = refs[n_pages:2 * n_pages]
    iq_ref, w_ref, ikn_ref, lfn_ref, key_ref, fneg_ref = refs[2 * n_pages:]

    row = lax.broadcasted_iota(jnp.int32, (PAGE_SIZE, PAGE_SIZE), 0)
    col = lax.broadcasted_iota(jnp.int32, (PAGE_SIZE, PAGE_SIZE), 1)
    tri = (row <= col).astype(F32)
    carry = jnp.zeros((H_FOX, 1), F32)
    for p in range(n_pages + 1):
        lf = lf_refs[p][...] if p < n_pages else lfn_ref[...]
        cum = jnp.dot(lf, tri, preferred_element_type=F32, precision=lax.Precision.HIGHEST) + carry
        fneg_ref[:, p * PAGE_SIZE:(p + 1) * PAGE_SIZE] = -cum
        carry = cum[:, PAGE_SIZE - 1:PAGE_SIZE]

    w = w_ref[...] * (H_IDX ** -0.5)
    iqs = (iq_ref[...] * (jnp.abs(w) * (IDX_DIM ** -0.5))).astype(BF16)
    sgn = jnp.where(w > 0, 1.0, -1.0)
    qrow = lax.broadcasted_iota(jnp.int32, (16, PAGE_SIZE), 0) % t_new
    lane = lax.broadcasted_iota(jnp.int32, (16, PAGE_SIZE), 1)
    for p in range(n_pages + 1):
        kp_t = (ik_refs[p] if p < n_pages else ikn_ref)[...].astype(BF16)
        d = jnp.dot(iqs, kp_t, preferred_element_type=F32)
        sc = jnp.sum((sgn * jnp.maximum(d, 0.0)).reshape(H_IDX, 16, PAGE_SIZE), axis=0)
        key = _sort_key(sc)
        if p == n_pages:
            key = jnp.where((lane <= qrow) & (lane < t_new), key, INT_MIN)
        key_ref[:, p * PAGE_SIZE:(p + 1) * PAGE_SIZE] = key


def _dec_idx(page_table, idx_pages, logf_pages_t, iq16, w16, ik_new_t, logf_new_t, t_new):
    b, n_pages = page_table.shape
    width = (n_pages + 1) * PAGE_SIZE
    page = lambda p: (lambda bi, pt: (pt[bi, p], 0, 0))
    per_b = lambda bi, pt: (bi, 0, 0)
    in_specs = ([pl.BlockSpec((None, IDX_DIM, PAGE_SIZE), page(p)) for p in range(n_pages)]
                + [pl.BlockSpec((None, H_FOX, PAGE_SIZE), page(p)) for p in range(n_pages)]
                + [pl.BlockSpec((None, H_IDX * 16, IDX_DIM), per_b), pl.BlockSpec((None, H_IDX * 16, 1), per_b),
                   pl.BlockSpec((None, IDX_DIM, PAGE_SIZE), per_b), pl.BlockSpec((None, H_FOX, PAGE_SIZE), per_b)])
    return pl.pallas_call(
        functools.partial(_dec_idx_kernel, n_pages=n_pages, t_new=t_new),
        grid_spec=pltpu.PrefetchScalarGridSpec(
            num_scalar_prefetch=1,
            grid=(b,),
            in_specs=in_specs,
            out_specs=[pl.BlockSpec((None, 16, width), per_b), pl.BlockSpec((None, H_FOX, width), per_b)]),
        out_shape=[jax.ShapeDtypeStruct((b, 16, width), jnp.int32), jax.ShapeDtypeStruct((b, H_FOX, width), F32)],
        compiler_params=_cparams(("parallel",)),
        name="dec_idx",
    )(page_table, *([idx_pages] * n_pages), *([logf_pages_t] * n_pages), iq16, w16, ik_new_t, logf_new_t)


def _dec_select_kernel(key_ref, nv_ref, sel_ref, *, k_top):
    def count_ge(cand):
        return jnp.sum(jnp.where(key_ref[...] >= cand, 1, 0), axis=-1, keepdims=True)

    thr = jnp.maximum(_bisect_threshold(count_ge, nv_ref[...], k_top), INT_MIN + 1)
    sel_ref[...] = jnp.where(key_ref[...] >= thr, 0.0, NEG)


def _dec_select(keys, n_valid, k_top, tm=256):
    rows, width = keys.shape
    tm = min(tm, rows)
    return pl.pallas_call(
        functools.partial(_dec_select_kernel, k_top=k_top),
        grid=(rows // tm,),
        in_specs=[pl.BlockSpec((tm, width), lambda i: (i, 0)), pl.BlockSpec((tm, 1), lambda i: (i, 0))],
        out_specs=pl.BlockSpec((tm, width), lambda i: (i, 0)),
        out_shape=jax.ShapeDtypeStruct((rows, width), F32),
        compiler_params=_cparams(("parallel",)),
        name="dec_select",
    )(keys, n_valid)


def _dec_attn_kernel(pt_ref, *refs, n_pages, kp, t_new):
    del pt_ref
    fk_refs, fv_refs, dk_refs, dv_refs = (refs[i * kp:(i + 1) * kp] for i in range(4))
    (fkn_ref, fvn_ref, dkn_ref, dvn_ref, qf_ref, qd_ref, fneg_ref, dbias_ref, fneg_new_ref, dbias_new_ref, slope_ref,
     of_ref, od_ref, mf_ref, lf_ref, accf_ref, md_ref, ld_ref, accd_ref, pairf_ref, paird_ref) = refs[4 * kp:]
    p_id = pl.program_id(1)
    n_steps = n_pages // kp
    rows_q = t_new * H_FOX
    wf = PAGE_SIZE * H_FOX
    wd = PAGE_SIZE * H_DSA_KV

    def pair_mask(n_keys, heads_per_key_head, n_kv):
        qh = lax.broadcasted_iota(jnp.int32, (rows_q, n_keys), 0) % H_FOX
        kh = lax.broadcasted_iota(jnp.int32, (rows_q, n_keys), 1) % n_kv
        return jnp.where(qh // heads_per_key_head == kh, 0.0, NEG)

    @pl.when(p_id == 0)
    def _():
        mf_ref[...] = jnp.full_like(mf_ref, NEG)
        lf_ref[...] = jnp.zeros_like(lf_ref)
        accf_ref[...] = jnp.zeros_like(accf_ref)
        md_ref[...] = jnp.full_like(md_ref, NEG)
        ld_ref[...] = jnp.zeros_like(ld_ref)
        accd_ref[...] = jnp.zeros_like(accd_ref)
        pairf_ref[...] = pair_mask(wf, 1, H_FOX)
        paird_ref[...] = pair_mask(wd, DSA_GROUP, H_DSA_KV)

    def attend(q_ref, k, v, bias, m_ref, l_ref, acc_ref):
        q = (q_ref[...] * (HEAD_DIM ** -0.5)).astype(BF16)
        s = lax.dot_general(q, k, _NT, preferred_element_type=F32) + bias
        m_prev = m_ref[...]
        m_new = jnp.maximum(m_prev, jnp.max(s, axis=-1, keepdims=True))
        alpha = jnp.exp(m_prev - m_new)
        p = jnp.exp(s - m_new)
        l_ref[...] = alpha * l_ref[...] + jnp.sum(p, axis=-1, keepdims=True)
        acc_ref[...] = alpha * acc_ref[...] + jnp.dot(p.astype(BF16), v, preferred_element_type=F32)
        m_ref[...] = m_new

    def sel_rows(db):
        return jnp.concatenate([jnp.broadcast_to(db[q:q + 1], (H_DSA, db.shape[1])) for q in range(t_new)], axis=0)

    def alibi(n_keys, tok0):
        tok = tok0 + lax.broadcasted_iota(jnp.int32, (1, n_keys), 1) // H_DSA_KV
        return slope_ref[...] * tok.astype(F32)

    @pl.when(p_id < n_steps)
    def _():
        cat = lambda page_refs: jnp.concatenate([r[...].astype(BF16) for r in page_refs], axis=0)
        bias = jnp.concatenate([pairf_ref[...]] * kp, axis=1) + fneg_ref[...]
        attend(qf_ref, cat(fk_refs), cat(fv_refs), bias, mf_ref, lf_ref, accf_ref)
        bias = (jnp.concatenate([paird_ref[...]] * kp, axis=1) + sel_rows(dbias_ref[...])
                + alibi(kp * wd, (p_id * kp - n_pages) * PAGE_SIZE))
        attend(qd_ref, cat(dk_refs), cat(dv_refs), bias, md_ref, ld_ref, accd_ref)

    @pl.when(p_id == n_steps)
    def _():
        def causal(n_keys, n_kv):
            q = lax.broadcasted_iota(jnp.int32, (rows_q, n_keys), 0) // H_FOX
            tok = lax.broadcasted_iota(jnp.int32, (rows_q, n_keys), 1) // n_kv
            return jnp.where(tok <= q, 0.0, NEG)

        nf = t_new * H_FOX
        bias = pairf_ref[:, :nf] + fneg_new_ref[:, :nf] + causal(nf, H_FOX)
        attend(qf_ref, fkn_ref[...].astype(BF16), fvn_ref[...].astype(BF16), bias, mf_ref, lf_ref, accf_ref)
        nd = t_new * H_DSA_KV
        bias = paird_ref[:, :nd] + sel_rows(dbias_new_ref[:, :nd]) + alibi(nd, 0)
        attend(qd_ref, dkn_ref[...].astype(BF16), dvn_ref[...].astype(BF16), bias, md_ref, ld_ref, accd_ref)
        of_ref[...] = accf_ref[...] / lf_ref[...]
        od_ref[...] = accd_ref[...] / ld_ref[...]


def _dec_attn(page_table, fox_k, fox_v, dsa_k, dsa_v, fk_new, fv_new, dk_new, dv_new, qf, qd, fneg_rows, dbias_rows,
              slopes, t_new):
    b, n_pages = page_table.shape
    kp = 4 if n_pages % 4 == 0 else 1
    n_steps = n_pages // kp
    wf = PAGE_SIZE * H_FOX
    wd = PAGE_SIZE * H_DSA_KV
    rows_q = t_new * H_FOX

    def page(j):
        return lambda bi, p, pt: (pt[bi, jnp.minimum(p, n_steps - 1) * kp + j], 0, 0)

    per_b = lambda bi, p, pt: (bi, 0, 0)
    past = lambda bi, p, pt: (bi, 0, jnp.minimum(p, n_steps - 1))
    new = lambda bi, p, pt: (bi, 0, n_pages)
    in_specs = ([pl.BlockSpec((None, wf, HEAD_DIM), page(j)) for j in range(kp)] * 2
                + [pl.BlockSpec((None, wd, HEAD_DIM), page(j)) for j in range(kp)] * 2
                + [pl.BlockSpec((None, t_new * H_FOX, HEAD_DIM), per_b)] * 2
                + [pl.BlockSpec((None, t_new * H_DSA_KV, HEAD_DIM), per_b)] * 2
                + [pl.BlockSpec((None, rows_q, HEAD_DIM), per_b)] * 2
                + [pl.BlockSpec((None, 1, kp * wf), past), pl.BlockSpec((None, 8, kp * wd), past),
                   pl.BlockSpec((None, 1, wf), new), pl.BlockSpec((None, 8, wd), new),
                   pl.BlockSpec((rows_q, 1), lambda bi, p, pt: (0, 0))])
    out = jax.ShapeDtypeStruct((b, rows_q, HEAD_DIM), F32)
    col = pltpu.VMEM((rows_q, 1), F32)
    acc = pltpu.VMEM((rows_q, HEAD_DIM), F32)
    return pl.pallas_call(
        functools.partial(_dec_attn_kernel, n_pages=n_pages, kp=kp, t_new=t_new),
        grid_spec=pltpu.PrefetchScalarGridSpec(
            num_scalar_prefetch=1,
            grid=(b, n_steps + 1),
            in_specs=in_specs,
            out_specs=[pl.BlockSpec((None, rows_q, HEAD_DIM), per_b)] * 2,
            scratch_shapes=[col, col, acc, col, col, acc, pltpu.VMEM((rows_q, wf), F32),
                            pltpu.VMEM((rows_q, wd), F32)]),
        out_shape=[out, out],
        compiler_params=_cparams(("parallel", "arbitrary")),
        name="dec_attn",
    )(page_table, *([fox_k] * kp), *([fox_v] * kp), *([dsa_k] * kp), *([dsa_v] * kp),
      fk_new, fv_new, dk_new, dv_new, qf, qd, fneg_rows, dbias_rows, fneg_rows, dbias_rows, slopes)


def _ffn_up_kernel(h_ref, hprev_ref, wg_ref, wu_ref, wc_ref, bc_ref, a_ref, tail_ref, *, tm):
    i = pl.program_id(0)
    h = h_ref[...]
    wg = wg_ref[...].astype(BF16)
    g = jnp.dot(h, wg, preferred_element_type=F32)
    u = jnp.dot(h, wu_ref[...].astype(BF16), preferred_element_type=F32)
    gh = jnp.dot(hprev_ref[...], wg, preferred_element_type=F32)
    gh = jnp.where(i > 0, gh, 0.0)
    ext = jnp.concatenate([gh, g], axis=0)
    g1 = pltpu.roll(ext, 1, 0)[16:]
    g2 = pltpu.roll(ext, 2, 0)[16:]
    wc = wc_ref[...]
    conv = bc_ref[...] + wc[0:1] * g2 + wc[1:2] * g1 + wc[2:3] * g
    a_ref[...] = (conv * _sigmoid(conv) * u).astype(a_ref.dtype)
    tail_ref[...] = g[tm - 8:]


def _ffn_up_prompt(h2, w_gate, w_up, w_conv, b_conv, tm=1024, tn=256):
    m, d = h2.shape
    d_ff = w_gate.shape[1]
    tm = min(tm, m)
    return pl.pallas_call(
        functools.partial(_ffn_up_kernel, tm=tm),
        grid=(m // tm, d_ff // tn),
        in_specs=[pl.BlockSpec((tm, d), lambda i, j: (i, 0)),
                  pl.BlockSpec((16, d), lambda i, j: (jnp.maximum(i * (tm // 16) - 1, 0), 0)),
                  pl.BlockSpec((d, tn), lambda i, j: (0, j)), pl.BlockSpec((d, tn), lambda i, j: (0, j)),
                  pl.BlockSpec((CONV_W, tn), lambda i, j: (0, j)), pl.BlockSpec((1, tn), lambda i, j: (0, j))],
        out_specs=[pl.BlockSpec((tm, tn), lambda i, j: (i, j)), pl.BlockSpec((8, tn), lambda i, j: (i, j))],
        out_shape=[jax.ShapeDtypeStruct((m, d_ff), BF16), jax.ShapeDtypeStruct((m // tm * 8, d_ff), F32)],
        compiler_params=_cparams(("parallel", "parallel")),
        name="ffn_up_prompt",
    )(h2, h2, w_gate, w_up, w_conv, b_conv.reshape(1, d_ff))


def _ffn_up_dec_kernel(h_ref, wg_ref, wu_ref, wc_ref, bc_ref, s0_ref, s1_ref, a_ref, g_ref, *, t_new):
    h = h_ref[...]
    g = jnp.dot(h, wg_ref[...].astype(BF16), preferred_element_type=F32)
    u = jnp.dot(h, wu_ref[...].astype(BF16), preferred_element_type=F32)
    g_ref[...] = g
    t = lax.broadcasted_iota(jnp.int32, g.shape, 0) % t_new
    g1 = jnp.where(t >= 1, pltpu.roll(g, 1, 0), 0.0) + s1_ref[...]
    g2 = jnp.where(t >= 2, pltpu.roll(g, 2, 0), 0.0) + s0_ref[...]
    wc = wc_ref[...]
    conv = bc_ref[...] + wc[0:1] * g2 + wc[1:2] * g1 + wc[2:3] * g
    a_ref[...] = (conv * _sigmoid(conv) * u).astype(a_ref.dtype)


def _ffn_up_dec(h2, w_gate, w_up, w_conv, b_conv, tap0, tap1, t_new, tn=256):
    m, d = h2.shape
    d_ff = w_gate.shape[1]
    col = lambda j: (0, j)
    return pl.pallas_call(
        functools.partial(_ffn_up_dec_kernel, t_new=t_new),
        grid=(d_ff // tn,),
        in_specs=[pl.BlockSpec((m, d), lambda j: (0, 0)), pl.BlockSpec((d, tn), col), pl.BlockSpec((d, tn), col),
                  pl.BlockSpec((CONV_W, tn), col), pl.BlockSpec((1, tn), col),
                  pl.BlockSpec((m, tn), col), pl.BlockSpec((m, tn), col)],
        out_specs=[pl.BlockSpec((m, tn), col), pl.BlockSpec((m, tn), col)],
        out_shape=[jax.ShapeDtypeStruct((m, d_ff), BF16), jax.ShapeDtypeStruct((m, d_ff), F32)],
        compiler_params=_cparams(("parallel",)),
        name="ffn_up_dec",
    )(h2, w_gate, w_up, w_conv, b_conv.reshape(1, d_ff), tap0, tap1)


def _split_mod(mod):
    return jnp.split(mod, 6, axis=-1)


def _attn_out_and_ffn_in(x, o_f, o_d, mods, p):
    _, _, gt1, sh2, sc2, _ = mods
    a = _attn_norm(o_f, o_d, p["g_fox_out"], p["g_dsa_out"])
    o = _matmul(a, p["w_out"], 1024, 512, "out_proj")
    return _res_pre(o, x, gt1, p["g_post_attn"], p["g_pre_ffn"], sc2, sh2)


def _ffn_down_and_res(a, x1, mods, p):
    d_ff = a.shape[1]
    f = _matmul_ksplit(a, p["w_down"], 1024, 512, d_ff // 2, "ffn_down")
    return _res(f, x1, mods[5], p["g_post_ffn"])


def _project(x, mods, p):
    sh1, sc1 = mods[0], mods[1]
    h = _prenorm(x, p["g_pre_attn"], sc1, sh1)
    return (_matmul_nt(h, p["w_in_t"], 1024, ZF_TN, "in_proj_fox", n=ZF_W),
            _matmul_nt(h, p["w_in_tail_t"], 1024, ZD_TN, "in_proj_rest"))


def _kv_states(zf, zd, logf, conv):
    return (zf[:, Z_FK:Z_FK + FOX_W], zf[:, Z_FV:Z_FV + FOX_W], logf, zd[:, Z_DK:Z_DK + DSA_KV_W],
            zd[:, Z_DV:Z_DV + DSA_KV_W], zd[:, Z_SM + SM_IK:Z_SM + SM_IK + IDX_DIM], conv)


def _prompt_layer(x, mods, p):
    s_len = x.shape[0]
    zf, zd = _project(x, mods, p)
    logf, cum = _logf_cumsum(zd, p["b_f"])
    fka, fvt, dka, dvt, iqs, ikb, sgn_t = _attn_prep(zf, zd, cum)
    o_f = _fox_prompt(zf, fka, fvt)
    bias_t = _idx_prompt(iqs, sgn_t, ikb, min(TOPK_MAX, s_len // 4))
    o_d = _dsa_prompt(zd, dka, dvt, bias_t)
    x1, h2 = _attn_out_and_ffn_in(x, o_f, o_d, mods, p)
    a, tails = _ffn_up_prompt(h2, p["w_gate"], p["w_up"], p["w_conv"], p["b_conv"])
    y = _ffn_down_and_res(a, x1, mods, p)
    return y, _kv_states(zf, zd, logf, tails[-(CONV_W - 1):])


def _pad_rows(x, rows):
    return jnp.pad(x, ((0, 0), (0, rows - x.shape[1]), (0, 0)))


def _decode_layer(x, mods, p, page_table, caches, state_conv, t_new):
    m = x.shape[0]
    b = m // t_new
    n_pages = page_table.shape[1]
    cache_fox_k, cache_fox_v, cache_fox_logf_t, cache_dsa_k, cache_dsa_v, cache_idx_k = caches
    zf, zd = _project(x, mods, p)
    logf, _ = _logf_cumsum(zd, p["b_f"])
    zf3 = zf.reshape(b, t_new, ZF_W)
    zd3 = zd.reshape(b, t_new, ZD_W)

    rep = 16 // t_new
    iq16 = jnp.tile(zd3[:, :, Z_IQ:Z_IQ + IQ_W].reshape(b, t_new, H_IDX, IDX_DIM).transpose(0, 2, 1, 3),
                    (1, 1, rep, 1)).reshape(b, H_IDX * 16, IDX_DIM)
    w16 = jnp.tile(zd3[:, :, Z_SM + SM_IW:Z_SM + SM_IW + H_IDX].transpose(0, 2, 1), (1, 1, rep)).reshape(
        b, H_IDX * 16, 1)
    ik_new_t = jnp.pad(zd3[:, :, Z_SM + SM_IK:Z_SM + SM_IK + IDX_DIM].transpose(0, 2, 1),
                       ((0, 0), (0, 0), (0, PAGE_SIZE - t_new)))
    logf_new_t = jnp.pad(logf.reshape(b, t_new, H_FOX).transpose(0, 2, 1), ((0, 0), (0, 0), (0, PAGE_SIZE - t_new)))
    k_top = min(TOPK_MAX, (n_pages * PAGE_SIZE + t_new) // 4)
    keys, fneg = _dec_idx(page_table, cache_idx_k, cache_fox_logf_t, iq16, w16, ik_new_t, logf_new_t, t_new)
    n_valid = jnp.tile(n_pages * PAGE_SIZE + 1 + jnp.arange(8, dtype=jnp.int32) % t_new, b).reshape(b * 8, 1)
    dbias = _dec_select(keys[:, :8].reshape(b * 8, -1), n_valid, k_top).reshape(b, 8, -1)

    heads = lambda z3, off, w: z3[:, :, off:off + w].reshape(b, t_new * (w // HEAD_DIM), HEAD_DIM)
    fneg_rows = fneg.transpose(0, 2, 1).reshape(b, 1, -1)
    dbias_rows = jnp.repeat(dbias, H_DSA_KV, axis=-1)
    slopes = jnp.tile(2.0 ** (-8.0 * jnp.arange(1, H_DSA + 1, dtype=F32) / H_DSA), t_new).reshape(H_DSA * t_new, 1)
    o_f, o_d = _dec_attn(
        page_table, cache_fox_k, cache_fox_v, cache_dsa_k, cache_dsa_v,
        heads(zf3, Z_FK, FOX_W), heads(zf3, Z_FV, FOX_W), heads(zd3, Z_DK, DSA_KV_W), heads(zd3, Z_DV, DSA_KV_W),
        heads(zf3, Z_FQ, FOX_W), heads(zd3, Z_DQ, DSA_W), fneg_rows, dbias_rows, slopes, t_new)

    x1, h2 = _attn_out_and_ffn_in(x, o_f.reshape(m, FOX_W), o_d.reshape(m, DSA_W), mods, p)
    d_ff = state_conv.shape[-1]
    zero = jnp.zeros((b, 1, d_ff), F32)
    s0, s1 = state_conv[:, 0:1], state_conv[:, 1:2]
    tap0 = jnp.concatenate([s0, s1] + [zero] * (t_new - 2), axis=1).reshape(m, d_ff)
    tap1 = jnp.concatenate([s1] + [zero] * (t_new - 1), axis=1).reshape(m, d_ff)
    a, g = _ffn_up_dec(h2, p["w_gate"], p["w_up"], p["w_conv"], p["b_conv"], tap0, tap1, t_new)
    y = _ffn_down_and_res(a, x1, mods, p)
    conv_new = g.reshape(b, t_new, d_ff)[:, t_new - (CONV_W - 1):]
    return y, _kv_states(zf, zd, logf, conv_new)


def _regroup_w_in_tail(w_in_t):
    n, d = w_in_t.shape
    rest = w_in_t[ZF_W + H_FOX:].astype(BF16)
    fg = w_in_t[ZF_W:ZF_W + H_FOX].astype(BF16)
    assert n - ZF_W - H_FOX == Z_SM + SM_IW + H_IDX and SM_FG == SM_IW + H_IDX
    return jnp.concatenate([rest, fg, jnp.zeros((ZD_W - (n - ZF_W), d), BF16)], axis=0)


def kernel(x_prompt, x_sample, c_prompt, c_sample, page_table, cache_fox_k, cache_fox_v, cache_fox_logf, cache_dsa_k, cache_dsa_v, cache_idx_k, state_conv, w_in, b_f, w_out, g_fox_out, g_dsa_out, w_mod, b_mod, g_pre_attn, g_post_attn, g_pre_ffn, g_post_ffn, w_gate, w_up, w_conv, b_conv, w_down):
    depth = w_in.shape[0]
    bp, s_len, d = x_prompt.shape
    bs, t_new, _ = x_sample.shape
    assert bp == 1 and t_new >= CONV_W - 1 and 16 % t_new == 0

    xp = x_prompt.reshape(s_len, d)
    xs = x_sample.reshape(bs * t_new, d)
    n_c = bp + bs
    c_all = jnp.pad(jnp.concatenate([c_prompt, c_sample], axis=0), ((0, -n_c % 8), (0, 0)))
    n_phys = cache_fox_k.shape[1]
    fold = lambda c: c.reshape((depth * n_phys,) + c.shape[2:])
    rows = lambda c: c.reshape(depth * n_phys, PAGE_SIZE * c.shape[3], HEAD_DIM)
    caches = (rows(cache_fox_k), rows(cache_fox_v), fold(cache_fox_logf).transpose(0, 2, 1), rows(cache_dsa_k),
              rows(cache_dsa_v), fold(cache_idx_k).transpose(0, 2, 1))
    p_states, s_states = [], []
    for l in range(depth):
        w_in_t = w_in[l].T
        p = dict(w_in_t=w_in_t, w_in_tail_t=_regroup_w_in_tail(w_in_t), b_f=b_f[l], w_out=w_out[l].astype(BF16),
                 g_fox_out=g_fox_out[l][None], g_dsa_out=g_dsa_out[l][None],
                 g_pre_attn=g_pre_attn[l][None], g_post_attn=g_post_attn[l][None],
                 g_pre_ffn=g_pre_ffn[l][None], g_post_ffn=g_post_ffn[l][None],
                 w_gate=w_gate[l], w_up=w_up[l], w_conv=w_conv[l], b_conv=b_conv[l],
                 w_down=w_down[l].astype(BF16))
        mod = _modulation(c_all, w_mod[l], b_mod[l])
        mods_p = _split_mod(mod[:bp])
        mods_s = [jnp.repeat(v, t_new, axis=0) for v in _split_mod(mod[bp:n_c])]
        xp, st_p = _prompt_layer(xp, mods_p, p)
        xs, st_s = _decode_layer(xs, mods_s, p, page_table + l * n_phys, caches, state_conv[l], t_new)
        p_states.append(st_p)
        s_states.append(st_s)

    def stack(states, i, shape):
        return jnp.stack([st[i].reshape(shape) for st in states])

    d_ff = state_conv.shape[-1]
    outs = [xp.reshape(bp, s_len, d), xs.reshape(bs, t_new, d)]
    for states, (bb, tt) in ((p_states, (bp, s_len)), (s_states, (bs, t_new))):
        outs += [stack(states, 0, (bb, tt, H_FOX, HEAD_DIM)), stack(states, 1, (bb, tt, H_FOX, HEAD_DIM)),
                 stack(states, 2, (bb, tt, H_FOX)), stack(states, 3, (bb, tt, H_DSA_KV, HEAD_DIM)),
                 stack(states, 4, (bb, tt, H_DSA_KV, HEAD_DIM)), stack(states, 5, (bb, tt, IDX_DIM)),
                 stack(states, 6, (bb, CONV_W - 1, d_ff))]
    return tuple(outs)
```

```python
import functools

import jax
import jax.numpy as jnp
import numpy as np
from jax import lax
from jax.experimental import pallas as pl
from jax.experimental.pallas import tpu as pltpu

HEAD_DIM = 128
H_FOX = 16
H_DSA = 16
H_DSA_KV = 4
DSA_GROUP = H_DSA // H_DSA_KV
H_IDX = 16
IDX_DIM = 64
TOPK_MAX = 256
CONV_W = 3
PAGE_SIZE = 128
RMS_EPS = 1e-6
FOX_W = H_FOX * HEAD_DIM
DSA_W = H_DSA * HEAD_DIM
DSA_KV_W = H_DSA_KV * HEAD_DIM
IQ_W = H_IDX * IDX_DIM

Z_FQ = 0
Z_FK = Z_FQ + FOX_W
Z_FV = Z_FK + FOX_W
ZF_W = Z_FV + FOX_W
ZF_TN = 512
Z_DQ = 0
Z_DK = Z_DQ + DSA_W
Z_DV = Z_DK + DSA_KV_W
Z_IQ = Z_DV + DSA_KV_W
Z_SM = Z_IQ + IQ_W
SM_IK = 0
SM_IW = SM_IK + IDX_DIM
SM_FG = SM_IW + H_IDX
ZD_TN = 768
ZD_W = -(-(Z_SM + 128) // ZD_TN) * ZD_TN

NEG = -1e30
LOG2E = 1.4426950408889634
VT_ROWS = HEAD_DIM + 16
INT_MIN = -2 ** 31
MIB = 1024 * 1024
BF16 = jnp.bfloat16
F32 = jnp.float32

_NT = (((1,), (1,)), ((), ()))


def _cparams(sem, vmem_mib=48):
    return pltpu.CompilerParams(dimension_semantics=sem, vmem_limit_bytes=vmem_mib * MIB)


def _rms(x, g):
    return x * lax.rsqrt(jnp.mean(x * x, axis=-1, keepdims=True) + RMS_EPS) * g


def _sigmoid(x):
    return 1.0 / (1.0 + jnp.exp(-x))


def _sort_key(x):
    b = pltpu.bitcast(x, jnp.int32)
    return b ^ ((b >> 31) & jnp.int32(0x7FFFFFFF))


def _row_spec(arr, tm):
    d = arr.shape[1]
    if arr.shape[0] == 1:
        return pl.BlockSpec((1, d), lambda i: (0, 0))
    return pl.BlockSpec((tm, d), lambda i: (i, 0))


def _mod_kernel(c_ref, w_ref, b_ref, o_ref):
    c = c_ref[...]
    a = (c * _sigmoid(c)).astype(BF16)
    o_ref[...] = jnp.dot(a, w_ref[...].astype(BF16), preferred_element_type=F32) + b_ref[...]


def _modulation(c, w_mod, b_mod):
    r, d = c.shape
    n = w_mod.shape[1]
    tn = 512
    return pl.pallas_call(
        _mod_kernel,
        grid=(n // tn,),
        in_specs=[pl.BlockSpec((r, d), lambda j: (0, 0)),
                  pl.BlockSpec((d, tn), lambda j: (0, j)),
                  pl.BlockSpec((1, tn), lambda j: (0, j))],
        out_specs=pl.BlockSpec((r, tn), lambda j: (0, j)),
        out_shape=jax.ShapeDtypeStruct((r, n), F32),
        compiler_params=_cparams(("parallel",)),
        name="modulation",
    )(c, w_mod, b_mod.reshape(1, n))


def _prenorm_kernel(x_ref, g_ref, sc_ref, sh_ref, o_ref):
    y = _rms(x_ref[...], g_ref[...])
    o_ref[...] = (y * (1.0 + sc_ref[...]) + sh_ref[...]).astype(o_ref.dtype)


def _prenorm(x, g, sc, sh, tm=256):
    m, d = x.shape
    tm = min(tm, m)
    return pl.pallas_call(
        _prenorm_kernel,
        grid=(m // tm,),
        in_specs=[pl.BlockSpec((tm, d), lambda i: (i, 0)), _row_spec(g, tm), _row_spec(sc, tm), _row_spec(sh, tm)],
        out_specs=pl.BlockSpec((tm, d), lambda i: (i, 0)),
        out_shape=jax.ShapeDtypeStruct((m, d), BF16),
        compiler_params=_cparams(("parallel",)),
        name="prenorm",
    )(x, g, sc, sh)


def _attn_norm_kernel(of_ref, od_ref, gf_ref, gd_ref, o_ref):
    o_ref[:, :FOX_W] = _rms(of_ref[...], gf_ref[...]).astype(o_ref.dtype)
    o_ref[:, FOX_W:] = _rms(od_ref[...], gd_ref[...]).astype(o_ref.dtype)


def _attn_norm(o_f, o_d, g_f, g_d, tm=256):
    m = o_f.shape[0]
    tm = min(tm, m)
    return pl.pallas_call(
        _attn_norm_kernel,
        grid=(m // tm,),
        in_specs=[pl.BlockSpec((tm, FOX_W), lambda i: (i, 0)), pl.BlockSpec((tm, DSA_W), lambda i: (i, 0)),
                  _row_spec(g_f, tm), _row_spec(g_d, tm)],
        out_specs=pl.BlockSpec((tm, FOX_W + DSA_W), lambda i: (i, 0)),
        out_shape=jax.ShapeDtypeStruct((m, FOX_W + DSA_W), BF16),
        compiler_params=_cparams(("parallel",)),
        name="attn_norm",
    )(o_f, o_d, g_f, g_d)


def _res_pre_kernel(o_ref, x_ref, gt_ref, gpost_ref, gpre_ref, sc_ref, sh_ref, x1_ref, h2_ref):
    x1 = x_ref[...] + gt_ref[...] * _rms(o_ref[...], gpost_ref[...])
    x1_ref[...] = x1
    h2_ref[...] = (_rms(x1, gpre_ref[...]) * (1.0 + sc_ref[...]) + sh_ref[...]).astype(h2_ref.dtype)


def _res_pre(o, x, gt, g_post, g_pre, sc, sh, tm=128):
    m, d = x.shape
    tm = min(tm, m)
    blk =pl.BlockSpec((tm, d), lambda i: (i, 0))
    return pl.pallas_call(
        _res_pre_kernel,
        grid=(m // tm,),
        in_specs=[blk, blk, _row_spec(gt, tm), _row_spec(g_post, tm), _row_spec(g_pre, tm),
                  _row_spec(sc, tm), _row_spec(sh, tm)],
        out_specs=[blk, blk],
        out_shape=[jax.ShapeDtypeStruct((m, d), F32), jax.ShapeDtypeStruct((m, d), BF16)],
        compiler_params=_cparams(("parallel",)),
        name="res_pre",
    )(o, x, gt, g_post, g_pre, sc, sh)


def _res_kernel(f_ref, x_ref, gt_ref, gpost_ref, y_ref):
    y_ref[...] = x_ref[...] + gt_ref[...] * _rms(f_ref[...], gpost_ref[...])


def _res(f, x, gt, g_post, tm=256):
    m, d = x.shape
    tm = min(tm, m)
    blk =pl.BlockSpec((tm, d), lambda i: (i, 0))
    return pl.pallas_call(
        _res_kernel,
        grid=(m // tm,),
        in_specs=[blk, blk, _row_spec(gt, tm), _row_spec(g_post, tm)],
        out_specs=blk,
        out_shape=jax.ShapeDtypeStruct((m, d), F32),
        compiler_params=_cparams(("parallel",)),
        name="res",
    )(f, x, gt, g_post)


def _mm_kernel(a_ref, w_ref, o_ref):
    o_ref[...] = jnp.dot(a_ref[...], w_ref[...], preferred_element_type=F32)


def _matmul(a, w, tm, tn, name):
    m, k = a.shape
    n = w.shape[1]
    tm = min(tm, m)
    return pl.pallas_call(
        _mm_kernel,
        grid=(m // tm, n // tn),
        in_specs=[pl.BlockSpec((tm, k), lambda i, j: (i, 0)), pl.BlockSpec((k, tn), lambda i, j: (0, j))],
        out_specs=pl.BlockSpec((tm, tn), lambda i, j: (i, j)),
        out_shape=jax.ShapeDtypeStruct((m, n), F32),
        compiler_params=_cparams(("parallel", "parallel")),
        name=name,
    )(a, w)


def _mm_nt_kernel(a_ref, wt_ref, o_ref):
    o_ref[...] = lax.dot_general(a_ref[...], wt_ref[...].astype(BF16), _NT, preferred_element_type=F32)


def _matmul_nt(a, w_t, tm, tn, name, n=None):
    m, k = a.shape
    n = w_t.shape[0] if n is None else n
    tm = min(tm, m)
    return pl.pallas_call(
        _mm_nt_kernel,
        grid=(m // tm, n // tn),
        in_specs=[pl.BlockSpec((tm, k), lambda i, j: (i, 0)), pl.BlockSpec((tn, k), lambda i, j: (j, 0))],
        out_specs=pl.BlockSpec((tm, tn), lambda i, j: (i, j)),
        out_shape=jax.ShapeDtypeStruct((m, n), F32),
        compiler_params=_cparams(("parallel", "parallel")),
        name=name,
    )(a, w_t)


def _mmk_kernel(a_ref, w_ref, o_ref):
    @pl.when(pl.program_id(2) == 0)
    def _():
        o_ref[...] = jnp.zeros_like(o_ref)

    o_ref[...] += jnp.dot(a_ref[...], w_ref[...], preferred_element_type=F32)


def _matmul_ksplit(a, w, tm, tn, tk, name):
    m, k = a.shape
    n = w.shape[1]
    tm = min(tm, m)
    return pl.pallas_call(
        _mmk_kernel,
        grid=(m // tm, n // tn, k // tk),
        in_specs=[pl.BlockSpec((tm, tk), lambda i, j, kk: (i, kk)), pl.BlockSpec((tk, tn), lambda i, j, kk: (kk, j))],
        out_specs=pl.BlockSpec((tm, tn), lambda i, j, kk: (i, j)),
        out_shape=jax.ShapeDtypeStruct((m, n), F32),
        compiler_params=_cparams(("parallel", "parallel", "arbitrary")),
        name=name,
    )(a, w)


def _logf_kernel(zs_ref, bf_ref, logf_ref, cum_ref, carry_ref, *, tm):
    @pl.when(pl.program_id(0) == 0)
    def _():
        carry_ref[...] = jnp.zeros_like(carry_ref)

    x = zs_ref[:, SM_FG:SM_FG + H_FOX] + bf_ref[...]
    lf = jnp.minimum(x, 0.0) - jnp.log1p(jnp.exp(-jnp.abs(x)))
    logf_ref[...] = lf
    row = lax.broadcasted_iota(jnp.int32, (tm, tm), 0)
    col = lax.broadcasted_iota(jnp.int32, (tm, tm), 1)
    tri = (col <= row).astype(F32)
    cum = jnp.dot(tri, lf, preferred_element_type=F32, precision=lax.Precision.HIGHEST) + carry_ref[...]
    cum_ref[...] = cum
    carry_ref[...] = cum[tm - 1:tm, :]


def _logf_cumsum(z, b_f, tm=256):
    m = z.shape[0]
    tm = min(tm, m)
    out = jax.ShapeDtypeStruct((m, H_FOX), F32)
    return pl.pallas_call(
        functools.partial(_logf_kernel, tm=tm),
        grid=(m // tm,),
        in_specs=[pl.BlockSpec((tm, 128), lambda i: (i, Z_SM // 128)), pl.BlockSpec((1, H_FOX), lambda i: (0, 0))],
        out_specs=[pl.BlockSpec((tm, H_FOX), lambda i: (i, 0)), pl.BlockSpec((tm, H_FOX), lambda i: (i, 0))],
        out_shape=[out, out],
        scratch_shapes=[pltpu.VMEM((1, H_FOX), F32)],
        compiler_params=_cparams(("arbitrary",)),
        name="logf_cumsum",
    )(z, b_f.reshape(1, H_FOX))


def _split3(x):
    hi = x.astype(BF16).astype(F32)
    mid = (x - hi).astype(BF16).astype(F32)
    lo = (x - hi - mid).astype(BF16).astype(F32)
    return hi, mid, lo


def _lane_row(shape, values):
    lane = lax.broadcasted_iota(jnp.int32, shape, len(shape) - 1)
    out = jnp.zeros(shape, F32)
    for i, v in enumerate(values):
        out = jnp.where(lane == i, v, out)
    return out


def _prep_kernel(fk_ref, fv_ref, dk_ref, dv_ref, iq_ref, sm_ref, cum_ref,
                 fka_ref, fvt_ref, dka_ref, dvt_ref, iqs_ref, ikb_ref, sgn_ref, *, tm):
    i = pl.program_id(0)
    ones_rows = jnp.where(lax.broadcasted_iota(jnp.int32, (VT_ROWS - HEAD_DIM, tm), 0) == 0, 1.0, 0.0).astype(BF16)
    nf = cum_ref[...] * (-LOG2E)
    for h in range(H_FOX):
        sl = slice(h * HEAD_DIM, (h + 1) * HEAD_DIM)
        fka_ref[h, :, :HEAD_DIM] = fk_ref[:, sl].astype(BF16)
        fka_ref[h, :, HEAD_DIM:] = _lane_row((tm, HEAD_DIM), _split3(nf[:, h:h + 1])).astype(BF16)
        fvt_ref[h, :HEAD_DIM, :] = fv_ref[:, sl].T.astype(BF16)
        fvt_ref[h, HEAD_DIM:, :] = ones_rows
    kpos = i * tm + lax.broadcasted_iota(jnp.int32, (tm, 1), 0)
    a = (kpos // 64).astype(F32)
    b = (kpos % 64).astype(F32)
    pos_aug = _lane_row((tm, HEAD_DIM), (a, a, a, b, b, b)).astype(BF16)
    for g in range(H_DSA_KV):
        sl = slice(g * HEAD_DIM, (g + 1) * HEAD_DIM)
        dka_ref[g, :, :HEAD_DIM] = dk_ref[:, sl].astype(BF16)
        dka_ref[g, :, HEAD_DIM:] = pos_aug
        dvt_ref[g, :HEAD_DIM, :] = dv_ref[:, sl].T.astype(BF16)
        dvt_ref[g, HEAD_DIM:, :] = ones_rows
    sm = sm_ref[...]
    w = sm[:, SM_IW:SM_IW + H_IDX] * (H_IDX ** -0.5)
    wabs = jnp.abs(w) * (IDX_DIM ** -0.5)
    for h in range(H_IDX):
        iqs_ref[h] = (iq_ref[:, h * IDX_DIM:(h + 1) * IDX_DIM] * wabs[:, h:h + 1]).astype(BF16)
    ikb_ref[...] = sm[:, SM_IK:SM_IK + IDX_DIM].astype(BF16)
    sgn_ref[...] = jnp.where(sm > 0, 1.0, -1.0).T[SM_IW:SM_IW + H_IDX, :]


def _attn_prep(zf, zd, cum, tm=256):
    s_len = zf.shape[0]
    blk = lambda w, off: pl.BlockSpec((tm, w), lambda i: (i, off // w))
    return pl.pallas_call(
        functools.partial(_prep_kernel, tm=tm),
        grid=(s_len // tm,),
        in_specs=[blk(FOX_W, Z_FK), blk(FOX_W, Z_FV), blk(DSA_KV_W, Z_DK), blk(DSA_KV_W, Z_DV), blk(IQ_W, Z_IQ),
                  blk(128, Z_SM), pl.BlockSpec((tm, H_FOX), lambda i: (i, 0))],
        out_specs=[pl.BlockSpec((H_FOX, tm, 2 * HEAD_DIM), lambda i: (0, i, 0)),
                   pl.BlockSpec((H_FOX, VT_ROWS, tm), lambda i: (0, 0, i)),
                   pl.BlockSpec((H_DSA_KV, tm, 2 * HEAD_DIM), lambda i: (0, i, 0)),
                   pl.BlockSpec((H_DSA_KV, VT_ROWS, tm), lambda i: (0, 0, i)),
                   pl.BlockSpec((H_IDX, tm, IDX_DIM), lambda i: (0, i, 0)),
                   pl.BlockSpec((tm, IDX_DIM), lambda i: (i, 0)),
                   pl.BlockSpec((H_IDX, tm), lambda i: (0, i))],
        out_shape=[jax.ShapeDtypeStruct((H_FOX, s_len, 2 * HEAD_DIM), BF16),
                   jax.ShapeDtypeStruct((H_FOX, VT_ROWS, s_len), BF16),
                   jax.ShapeDtypeStruct((H_DSA_KV, s_len, 2 * HEAD_DIM), BF16),
                   jax.ShapeDtypeStruct((H_DSA_KV, VT_ROWS, s_len), BF16),
                   jax.ShapeDtypeStruct((H_IDX, s_len, IDX_DIM), BF16),
                   jax.ShapeDtypeStruct((s_len, IDX_DIM), BF16),
                   jax.ShapeDtypeStruct((H_IDX, s_len), F32)],
        compiler_params=_cparams(("parallel",)),
        name="attn_prep",
    )(zf, zf, zd, zd, zd, zd, cum)


def _flash_update(s, vt, m_ref, acc_ref):
    _flash_accumulate(*_flash_probs(s, m_ref), vt, acc_ref)


def _flash_probs(s, m_ref):
    m_prev = m_ref[...]
    m_new = jnp.maximum(m_prev, jnp.max(s, axis=0, keepdims=True))
    m_ref[...] = m_new
    return jnp.exp2(m_prev - m_new), jnp.exp2((s - m_new).astype(BF16))


def _flash_accumulate(alpha, p, vt, acc_ref):
    acc_ref[...] = alpha * acc_ref[...] + jnp.dot(vt, p, preferred_element_type=F32)


def _flash_finish(acc):
    return (acc[:HEAD_DIM] / acc[HEAD_DIM:HEAD_DIM + 1]).T


FOX_HEADS_PER_STEP = 8


def _fox_kernel(q_ref, ka_ref, vt_ref, o_ref, qa_ref, m_ref, acc_ref, *, t):
    qi = pl.program_id(1)
    ki = pl.program_id(2)
    heads = range(FOX_HEADS_PER_STEP)

    @pl.when(ki == 0)
    def _():
        for h in heads:
            qa_ref[h, :, :HEAD_DIM] = (
                q_ref[:, h * HEAD_DIM:(h + 1) * HEAD_DIM] * (HEAD_DIM ** -0.5 * LOG2E)).astype(BF16)
            qa_ref[h, :, HEAD_DIM:] = _lane_row((t, HEAD_DIM), (1.0, 1.0, 1.0)).astype(BF16)
        m_ref[...] = jnp.full_like(m_ref, NEG)
        acc_ref[...] = jnp.zeros_like(acc_ref)

    def step(diagonal):
        logits = [lax.dot_general(ka_ref[h], qa_ref[h], _NT, preferred_element_type=F32) for h in heads]
        if diagonal:
            visible = (lax.broadcasted_iota(jnp.int32, (t, t), 0) <= lax.broadcasted_iota(jnp.int32, (t, t), 1))
            logits = [jnp.where(visible, s, NEG) for s in logits]
        probs = [_flash_probs(logits[h], m_ref.at[h]) for h in heads]
        for h in heads:
            _flash_accumulate(*probs[h], vt_ref[h], acc_ref.at[h])

    @pl.when(ki < qi)
    def _():
        step(False)

    @pl.when(ki == qi)
    def _():
        step(True)

    @pl.when(ki == pl.num_programs(2) - 1)
    def _():
        for h in heads:
            o_ref[:, h * HEAD_DIM:(h + 1) * HEAD_DIM] = _flash_finish(acc_ref[h])


def _fox_prompt(z, fka, fvt, t=512):
    s_len = z.shape[0]
    n = s_len // t
    hp = FOX_HEADS_PER_STEP
    w = hp * HEAD_DIM
    return pl.pallas_call(
        functools.partial(_fox_kernel, t=t),
        grid=(H_FOX // hp, n, n),
        in_specs=[pl.BlockSpec((t, w), lambda h, qi, ki: (qi, Z_FQ // w + h)),
                  pl.BlockSpec((hp, t, 2 * HEAD_DIM), lambda h, qi, ki: (h, jnp.minimum(ki, qi), 0)),
                  pl.BlockSpec((hp, VT_ROWS, t), lambda h, qi, ki: (h, 0, jnp.minimum(ki, qi)))],
        out_specs=pl.BlockSpec((t, w), lambda h, qi, ki: (qi, h)),
        out_shape=jax.ShapeDtypeStruct((s_len, FOX_W), F32),
        scratch_shapes=[pltpu.VMEM((hp, t, 2 * HEAD_DIM), BF16), pltpu.VMEM((hp, 1, t), F32),
                        pltpu.VMEM((hp, VT_ROWS, t), F32)],
        compiler_params=_cparams(("parallel", "parallel", "arbitrary")),
        name="fox_prompt",
    )(z, fka, fvt)


def _bisect_threshold(count_ge, n_valid, k_top):
    def cond(carry):
        it, _, cnt = carry
        unsettled = jnp.sum(jnp.where(cnt > k_top, 1, 0))
        return (it < 32) & (unsettled > 0)

    def body(carry):
        it, thr, cnt = carry
        cand = thr + lax.shift_left(jnp.int32(1), jnp.int32(31) - it)
        c = count_ge(cand)
        take = c >= k_top
        return it + 1, jnp.where(take, cand, thr), jnp.where(take, c, cnt)

    init = (jnp.int32(0), jnp.full(n_valid.shape, INT_MIN, jnp.int32), n_valid)
    return lax.while_loop(cond, body, init)[1]


def _idx_kernel(iqs_ref, sgn_ref, ik_ref, bias_ref, key_ref, *, tq, tkc, k_top):
    i = pl.program_id(0)
    s_len = key_ref.shape[0]
    n_chunks = ((i + 1) * tq + tkc - 1) // tkc
    qpos = i * tq + lax.broadcasted_iota(jnp.int32, (tkc, tq), 1)

    def score_chunk(c, carry):
        off = pl.multiple_of(c * tkc, tkc)
        kc = ik_ref[pl.ds(off, tkc), :]
        acc = jnp.zeros((tkc, tq), F32)
        for h in range(H_IDX):
            d = lax.dot_general(kc, iqs_ref[h], _NT, preferred_element_type=F32)
            acc = acc + sgn_ref[h:h + 1, :] * jnp.maximum(d, 0.0)
        kpos = off + lax.broadcasted_iota(jnp.int32, (tkc, tq), 0)
        key_ref[pl.ds(off, tkc), :] = jnp.where(kpos <= qpos, _sort_key(acc), INT_MIN)
        return carry

    lax.fori_loop(0, n_chunks, score_chunk, 0)

    def count_ge(cand):
        def body(c, cnt):
            off = pl.multiple_of(c * tq, tq)
            hit = jnp.where(key_ref[pl.ds(off, tq), :] >= cand, 1, 0)
            return cnt + jnp.sum(hit.reshape(tq // 8, 8, tq), axis=0)

        cnt = lax.fori_loop(0, i + 1, body, jnp.zeros((8, tq), jnp.int32))
        return jnp.sum(cnt, axis=0, keepdims=True)

    n_valid = i * tq + lax.broadcasted_iota(jnp.int32, (1, tq), 1) + 1
    thr = _bisect_threshold(count_ge, n_valid, k_top)
    thr = jnp.maximum(thr, INT_MIN + 1)

    def write_chunk(c, carry):
        off = pl.multiple_of(c * tkc, tkc)
        bias_ref[pl.ds(off, tkc), :] = jnp.where(key_ref[pl.ds(off, tkc), :] >= thr, 0.0, NEG).astype(BF16)
        return carry

    lax.fori_loop(0, n_chunks, write_chunk, 0)

    def fill_chunk(c, carry):
        off = pl.multiple_of(c * tkc, tkc)
        bias_ref[pl.ds(off, tkc), :] = jnp.full((tkc, tq), NEG, BF16)
        return carry

    lax.fori_loop(n_chunks, s_len // tkc, fill_chunk, 0)


def _idx_prompt(iqs, sgn_t, ikb, k_top, tq=256, tkc=256):
    s_len = ikb.shape[0]
    return pl.pallas_call(
        functools.partial(_idx_kernel, tq=tq, tkc=tkc, k_top=k_top),
        grid=(s_len // tq,),
        in_specs=[pl.BlockSpec((H_IDX, tq, IDX_DIM), lambda i: (0, i, 0)),
                  pl.BlockSpec((H_IDX, tq), lambda i: (0, i)),
                  pl.BlockSpec((s_len, IDX_DIM), lambda i: (0, 0))],
        out_specs=pl.BlockSpec((s_len, tq), lambda i: (0, i)),
        out_shape=jax.ShapeDtypeStruct((s_len, s_len), BF16),
        scratch_shapes=[pltpu.VMEM((s_len, tq), jnp.int32)],
        compiler_params=_cparams(("parallel",), 56),
        name="idx_prompt",
    )(iqs, sgn_t, ikb)


def _alibi_slope(h):
    return 2.0 ** (-8.0 * (h + 1) / H_DSA)


def _bf16_terms(x):
    out = []
    for _ in range(3):
        t = float(np.float32(x).astype(BF16))
        out.append(t)
        x = x - t
    return out


def _dsa_kernel(q_ref, ka_ref, vt_ref, bias_ref, o_ref, qa_ref, m_ref, acc_ref, *, tq, tk):
    qi = pl.program_id(0)
    ki = pl.program_id(1)
    ki_last = (qi * tq + tq - 1) // tk

    @pl.when(ki == 0)
    def _():
        for h in range(H_DSA):
            g, j = divmod(h, DSA_GROUP)
            rows = slice(j * tq, (j + 1) * tq)
            qa_ref[g, rows, :HEAD_DIM] = (
                q_ref[:, h * HEAD_DIM:(h + 1) * HEAD_DIM] * (HEAD_DIM ** -0.5 * LOG2E)).astype(BF16)
            sl = _bf16_terms(_alibi_slope(h) * LOG2E)
            qa_ref[g, rows, HEAD_DIM:] = _lane_row((tq, HEAD_DIM), [64.0 * t for t in sl] + sl).astype(BF16)
        m_ref[...] = jnp.full_like(m_ref, NEG)
        acc_ref[...] = jnp.zeros_like(acc_ref)

    @pl.when(ki <= ki_last)
    def _():
        mask = jnp.concatenate([bias_ref[...].astype(F32)] * DSA_GROUP, axis=1)
        groups = range(H_DSA_KV)
        logits = [lax.dot_general(ka_ref[g], qa_ref[g], _NT, preferred_element_type=F32) for g in groups]
        probs = [_flash_probs(logits[g] + mask, m_ref.at[g]) for g in groups]
        for g in groups:
            _flash_accumulate(*probs[g], vt_ref[g], acc_ref.at[g])

    @pl.when(ki == pl.num_programs(1) - 1)
    def _():
        for h in range(H_DSA):
            g, j = divmod(h, DSA_GROUP)
            o_ref[:, h * HEAD_DIM:(h + 1) * HEAD_DIM] = _flash_finish(acc_ref[g, :, j * tq:(j + 1) * tq])


def _dsa_prompt(z, dka, dvt, bias_t, tq=256, tk=512):
    s_len = z.shape[0]

    def kv_blk(qi, ki):
        return jnp.minimum(ki, (qi * tq + tq - 1) // tk)

    return pl.pallas_call(
        functools.partial(_dsa_kernel, tq=tq, tk=tk),
        grid=(s_len // tq, s_len // tk),
        in_specs=[pl.BlockSpec((tq, DSA_W), lambda qi, ki: (qi, Z_DQ // DSA_W)),
                  pl.BlockSpec((H_DSA_KV, tk, 2 * HEAD_DIM), lambda qi, ki: (0, kv_blk(qi, ki), 0)),
                  pl.BlockSpec((H_DSA_KV, VT_ROWS, tk), lambda qi, ki: (0, 0, kv_blk(qi, ki))),
                  pl.BlockSpec((tk, tq), lambda qi, ki: (kv_blk(qi, ki), qi))],
        out_specs=pl.BlockSpec((tq, DSA_W), lambda qi, ki: (qi, 0)),
        out_shape=jax.ShapeDtypeStruct((s_len, DSA_W), F32),
        scratch_shapes=[pltpu.VMEM((H_DSA_KV, DSA_GROUP * tq, 2 * HEAD_DIM), BF16),
                        pltpu.VMEM((H_DSA_KV, 1, DSA_GROUP * tq), F32),
                        pltpu.VMEM((H_DSA_KV, VT_ROWS, DSA_GROUP * tq), F32)],
        compiler_params=_cparams(("parallel", "arbitrary")),
        name="dsa_prompt",
    )(z, dka, dvt, bias_t)


def _dec_idx_kernel(pt_ref, *refs, n_pages, t_new):
    del pt_ref
    ik_refs = refs[:n_pages]
    lf_refs = refs[n_pages:2 * n_pages]
    iq_ref, w_ref, ikn_ref, lfn_ref, key_ref, fneg_ref = refs[2 * n_pages:]

    row = lax.broadcasted_iota(jnp.int32, (PAGE_SIZE, PAGE_SIZE), 0)
    col = lax.broadcasted_iota(jnp.int32, (PAGE_SIZE, PAGE_SIZE), 1)
    tri = (row <= col).astype(F32)
    carry = jnp.zeros((H_FOX, 1), F32)
    for p in range(n_pages + 1):
        lf = lf_refs[p][...] if p < n_pages else lfn_ref[...]
        cum = jnp.dot(lf, tri, preferred_element_type=F32, precision=lax.Precision.HIGHEST) + carry
        fneg_ref[:, p * PAGE_SIZE:(p + 1) * PAGE_SIZE] = -cum
        carry = cum[:, PAGE_SIZE - 1:PAGE_SIZE]

    w = w_ref[...] * (H_IDX ** -0.5)
    iqs = (iq_ref[...] * (jnp.abs(w) * (IDX_DIM ** -0.5))).astype(BF16)
    sgn = jnp.where(w > 0, 1.0, -1.0)
    qrow = lax.broadcasted_iota(jnp.int32, (16, PAGE_SIZE), 0) % t_new
    lane = lax.broadcasted_iota(jnp.int32, (16, PAGE_SIZE), 1)
    for p in range(n_pages + 1):
        kp_t = (ik_refs[p] if p < n_pages else ikn_ref)[...].astype(BF16)
        d = jnp.dot(iqs, kp_t, preferred_element_type=F32)
        sc = jnp.sum((sgn * jnp.maximum(d, 0.0)).reshape(H_IDX, 16, PAGE_SIZE), axis=0)
        key = _sort_key(sc)
        if p == n_pages:
            key = jnp.where((lane <= qrow) & (lane < t_new), key, INT_MIN)
        key_ref[:, p * PAGE_SIZE:(p + 1) * PAGE_SIZE] = key


def _dec_idx(page_table, idx_pages, logf_pages_t, iq16, w16, ik_new_t, logf_new_t, t_new):
    b, n_pages = page_table.shape
    width = (n_pages + 1) * PAGE_SIZE
    page = lambda p: (lambda bi, pt: (pt[bi, p], 0, 0))
    per_b = lambda bi, pt: (bi, 0, 0)
    in_specs = ([pl.BlockSpec((None, IDX_DIM, PAGE_SIZE), page(p)) for p in range(n_pages)]
                + [pl.BlockSpec((None, H_FOX, PAGE_SIZE), page(p)) for p in range(n_pages)]
                + [pl.BlockSpec((None, H_IDX * 16, IDX_DIM), per_b), pl.BlockSpec((None, H_IDX * 16, 1), per_b),
                   pl.BlockSpec((None, IDX_DIM, PAGE_SIZE), per_b), pl.BlockSpec((None, H_FOX, PAGE_SIZE), per_b)])
    return pl.pallas_call(
        functools.partial(_dec_idx_kernel, n_pages=n_pages, t_new=t_new),
        grid_spec=pltpu.PrefetchScalarGridSpec(
            num_scalar_prefetch=1,
            grid=(b,),
            in_specs=in_specs,
            out_specs=[pl.BlockSpec((None, 16, width), per_b), pl.BlockSpec((None, H_FOX, width), per_b)]),
        out_shape=[jax.ShapeDtypeStruct((b, 16, width), jnp.int32), jax.ShapeDtypeStruct((b, H_FOX, width), F32)],
        compiler_params=_cparams(("parallel",)),
        name="dec_idx",
    )(page_table, *([idx_pages] * n_pages), *([logf_pages_t] * n_pages), iq16, w16, ik_new_t, logf_new_t)


def _dec_select_kernel(key_ref, nv_ref, sel_ref, *, k_top):
    def count_ge(cand):
        return jnp.sum(jnp.where(key_ref[...] >= cand, 1, 0), axis=-1, keepdims=True)

    thr = jnp.maximum(_bisect_threshold(count_ge, nv_ref[...], k_top), INT_MIN + 1)
    sel_ref[...] = jnp.where(key_ref[...] >= thr, 0.0, NEG)


def _dec_select(keys, n_valid, k_top, tm=256):
    rows, width = keys.shape
    tm = min(tm, rows)
    return pl.pallas_call(
        functools.partial(_dec_select_kernel, k_top=k_top),
        grid=(rows // tm,),
        in_specs=[pl.BlockSpec((tm, width), lambda i: (i, 0)), pl.BlockSpec((tm, 1), lambda i: (i, 0))],
        out_specs=pl.BlockSpec((tm, width), lambda i: (i, 0)),
        out_shape=jax.ShapeDtypeStruct((rows, width), F32),
        compiler_params=_cparams(("parallel",)),
        name="dec_select",
    )(keys, n_valid)


def _dec_attn_kernel(pt_ref, *refs, n_pages, kp, t_new):
    del pt_ref
    fk_refs, fv_refs, dk_refs, dv_refs = (refs[i * kp:(i + 1) * kp] for i in range(4))
    (fkn_ref, fvn_ref, dkn_ref, dvn_ref, qf_ref, qd_ref, fneg_ref, dbias_ref, fneg_new_ref, dbias_new_ref, slope_ref,
     of_ref, od_ref, mf_ref, lf_ref, accf_ref, md_ref, ld_ref, accd_ref, pairf_ref, paird_ref) = refs[4 * kp:]
    p_id = pl.program_id(1)
    n_steps = n_pages // kp
    rows_q = t_new * H_FOX
    wf = PAGE_SIZE * H_FOX
    wd = PAGE_SIZE * H_DSA_KV

    def pair_mask(n_keys, heads_per_key_head, n_kv):
        qh = lax.broadcasted_iota(jnp.int32, (rows_q, n_keys), 0) % H_FOX
        kh = lax.broadcasted_iota(jnp.int32, (rows_q, n_keys), 1) % n_kv
        return jnp.where(qh // heads_per_key_head == kh, 0.0, NEG)

    @pl.when(p_id == 0)
    def _():
        mf_ref[...] = jnp.full_like(mf_ref, NEG)
        lf_ref[...] = jnp.zeros_like(lf_ref)
        accf_ref[...] = jnp.zeros_like(accf_ref)
        md_ref[...] = jnp.full_like(md_ref, NEG)
        ld_ref[...] = jnp.zeros_like(ld_ref)
        accd_ref[...] = jnp.zeros_like(accd_ref)
        pairf_ref[...] = pair_mask(wf, 1, H_FOX)
        paird_ref[...] = pair_mask(wd, DSA_GROUP, H_DSA_KV)

    def attend(q_ref, k, v, bias, m_ref, l_ref, acc_ref):
        q = (q_ref[...] * (HEAD_DIM ** -0.5)).astype(BF16)
        s = lax.dot_general(q, k, _NT, preferred_element_type=F32) + bias
        m_prev = m_ref[...]
        m_new = jnp.maximum(m_prev, jnp.max(s, axis=-1, keepdims=True))
        alpha = jnp.exp(m_prev - m_new)
        p = jnp.exp(s - m_new)
        l_ref[...] = alpha * l_ref[...] + jnp.sum(p, axis=-1, keepdims=True)
        acc_ref[...] = alpha * acc_ref[...] + jnp.dot(p.astype(BF16), v, preferred_element_type=F32)
        m_ref[...] = m_new

    def sel_rows(db):
        return jnp.concatenate([jnp.broadcast_to(db[q:q + 1], (H_DSA, db.shape[1])) for q in range(t_new)], axis=0)

    def alibi(n_keys, tok0):
        tok = tok0 + lax.broadcasted_iota(jnp.int32, (1, n_keys), 1) // H_DSA_KV
        return slope_ref[...] * tok.astype(F32)

    cat = lambda page_refs: jnp.concatenate([r[...].astype(BF16) for r in page_refs], axis=0)
    bias = jnp.concatenate([pairf_ref[...]] * kp, axis=1) + fneg_ref[...]
    attend(qf_ref, cat(fk_refs), cat(fv_refs), bias, mf_ref, lf_ref, accf_ref)
    bias = (jnp.concatenate([paird_ref[...]] * kp, axis=1) + sel_rows(dbias_ref[...])
            + alibi(kp * wd, (p_id * kp - n_pages) * PAGE_SIZE))
    attend(qd_ref, cat(dk_refs), cat(dv_refs), bias, md_ref, ld_ref, accd_ref)

    @pl.when(p_id == n_steps - 1)
    def _():
        def causal(n_keys, n_kv):
            q = lax.broadcasted_iota(jnp.int32, (rows_q, n_keys), 0) // H_FOX
            tok = lax.broadcasted_iota(jnp.int32, (rows_q, n_keys), 1) // n_kv
            return jnp.where(tok <= q, 0.0, NEG)

        nf = t_new * H_FOX
        bias = pairf_ref[:, :nf] + fneg_new_ref[:, :nf] + causal(nf, H_FOX)
        attend(qf_ref, fkn_ref[...].astype(BF16), fvn_ref[...].astype(BF16), bias, mf_ref, lf_ref, accf_ref)
        nd = t_new * H_DSA_KV
        bias = paird_ref[:, :nd] + sel_rows(dbias_new_ref[:, :nd]) + alibi(nd, 0)
        attend(qd_ref, dkn_ref[...].astype(BF16), dvn_ref[...].astype(BF16), bias, md_ref, ld_ref, accd_ref)
        of_ref[...] = accf_ref[...] / lf_ref[...]
        od_ref[...] = accd_ref[...] / ld_ref[...]


def _dec_attn(page_table, fox_k, fox_v, dsa_k, dsa_v, fk_new, fv_new, dk_new, dv_new, qf, qd, fneg_rows, dbias_rows,
              slopes, t_new):
    b, n_pages = page_table.shape
    kp = 4 if n_pages % 4 == 0 else 1
    n_steps = n_pages // kp
    wf = PAGE_SIZE * H_FOX
    wd = PAGE_SIZE * H_DSA_KV
    rows_q = t_new * H_FOX

    def page(j):
        return lambda bi, p, pt: (pt[bi, p * kp + j], 0, 0)

    per_b = lambda bi, p, pt: (bi, 0, 0)
    past = lambda bi, p, pt: (bi, 0, p)
    new = lambda bi, p, pt: (bi, 0, n_pages)
    in_specs = ([pl.BlockSpec((None, wf, HEAD_DIM), page(j)) for j in range(kp)] * 2
                + [pl.BlockSpec((None, wd, HEAD_DIM), page(j)) for j in range(kp)] * 2
                + [pl.BlockSpec((None, t_new * H_FOX, HEAD_DIM), per_b)] * 2
                + [pl.BlockSpec((None, t_new * H_DSA_KV, HEAD_DIM), per_b)] * 2
                + [pl.BlockSpec((None, rows_q, HEAD_DIM), per_b)] * 2
                + [pl.BlockSpec((None, 1, kp * wf), past), pl.BlockSpec((None, 8, kp * wd), past),
                   pl.BlockSpec((None, 1, wf), new), pl.BlockSpec((None, 8, wd), new),
                   pl.BlockSpec((rows_q, 1), lambda bi, p, pt: (0, 0))])
    out = jax.ShapeDtypeStruct((b, rows_q, HEAD_DIM), F32)
    col = pltpu.VMEM((rows_q, 1), F32)
    acc = pltpu.VMEM((rows_q, HEAD_DIM), F32)
    return pl.pallas_call(
        functools.partial(_dec_attn_kernel, n_pages=n_pages, kp=kp, t_new=t_new),
        grid_spec=pltpu.PrefetchScalarGridSpec(
            num_scalar_prefetch=1,
            grid=(b, n_steps),
            in_specs=in_specs,
            out_specs=[pl.BlockSpec((None, rows_q, HEAD_DIM), per_b)] * 2,
            scratch_shapes=[col, col, acc, col, col, acc, pltpu.VMEM((rows_q, wf), F32),
                            pltpu.VMEM((rows_q, wd), F32)]),
        out_shape=[out, out],
        compiler_params=_cparams(("parallel", "arbitrary")),
        name="dec_attn",
    )(page_table, *([fox_k] * kp), *([fox_v] * kp), *([dsa_k] * kp), *([dsa_v] * kp),
      fk_new, fv_new, dk_new, dv_new, qf, qd, fneg_rows, dbias_rows, fneg_rows, dbias_rows, slopes)


def _ffn_up_kernel(h_ref, hprev_ref, wg_ref, wu_ref, wc_ref, bc_ref, a_ref, tail_ref, *, tm):
    i = pl.program_id(0)
    h = h_ref[...]
    wg = wg_ref[...].astype(BF16)
    g = jnp.dot(h, wg, preferred_element_type=F32)
    u = jnp.dot(h, wu_ref[...].astype(BF16), preferred_element_type=F32)
    gh = jnp.dot(hprev_ref[...], wg, preferred_element_type=F32)
    gh = jnp.where(i > 0, gh, 0.0)
    ext = jnp.concatenate([gh, g], axis=0)
    g1 = pltpu.roll(ext, 1, 0)[16:]
    g2 = pltpu.roll(ext, 2, 0)[16:]
    wc = wc_ref[...]
    conv = bc_ref[...] + wc[0:1] * g2 + wc[1:2] * g1 + wc[2:3] * g
    a_ref[...] = (conv * _sigmoid(conv) * u).astype(a_ref.dtype)
    tail_ref[...] = g[tm - 8:]


def _ffn_up_prompt(h2, w_gate, w_up, w_conv, b_conv, tm=1024, tn=256):
    m, d = h2.shape
    d_ff = w_gate.shape[1]
    tm = min(tm, m)
    return pl.pallas_call(
        functools.partial(_ffn_up_kernel, tm=tm),
        grid=(m // tm, d_ff // tn),
        in_specs=[pl.BlockSpec((tm, d), lambda i, j: (i, 0)),
                  pl.BlockSpec((16, d), lambda i, j: (jnp.maximum(i * (tm // 16) - 1, 0), 0)),
                  pl.BlockSpec((d, tn), lambda i, j: (0, j)), pl.BlockSpec((d, tn), lambda i, j: (0, j)),
                  pl.BlockSpec((CONV_W, tn), lambda i, j: (0, j)), pl.BlockSpec((1, tn), lambda i, j: (0, j))],
        out_specs=[pl.BlockSpec((tm, tn), lambda i, j: (i, j)), pl.BlockSpec((8, tn), lambda i, j: (i, j))],
        out_shape=[jax.ShapeDtypeStruct((m, d_ff), BF16), jax.ShapeDtypeStruct((m // tm * 8, d_ff), F32)],
        compiler_params=_cparams(("parallel", "parallel")),
        name="ffn_up_prompt",
    )(h2, h2, w_gate, w_up, w_conv, b_conv.reshape(1, d_ff))


def _ffn_up_dec_kernel(h_ref, wg_ref, wu_ref, wc_ref, bc_ref, s0_ref, s1_ref, a_ref, g_ref, *, t_new):
    h = h_ref[...]
    g = jnp.dot(h, wg_ref[...].astype(BF16), preferred_element_type=F32)
    u = jnp.dot(h, wu_ref[...].astype(BF16), preferred_element_type=F32)
    g_ref[...] = g
    t = lax.broadcasted_iota(jnp.int32, g.shape, 0) % t_new
    g1 = jnp.where(t >= 1, pltpu.roll(g, 1, 0), 0.0) + s1_ref[...]
    g2 = jnp.where(t >= 2, pltpu.roll(g, 2, 0), 0.0) + s0_ref[...]
    wc = wc_ref[...]
    conv = bc_ref[...] + wc[0:1] * g2 + wc[1:2] * g1 + wc[2:3] * g
    a_ref[...] = (conv * _sigmoid(conv) * u).astype(a_ref.dtype)


def _ffn_up_dec(h2, w_gate, w_up, w_conv, b_conv, tap0, tap1, t_new, tn=256):
    m, d = h2.shape
    d_ff = w_gate.shape[1]
    col = lambda j: (0, j)
    return pl.pallas_call(
        functools.partial(_ffn_up_dec_kernel, t_new=t_new),
        grid=(d_ff // tn,),
        in_specs=[pl.BlockSpec((m, d), lambda j: (0, 0)), pl.BlockSpec((d, tn), col), pl.BlockSpec((d, tn), col),
                  pl.BlockSpec((CONV_W, tn), col), pl.BlockSpec((1, tn), col),
                  pl.BlockSpec((m, tn), col), pl.BlockSpec((m, tn), col)],
        out_specs=[pl.BlockSpec((m, tn), col), pl.BlockSpec((m, tn), col)],
        out_shape=[jax.ShapeDtypeStruct((m, d_ff), BF16), jax.ShapeDtypeStruct((m, d_ff), F32)],
        compiler_params=_cparams(("parallel",)),
        name="ffn_up_dec",
    )(h2, w_gate, w_up, w_conv, b_conv.reshape(1, d_ff), tap0, tap1)


def _split_mod(mod):
    return jnp.split(mod, 6, axis=-1)


def _attn_out_and_ffn_in(x, o_f, o_d, mods, p):
    _, _, gt1, sh2, sc2, _ = mods
    a = _attn_norm(o_f, o_d, p["g_fox_out"], p["g_dsa_out"])
    o = _matmul(a, p["w_out"], 1024, 512, "out_proj")
    return _res_pre(o, x, gt1, p["g_post_attn"], p["g_pre_ffn"], sc2, sh2)


def _ffn_down_and_res(a, x1, mods, p):
    d_ff = a.shape[1]
    f = _matmul_ksplit(a, p["w_down"], 1024, 512, d_ff // 2, "ffn_down")
    return _res(f, x1, mods[5], p["g_post_ffn"])


def _project(x, mods, p):
    sh1, sc1 = mods[0], mods[1]
    h = _prenorm(x, p["g_pre_attn"], sc1, sh1)
    return (_matmul_nt(h, p["w_in_t"], 1024, ZF_TN, "in_proj_fox", n=ZF_W),
            _matmul_nt(h, p["w_in_tail_t"], 1024, ZD_TN, "in_proj_rest"))


def _kv_states(zf, zd, logf, conv):
    return (zf[:, Z_FK:Z_FK + FOX_W], zf[:, Z_FV:Z_FV + FOX_W], logf, zd[:, Z_DK:Z_DK + DSA_KV_W],
            zd[:, Z_DV:Z_DV + DSA_KV_W], zd[:, Z_SM + SM_IK:Z_SM + SM_IK + IDX_DIM], conv)


def _prompt_layer(x, mods, p):
    s_len = x.shape[0]
    zf, zd = _project(x, mods, p)
    logf, cum = _logf_cumsum(zd, p["b_f"])
    fka, fvt, dka, dvt, iqs, ikb, sgn_t = _attn_prep(zf, zd, cum)
    o_f = _fox_prompt(zf, fka, fvt)
    bias_t = _idx_prompt(iqs, sgn_t, ikb, min(TOPK_MAX, s_len // 4))
    o_d = _dsa_prompt(zd, dka, dvt, bias_t)
    x1, h2 = _attn_out_and_ffn_in(x, o_f, o_d, mods, p)
    a, tails = _ffn_up_prompt(h2, p["w_gate"], p["w_up"], p["w_conv"], p["b_conv"])
    y = _ffn_down_and_res(a, x1, mods, p)
    return y, _kv_states(zf, zd, logf, tails[-(CONV_W - 1):])


def _pad_rows(x, rows):
    return jnp.pad(x, ((0, 0), (0, rows - x.shape[1]), (0, 0)))


def _decode_layer(x, mods, p, page_table, caches, state_conv, t_new):
    m = x.shape[0]
    b = m // t_new
    n_pages = page_table.shape[1]
    cache_fox_k, cache_fox_v, cache_fox_logf_t, cache_dsa_k, cache_dsa_v, cache_idx_k = caches
    zf, zd = _project(x, mods, p)
    logf, _ = _logf_cumsum(zd, p["b_f"])
    zf3 = zf.reshape(b, t_new, ZF_W)
    zd3 = zd.reshape(b, t_new, ZD_W)

    rep = 16 // t_new
    iq16 = jnp.tile(zd3[:, :, Z_IQ:Z_IQ + IQ_W].reshape(b, t_new, H_IDX, IDX_DIM).transpose(0, 2, 1, 3),
                    (1, 1, rep, 1)).reshape(b, H_IDX * 16, IDX_DIM)
    w16 = jnp.tile(zd3[:, :, Z_SM + SM_IW:Z_SM + SM_IW + H_IDX].transpose(0, 2, 1), (1, 1, rep)).reshape(
        b, H_IDX * 16, 1)
    ik_new_t = jnp.pad(zd3[:, :, Z_SM + SM_IK:Z_SM + SM_IK + IDX_DIM].transpose(0, 2, 1),
                       ((0, 0), (0, 0), (0, PAGE_SIZE - t_new)))
    logf_new_t = jnp.pad(logf.reshape(b, t_new, H_FOX).transpose(0, 2, 1), ((0, 0), (0, 0), (0, PAGE_SIZE - t_new)))
    k_top = min(TOPK_MAX, (n_pages * PAGE_SIZE + t_new) // 4)
    keys, fneg = _dec_idx(page_table, cache_idx_k, cache_fox_logf_t, iq16, w16, ik_new_t, logf_new_t, t_new)
    n_valid = jnp.tile(n_pages * PAGE_SIZE + 1 + jnp.arange(8, dtype=jnp.int32) % t_new, b).reshape(b * 8, 1)
    dbias = _dec_select(keys[:, :8].reshape(b * 8, -1), n_valid, k_top).reshape(b, 8, -1)

    heads = lambda z3, off, w: z3[:, :, off:off + w].reshape(b, t_new * (w // HEAD_DIM), HEAD_DIM)
    fneg_rows = fneg.transpose(0, 2, 1).reshape(b, 1, -1)
    dbias_rows = jnp.repeat(dbias, H_DSA_KV, axis=-1)
    slopes = jnp.tile(2.0 ** (-8.0 * jnp.arange(1, H_DSA + 1, dtype=F32) / H_DSA), t_new).reshape(H_DSA * t_new, 1)
    o_f, o_d = _dec_attn(
        page_table, cache_fox_k, cache_fox_v, cache_dsa_k, cache_dsa_v,
        heads(zf3, Z_FK, FOX_W), heads(zf3, Z_FV, FOX_W), heads(zd3, Z_DK, DSA_KV_W), heads(zd3, Z_DV, DSA_KV_W),
        heads(zf3, Z_FQ, FOX_W), heads(zd3, Z_DQ, DSA_W), fneg_rows, dbias_rows, slopes, t_new)

    x1, h2 = _attn_out_and_ffn_in(x, o_f.reshape(m, FOX_W), o_d.reshape(m, DSA_W), mods, p)
    d_ff = state_conv.shape[-1]
    zero = jnp.zeros((b, 1, d_ff), F32)
    s0, s1 = state_conv[:, 0:1], state_conv[:, 1:2]
    tap0 = jnp.concatenate([s0, s1] + [zero] * (t_new - 2), axis=1).reshape(m, d_ff)
    tap1 = jnp.concatenate([s1] + [zero] * (t_new - 1), axis=1).reshape(m, d_ff)
    a, g = _ffn_up_dec(h2, p["w_gate"], p["w_up"], p["w_conv"], p["b_conv"], tap0, tap1, t_new)
    y = _ffn_down_and_res(a, x1, mods, p)
    conv_new = g.reshape(b, t_new, d_ff)[:, t_new - (CONV_W - 1):]
    return y, _kv_states(zf, zd, logf, conv_new)


def _regroup_w_in_tail(w_in_t):
    n, d = w_in_t.shape
    rest = w_in_t[ZF_W + H_FOX:].astype(BF16)
    fg = w_in_t[ZF_W:ZF_W + H_FOX].astype(BF16)
    assert n - ZF_W - H_FOX == Z_SM + SM_IW + H_IDX and SM_FG == SM_IW + H_IDX
    return jnp.concatenate([rest, fg, jnp.zeros((ZD_W - (n - ZF_W), d), BF16)], axis=0)


def kernel(x_prompt, x_sample, c_prompt, c_sample, page_table, cache_fox_k, cache_fox_v, cache_fox_logf, cache_dsa_k, cache_dsa_v, cache_idx_k, state_conv, w_in, b_f, w_out, g_fox_out, g_dsa_out, w_mod, b_mod, g_pre_attn, g_post_attn, g_pre_ffn, g_post_ffn, w_gate, w_up, w_conv, b_conv, w_down):
    depth = w_in.shape[0]
    bp, s_len, d = x_prompt.shape
    bs, t_new, _ = x_sample.shape
    assert bp == 1 and t_new >= CONV_W - 1 and 16 % t_new == 0

    xp = x_prompt.reshape(s_len, d)
    xs = x_sample.reshape(bs * t_new, d)
    n_c = bp + bs
    c_all = jnp.pad(jnp.concatenate([c_prompt, c_sample], axis=0), ((0, -n_c % 8), (0, 0)))
    n_phys = cache_fox_k.shape[1]
    fold = lambda c: c.reshape((depth * n_phys,) + c.shape[2:])
    rows = lambda c: c.reshape(depth * n_phys, PAGE_SIZE * c.shape[3], HEAD_DIM)
    caches = (rows(cache_fox_k), rows(cache_fox_v), fold(cache_fox_logf).transpose(0, 2, 1), rows(cache_dsa_k),
              rows(cache_dsa_v), fold(cache_idx_k).transpose(0, 2, 1))
    p_states, s_states = [], []
    for l in range(depth):
        w_in_t = w_in[l].T
        p = dict(w_in_t=w_in_t, w_in_tail_t=_regroup_w_in_tail(w_in_t), b_f=b_f[l], w_out=w_out[l].astype(BF16),
                 g_fox_out=g_fox_out[l][None], g_dsa_out=g_dsa_out[l][None],
                 g_pre_attn=g_pre_attn[l][None], g_post_attn=g_post_attn[l][None],
                 g_pre_ffn=g_pre_ffn[l][None], g_post_ffn=g_post_ffn[l][None],
                 w_gate=w_gate[l], w_up=w_up[l], w_conv=w_conv[l], b_conv=b_conv[l],
                 w_down=w_down[l].astype(BF16))
        mod = _modulation(c_all, w_mod[l], b_mod[l])
        mods_p = _split_mod(mod[:bp])
        mods_s = [jnp.repeat(v, t_new, axis=0) for v in _split_mod(mod[bp:n_c])]
        xp, st_p = _prompt_layer(xp, mods_p, p)
        xs, st_s = _decode_layer(xs, mods_s, p, page_table + l * n_phys, caches, state_conv[l], t_new)
        p_states.append(st_p)
        s_states.append(st_s)

    def stack(states, i, shape):
        return jnp.stack([st[i].reshape(shape) for st in states])

    d_ff = state_conv.shape[-1]
    outs = [xp.reshape(bp, s_len, d), xs.reshape(bs, t_new, d)]
    for states, (bb, tt) in ((p_states, (bp, s_len)), (s_states, (bs, t_new))):
        outs += [stack(states, 0, (bb, tt, H_FOX, HEAD_DIM)), stack(states, 1, (bb, tt, H_FOX, HEAD_DIM)),
                 stack(states, 2, (bb, tt, H_FOX)), stack(states, 3, (bb, tt, H_DSA_KV, HEAD_DIM)),
                 stack(states, 4, (bb, tt, H_DSA_KV, HEAD_DIM)), stack(states, 5, (bb, tt, IDX_DIM)),
                 stack(states, 6, (bb, CONV_W - 1, d_ff))]
    return tuple(outs)
```

```python
import functools

import jax
import jax.numpy as jnp
import numpy as np
from jax import lax
from jax.experimental import pallas as pl
from jax.experimental.pallas import tpu as pltpu

HEAD_DIM = 128
H_FOX = 16
H_DSA = 16
H_DSA_KV = 4
DSA_GROUP = H_DSA // H_DSA_KV
H_IDX = 16
IDX_DIM = 64
TOPK_MAX = 256
CONV_W = 3
PAGE_SIZE = 128
RMS_EPS = 1e-6
FOX_W = H_FOX * HEAD_DIM
DSA_W = H_DSA * HEAD_DIM
DSA_KV_W = H_DSA_KV * HEAD_DIM
IQ_W = H_IDX * IDX_DIM

Z_FQ = 0
Z_FK = Z_FQ + FOX_W
Z_FV = Z_FK + FOX_W
ZF_W = Z_FV + FOX_W
ZF_TN = 512
Z_DQ = 0
Z_DK = Z_DQ + DSA_W
Z_DV = Z_DK + DSA_KV_W
Z_IQ = Z_DV + DSA_KV_W
Z_SM = Z_IQ + IQ_W
SM_IK = 0
SM_IW = SM_IK + IDX_DIM
SM_FG = SM_IW + H_IDX
ZD_TN = 768
ZD_W = -(-(Z_SM + 128) // ZD_TN) * ZD_TN

NEG = -1e30
LOG2E = 1.4426950408889634
VT_ROWS = HEAD_DIM + 16
INT_MIN = -2 ** 31
MIB = 1024 * 1024
BF16 = jnp.bfloat16
F32 = jnp.float32

_NT = (((1,), (1,)), ((), ()))


def _cparams(sem, vmem_mib=48):
    return pltpu.CompilerParams(dimension_semantics=sem, vmem_limit_bytes=vmem_mib * MIB)


def _rms(x, g):
    return x * lax.rsqrt(jnp.mean(x * x, axis=-1, keepdims=True) + RMS_EPS) * g


def _sigmoid(x):
    return 1.0 / (1.0 + jnp.exp(-x))


def _sort_key(x):
    b = pltpu.bitcast(x, jnp.int32)
    return b ^ ((b >> 31) & jnp.int32(0x7FFFFFFF))


def _row_spec(arr, tm):
    d = arr.shape[1]
    if arr.shape[0] == 1:
        return pl.BlockSpec((1, d), lambda i: (0, 0))
    return pl.BlockSpec((tm, d), lambda i: (i, 0))


def _mod_kernel(c_ref, w_ref, b_ref, o_ref):
    c = c_ref[...]
    a = (c * _sigmoid(c)).astype(BF16)
    o_ref[...] = jnp.dot(a, w_ref[...].astype(BF16), preferred_element_type=F32) + b_ref[...]


def _modulation(c, w_mod, b_mod):
    r, d = c.shape
    n = w_mod.shape[1]
    tn = 512
    return pl.pallas_call(
        _mod_kernel,
        grid=(n // tn,),
        in_specs=[pl.BlockSpec((r, d), lambda j: (0, 0)),
                  pl.BlockSpec((d, tn), lambda j: (0, j)),
                  pl.BlockSpec((1, tn), lambda j: (0, j))],
        out_specs=pl.BlockSpec((r, tn), lambda j: (0, j)),
        out_shape=jax.ShapeDtypeStruct((r, n), F32),
        compiler_params=_cparams(("parallel",)),
        name="modulation",
    )(c, w_mod, b_mod.reshape(1, n))


def _prenorm_kernel(x_ref, g_ref, sc_ref, sh_ref, o_ref):
    y = _rms(x_ref[...], g_ref[...])
    o_ref[...] = (y * (1.0 + sc_ref[...]) + sh_ref[...]).astype(o_ref.dtype)


def _prenorm(x, g, sc, sh, tm=256):
    m, d = x.shape
    tm = min(tm, m)
    return pl.pallas_call(
        _prenorm_kernel,
        grid=(m // tm,),
        in_specs=[pl.BlockSpec((tm, d), lambda i: (i, 0)), _row_spec(g, tm), _row_spec(sc, tm), _row_spec(sh, tm)],
        out_specs=pl.BlockSpec((tm, d), lambda i: (i, 0)),
        out_shape=jax.ShapeDtypeStruct((m, d), BF16),
        compiler_params=_cparams(("parallel",)),
        name="prenorm",
    )(x, g, sc, sh)


def _attn_norm_kernel(of_ref, od_ref, gf_ref, gd_ref, o_ref):
    o_ref[:, :FOX_W] = _rms(of_ref[...], gf_ref[...]).astype(o_ref.dtype)
    o_ref[:, FOX_W:] = _rms(od_ref[...], gd_ref[...]).astype(o_ref.dtype)


def _attn_norm(o_f, o_d, g_f, g_d, tm=256):
    m = o_f.shape[0]
    tm = min(tm, m)
    return pl.pallas_call(
        _attn_norm_kernel,
        grid=(m // tm,),
        in_specs=[pl.BlockSpec((tm, FOX_W), lambda i: (i, 0)), pl.BlockSpec((tm, DSA_W), lambda i: (i, 0)),
                  _row_spec(g_f, tm), _row_spec(g_d, tm)],
        out_specs=pl.BlockSpec((tm, FOX_W + DSA_W), lambda i: (i, 0)),
        out_shape=jax.ShapeDtypeStruct((m, FOX_W + DSA_W), BF16),
        compiler_params=_cparams(("parallel",)),
        name="attn_norm",
    )(o_f, o_d, g_f, g_d)


def _res_pre_kernel(o_ref, x_ref, gt_ref, gpost_ref, gpre_ref, sc_ref, sh_ref, x1_ref, h2_ref):
    x1 = x_ref[...] + gt_ref[...] * _rms(o_ref[...], gpost_ref[...])
    x1_ref[...] = x1
    h2_ref[...] = (_rms(x1, gpre_ref[...]) * (1.0 + sc_ref[...]) + sh_ref[...]).astype(h2_ref.dtype)


def _res_pre(o, x, gt, g_post, g_pre, sc, sh, tm=128):
    m, d = x.shape
    tm = min(tm, m)
    blk =pl.BlockSpec((tm, d), lambda i: (i, 0))
    return pl.pallas_call(
        _res_pre_kernel,
        grid=(m // tm,),
        in_specs=[blk, blk, _row_spec(gt, tm), _row_spec(g_post, tm), _row_spec(g_pre, tm),
                  _row_spec(sc, tm), _row_spec(sh, tm)],
        out_specs=[blk, blk],
        out_shape=[jax.ShapeDtypeStruct((m, d), F32), jax.ShapeDtypeStruct((m, d), BF16)],
        compiler_params=_cparams(("parallel",)),
        name="res_pre",
    )(o, x, gt, g_post, g_pre, sc, sh)


def _res_kernel(f_ref, x_ref, gt_ref, gpost_ref, y_ref):
    y_ref[...] = x_ref[...] + gt_ref[...] * _rms(f_ref[...], gpost_ref[...])


def _res(f, x, gt, g_post, tm=256):
    m, d = x.shape
    tm = min(tm, m)
    blk =pl.BlockSpec((tm, d), lambda i: (i, 0))
    return pl.pallas_call(
        _res_kernel,
        grid=(m // tm,),
        in_specs=[blk, blk, _row_spec(gt, tm), _row_spec(g_post, tm)],
        out_specs=blk,
        out_shape=jax.ShapeDtypeStruct((m, d), F32),
        compiler_params=_cparams(("parallel",)),
        name="res",
    )(f, x, gt, g_post)


def _mm_kernel(a_ref, w_ref, o_ref):
    o_ref[...] = jnp.dot(a_ref[...], w_ref[...], preferred_element_type=F32)


def _matmul(a, w, tm, tn, name):
    m, k = a.shape
    n = w.shape[1]
    tm = min(tm, m)
    return pl.pallas_call(
        _mm_kernel,
        grid=(m // tm, n // tn),
        in_specs=[pl.BlockSpec((tm, k), lambda i, j: (i, 0)), pl.BlockSpec((k, tn), lambda i, j: (0, j))],
        out_specs=pl.BlockSpec((tm, tn), lambda i, j: (i, j)),
        out_shape=jax.ShapeDtypeStruct((m, n), F32),
        compiler_params=_cparams(("parallel", "parallel")),
        name=name,
    )(a, w)


def _mm_nt_kernel(a_ref, wt_ref, o_ref):
    o_ref[...] = lax.dot_general(a_ref[...], wt_ref[...].astype(BF16), _NT, preferred_element_type=F32)


def _matmul_nt(a, w_t, tm, tn, name, n=None):
    m, k = a.shape
    n = w_t.shape[0] if n is None else n
    tm = min(tm, m)
    return pl.pallas_call(
        _mm_nt_kernel,
        grid=(m // tm, n // tn),
        in_specs=[pl.BlockSpec((tm, k), lambda i, j: (i, 0)), pl.BlockSpec((tn, k), lambda i, j: (j, 0))],
        out_specs=pl.BlockSpec((tm, tn), lambda i, j: (i, j)),
        out_shape=jax.ShapeDtypeStruct((m, n), F32),
        compiler_params=_cparams(("parallel", "parallel")),
        name=name,
    )(a, w_t)


def _mmk_kernel(a_ref, w_ref, o_ref):
    @pl.when(pl.program_id(2) == 0)
    def _():
        o_ref[...] = jnp.zeros_like(o_ref)

    o_ref[...] += jnp.dot(a_ref[...], w_ref[...], preferred_element_type=F32)


def _matmul_ksplit(a, w, tm, tn, tk, name):
    m, k = a.shape
    n = w.shape[1]
    tm = min(tm, m)
    return pl.pallas_call(
        _mmk_kernel,
        grid=(m // tm, n // tn, k // tk),
        in_specs=[pl.BlockSpec((tm, tk), lambda i, j, kk: (i, kk)), pl.BlockSpec((tk, tn), lambda i, j, kk: (kk, j))],
        out_specs=pl.BlockSpec((tm, tn), lambda i, j, kk: (i, j)),
        out_shape=jax.ShapeDtypeStruct((m, n), F32),
        compiler_params=_cparams(("parallel", "parallel", "arbitrary")),
        name=name,
    )(a, w)


def _logf_kernel(zs_ref, bf_ref, logf_ref, cum_ref, carry_ref, *, tm):
    @pl.when(pl.program_id(0) == 0)
    def _():
        carry_ref[...] = jnp.zeros_like(carry_ref)

    x = zs_ref[:, SM_FG:SM_FG + H_FOX] + bf_ref[...]
    lf = jnp.minimum(x, 0.0) - jnp.log1p(jnp.exp(-jnp.abs(x)))
    logf_ref[...] = lf
    row = lax.broadcasted_iota(jnp.int32, (tm, tm), 0)
    col = lax.broadcasted_iota(jnp.int32, (tm, tm), 1)
    tri = (col <= row).astype(F32)
    cum = jnp.dot(tri, lf, preferred_element_type=F32, precision=lax.Precision.HIGHEST) + carry_ref[...]
    cum_ref[...] = cum
    carry_ref[...] = cum[tm - 1:tm, :]


def _logf_cumsum(z, b_f, tm=256):
    m = z.shape[0]
    tm = min(tm, m)
    out = jax.ShapeDtypeStruct((m, H_FOX), F32)
    return pl.pallas_call(
        functools.partial(_logf_kernel, tm=tm),
        grid=(m // tm,),
        in_specs=[pl.BlockSpec((tm, 128), lambda i: (i, Z_SM // 128)), pl.BlockSpec((1, H_FOX), lambda i: (0, 0))],
        out_specs=[pl.BlockSpec((tm, H_FOX), lambda i: (i, 0)), pl.BlockSpec((tm, H_FOX), lambda i: (i, 0))],
        out_shape=[out, out],
        scratch_shapes=[pltpu.VMEM((1, H_FOX), F32)],
        compiler_params=_cparams(("arbitrary",)),
        name="logf_cumsum",
    )(z, b_f.reshape(1, H_FOX))


def _split3(x):
    hi = x.astype(BF16).astype(F32)
    mid = (x - hi).astype(BF16).astype(F32)
    lo = (x - hi - mid).astype(BF16).astype(F32)
    return hi, mid, lo


IDX_K = 3 * IDX_DIM


def _split_operand(x, axis, left):
    hi = x.astype(BF16)
    lo = (x - hi.astype(F32)).astype(BF16)
    return jnp.concatenate([hi, hi, lo] if left else [hi, lo, hi], axis=axis)


def _lane_row(shape, values):
    lane = lax.broadcasted_iota(jnp.int32, shape, len(shape) - 1)
    out = jnp.zeros(shape, F32)
    for i, v in enumerate(values):
        out = jnp.where(lane == i, v, out)
    return out


def _prep_kernel(fk_ref, fv_ref, dk_ref, dv_ref, iq_ref, sm_ref, cum_ref,
                 fka_ref, fvt_ref, dka_ref, dvt_ref, iqs_ref, ikb_ref, sgn_ref, *, tm):
    i = pl.program_id(0)
    ones_rows = jnp.where(lax.broadcasted_iota(jnp.int32, (VT_ROWS - HEAD_DIM, tm), 0) == 0, 1.0, 0.0).astype(BF16)
    nf = cum_ref[...] * (-LOG2E)
    for h in range(H_FOX):
        sl = slice(h * HEAD_DIM, (h + 1) * HEAD_DIM)
        fka_ref[h, :, :HEAD_DIM] = fk_ref[:, sl].astype(BF16)
        fka_ref[h, :, HEAD_DIM:] = _lane_row((tm, HEAD_DIM), _split3(nf[:, h:h + 1])).astype(BF16)
        fvt_ref[h, :HEAD_DIM, :] = fv_ref[:, sl].T.astype(BF16)
        fvt_ref[h, HEAD_DIM:, :] = ones_rows
    kpos = i * tm + lax.broadcasted_iota(jnp.int32, (tm, 1), 0)
    a = (kpos // 64).astype(F32)
    b = (kpos % 64).astype(F32)
    pos_aug = _lane_row((tm, HEAD_DIM), (a, a, a, b, b, b)).astype(BF16)
    for g in range(H_DSA_KV):
        sl = slice(g * HEAD_DIM, (g + 1) * HEAD_DIM)
        dka_ref[g, :, :HEAD_DIM] = dk_ref[:, sl].astype(BF16)
        dka_ref[g, :, HEAD_DIM:] = pos_aug
        dvt_ref[g, :HEAD_DIM, :] = dv_ref[:, sl].T.astype(BF16)
        dvt_ref[g, HEAD_DIM:, :] = ones_rows
    sm = sm_ref[...]
    w = sm[:, SM_IW:SM_IW + H_IDX] * (H_IDX ** -0.5)
    wabs = jnp.abs(w) * (IDX_DIM ** -0.5)
    for h in range(H_IDX):
        iqs_ref[h] = _split_operand(iq_ref[:, h * IDX_DIM:(h + 1) * IDX_DIM] * wabs[:, h:h + 1], 1, True)
    ikb_ref[...] = _split_operand(sm[:, SM_IK:SM_IK + IDX_DIM], 1, False)
    sgn_ref[...] = jnp.where(sm > 0, 1.0, -1.0).T[SM_IW:SM_IW + H_IDX, :]


def _attn_prep(zf, zd, cum, tm=256):
    s_len = zf.shape[0]
    blk = lambda w, off: pl.BlockSpec((tm, w), lambda i: (i, off // w))
    return pl.pallas_call(
        functools.partial(_prep_kernel, tm=tm),
        grid=(s_len // tm,),
        in_specs=[blk(FOX_W, Z_FK), blk(FOX_W, Z_FV), blk(DSA_KV_W, Z_DK), blk(DSA_KV_W, Z_DV), blk(IQ_W, Z_IQ),
                  blk(128, Z_SM), pl.BlockSpec((tm, H_FOX), lambda i: (i, 0))],
        out_specs=[pl.BlockSpec((H_FOX, tm, 2 * HEAD_DIM), lambda i: (0, i, 0)),
                   pl.BlockSpec((H_FOX, VT_ROWS, tm), lambda i: (0, 0, i)),
                   pl.BlockSpec((H_DSA_KV, tm, 2 * HEAD_DIM), lambda i: (0, i, 0)),
                   pl.BlockSpec((H_DSA_KV, VT_ROWS, tm), lambda i: (0, 0, i)),
                   pl.BlockSpec((H_IDX, tm, IDX_K), lambda i: (0, i, 0)),
                   pl.BlockSpec((tm, IDX_K), lambda i: (i, 0)),
                   pl.BlockSpec((H_IDX, tm), lambda i: (0, i))],
        out_shape=[jax.ShapeDtypeStruct((H_FOX, s_len, 2 * HEAD_DIM), BF16),
                   jax.ShapeDtypeStruct((H_FOX, VT_ROWS, s_len), BF16),
                   jax.ShapeDtypeStruct((H_DSA_KV, s_len, 2 * HEAD_DIM), BF16),
                   jax.ShapeDtypeStruct((H_DSA_KV, VT_ROWS, s_len), BF16),
                   jax.ShapeDtypeStruct((H_IDX, s_len, IDX_K), BF16),
                   jax.ShapeDtypeStruct((s_len, IDX_K), BF16),
                   jax.ShapeDtypeStruct((H_IDX, s_len), F32)],
        compiler_params=_cparams(("parallel",)),
        name="attn_prep",
    )(zf, zf, zd, zd, zd, zd, cum)


def _flash_update(s, vt, m_ref, acc_ref):
    _flash_accumulate(*_flash_probs(s, m_ref), vt, acc_ref)


def _flash_probs(s, m_ref):
    m_prev = m_ref[...]
    m_new = jnp.maximum(m_prev, jnp.max(s, axis=0, keepdims=True))
    m_ref[...] = m_new
    return jnp.exp2(m_prev - m_new), jnp.exp2((s - m_new).astype(BF16))


def _flash_accumulate(alpha, p, vt, acc_ref):
    acc_ref[...] = alpha * acc_ref[...] + jnp.dot(vt, p, preferred_element_type=F32)


def _flash_finish(acc):
    return (acc[:HEAD_DIM] / acc[HEAD_DIM:HEAD_DIM + 1]).T


FOX_HEADS_PER_STEP = 8


def _causal_steps(n_q, last_key_block):
    steps = [(qi, ki) for qi in range(n_q) for ki in range(last_key_block(qi) + 1)]
    return (np.asarray([qi for qi, _ in steps], np.int32), np.asarray([ki for _, ki in steps], np.int32))


def _fox_kernel(qi_ref, ki_ref, q_ref, ka_ref, vt_ref, o_ref, qa_ref, m_ref, acc_ref, *, t):
    qi = qi_ref[pl.program_id(1)]
    ki = ki_ref[pl.program_id(1)]
    heads = range(FOX_HEADS_PER_STEP)

    @pl.when(ki == 0)
    def _():
        for h in heads:
            qa_ref[h, :, :HEAD_DIM] = (
                q_ref[:, h * HEAD_DIM:(h + 1) * HEAD_DIM] * (HEAD_DIM ** -0.5 * LOG2E)).astype(BF16)
            qa_ref[h, :, HEAD_DIM:] = _lane_row((t, HEAD_DIM), (1.0, 1.0, 1.0)).astype(BF16)
        m_ref[...] = jnp.full_like(m_ref, NEG)
        acc_ref[...] = jnp.zeros_like(acc_ref)

    def step(diagonal):
        logits = [lax.dot_general(ka_ref[h], qa_ref[h], _NT, preferred_element_type=F32) for h in heads]
        if diagonal:
            visible = (lax.broadcasted_iota(jnp.int32, (t, t), 0) <= lax.broadcasted_iota(jnp.int32, (t, t), 1))
            logits = [jnp.where(visible, s, NEG) for s in logits]
        probs = [_flash_probs(logits[h], m_ref.at[h]) for h in heads]
        for h in heads:
            _flash_accumulate(*probs[h], vt_ref[h], acc_ref.at[h])

    @pl.when(ki < qi)
    def _():
        step(False)

    @pl.when(ki == qi)
    def _():
        step(True)
        for h in heads:
            o_ref[:, h * HEAD_DIM:(h + 1) * HEAD_DIM] = _flash_finish(acc_ref[h])


def _fox_prompt(z, fka, fvt, t=512):
    s_len = z.shape[0]
    n = s_len // t
    hp = FOX_HEADS_PER_STEP
    w = hp * HEAD_DIM
    q_steps, k_steps = _causal_steps(n, lambda qi: qi)
    return pl.pallas_call(
        functools.partial(_fox_kernel, t=t),
        grid_spec=pltpu.PrefetchScalarGridSpec(
            num_scalar_prefetch=2,
            grid=(H_FOX // hp, q_steps.shape[0]),
            in_specs=[pl.BlockSpec((t, w), lambda h, s, qs, ks: (qs[s], Z_FQ // w + h)),
                      pl.BlockSpec((hp, t, 2 * HEAD_DIM), lambda h, s, qs, ks: (h, ks[s], 0)),
                      pl.BlockSpec((hp, VT_ROWS, t), lambda h, s, qs, ks: (h, 0, ks[s]))],
            out_specs=pl.BlockSpec((t, w), lambda h, s, qs, ks: (qs[s], h)),
            scratch_shapes=[pltpu.VMEM((hp, t, 2 * HEAD_DIM), BF16), pltpu.VMEM((hp, 1, t), F32),
                            pltpu.VMEM((hp, VT_ROWS, t), F32)]),
        out_shape=jax.ShapeDtypeStruct((s_len, FOX_W), F32),
        compiler_params=_cparams(("parallel", "arbitrary")),
        name="fox_prompt",
    )(q_steps, k_steps, z, fka, fvt)


def _bisect_threshold(count_ge, n_valid, k_top):
    def cond(carry):
        it, _, cnt = carry
        unsettled = jnp.sum(jnp.where(cnt > k_top, 1, 0))
        return (it < 32) & (unsettled > 0)

    def body(carry):
        it, thr, cnt = carry
        cand = thr + lax.shift_left(jnp.int32(1), jnp.int32(31) - it)
        c = count_ge(cand)
        take = c >= k_top
        return it + 1, jnp.where(take, cand, thr), jnp.where(take, c, cnt)

    init = (jnp.int32(0), jnp.full(n_valid.shape, INT_MIN, jnp.int32), n_valid)
    return lax.while_loop(cond, body, init)[1]


def _idx_kernel(iqs_ref, sgn_ref, ik_ref, bias_ref, key_ref, *, tq, tkc, k_top):
    i = pl.program_id(0)
    s_len = key_ref.shape[0]
    n_chunks = ((i + 1) * tq + tkc - 1) // tkc
    qpos = i * tq + lax.broadcasted_iota(jnp.int32, (tkc, tq), 1)

    def score_chunk(c, carry):
        off = pl.multiple_of(c * tkc, tkc)
        kc = ik_ref[pl.ds(off, tkc), :]
        acc = jnp.zeros((tkc, tq), F32)
        for h in range(H_IDX):
            d = lax.dot_general(kc, iqs_ref[h], _NT, preferred_element_type=F32)
            acc = acc + sgn_ref[h:h + 1, :] * jnp.maximum(d, 0.0)
        kpos = off + lax.broadcasted_iota(jnp.int32, (tkc, tq), 0)
        key_ref[pl.ds(off, tkc), :] = jnp.where(kpos <= qpos, _sort_key(acc), INT_MIN)
        return carry

    lax.fori_loop(0, n_chunks, score_chunk, 0)

    def count_ge(cand):
        def body(c, cnt):
            off = pl.multiple_of(c * tq, tq)
            hit = jnp.where(key_ref[pl.ds(off, tq), :] >= cand, 1, 0)
            return cnt + jnp.sum(hit.reshape(tq // 8, 8, tq), axis=0)

        cnt = lax.fori_loop(0, i + 1, body, jnp.zeros((8, tq), jnp.int32))
        return jnp.sum(cnt, axis=0, keepdims=True)

    n_valid = i * tq + lax.broadcasted_iota(jnp.int32, (1, tq), 1) + 1
    thr = _bisect_threshold(count_ge, n_valid, k_top)
    thr = jnp.maximum(thr, INT_MIN + 1)

    def write_chunk(c, carry):
        off = pl.multiple_of(c * tkc, tkc)
        bias_ref[pl.ds(off, tkc), :] = jnp.where(key_ref[pl.ds(off, tkc), :] >= thr, 0.0, NEG).astype(BF16)
        return carry

    lax.fori_loop(0, n_chunks, write_chunk, 0)

    def fill_chunk(c, carry):
        off = pl.multiple_of(c * tkc, tkc)
        bias_ref[pl.ds(off, tkc), :] = jnp.full((tkc, tq), NEG, BF16)
        return carry

    lax.fori_loop(n_chunks, s_len // tkc, fill_chunk, 0)


def _idx_prompt(iqs, sgn_t, ikb, k_top, tq=256, tkc=256):
    s_len = ikb.shape[0]
    return pl.pallas_call(
        functools.partial(_idx_kernel, tq=tq, tkc=tkc, k_top=k_top),
        grid=(s_len // tq,),
        in_specs=[pl.BlockSpec((H_IDX, tq, IDX_K), lambda i: (0, i, 0)),
                  pl.BlockSpec((H_IDX, tq), lambda i: (0, i)),
                  pl.BlockSpec((s_len, IDX_K), lambda i: (0, 0))],
        out_specs=pl.BlockSpec((s_len, tq), lambda i: (0, i)),
        out_shape=jax.ShapeDtypeStruct((s_len, s_len), BF16),
        scratch_shapes=[pltpu.VMEM((s_len, tq), jnp.int32)],
        compiler_params=_cparams(("parallel",), 56),
        name="idx_prompt",
    )(iqs, sgn_t, ikb)


def _alibi_slope(h):
    return 2.0 ** (-8.0 * (h + 1) / H_DSA)


def _bf16_terms(x):
    out = []
    for _ in range(3):
        t = float(np.float32(x).astype(BF16))
        out.append(t)
        x = x - t
    return out


def _dsa_kernel(qi_ref, ki_ref, q_ref, ka_ref, vt_ref, bias_ref, o_ref, qa_ref, m_ref, acc_ref, *, tq, tk):
    qi = qi_ref[pl.program_id(0)]
    ki = ki_ref[pl.program_id(0)]
    ki_last = (qi * tq + tq - 1) // tk

    @pl.when(ki == 0)
    def _():
        for h in range(H_DSA):
            g, j = divmod(h, DSA_GROUP)
            rows = slice(j * tq, (j + 1) * tq)
            qa_ref[g, rows, :HEAD_DIM] = (
                q_ref[:, h * HEAD_DIM:(h + 1) * HEAD_DIM] * (HEAD_DIM ** -0.5 * LOG2E)).astype(BF16)
            sl = _bf16_terms(_alibi_slope(h) * LOG2E)
            qa_ref[g, rows, HEAD_DIM:] = _lane_row((tq, HEAD_DIM), [64.0 * t for t in sl] + sl).astype(BF16)
        m_ref[...] = jnp.full_like(m_ref, NEG)
        acc_ref[...] = jnp.zeros_like(acc_ref)

    mask = jnp.concatenate([bias_ref[...].astype(F32)] * DSA_GROUP, axis=1)
    groups = range(H_DSA_KV)
    logits = [lax.dot_general(ka_ref[g], qa_ref[g], _NT, preferred_element_type=F32) for g in groups]
    probs = [_flash_probs(logits[g] + mask, m_ref.at[g]) for g in groups]
    for g in groups:
        _flash_accumulate(*probs[g], vt_ref[g], acc_ref.at[g])

    @pl.when(ki == ki_last)
    def _():
        for h in range(H_DSA):
            g, j = divmod(h, DSA_GROUP)
            o_ref[:, h * HEAD_DIM:(h + 1) * HEAD_DIM] = _flash_finish(acc_ref[g, :, j * tq:(j + 1) * tq])


def _dsa_prompt(z, dka, dvt, bias_t, tq=256, tk=512):
    s_len = z.shape[0]
    q_steps, k_steps = _causal_steps(s_len // tq, lambda qi: (qi * tq + tq - 1) // tk)
    return pl.pallas_call(
        functools.partial(_dsa_kernel, tq=tq, tk=tk),
        grid_spec=pltpu.PrefetchScalarGridSpec(
            num_scalar_prefetch=2,
            grid=(q_steps.shape[0],),
            in_specs=[pl.BlockSpec((tq, DSA_W), lambda s, qs, ks: (qs[s], Z_DQ // DSA_W)),
                      pl.BlockSpec((H_DSA_KV, tk, 2 * HEAD_DIM), lambda s, qs, ks: (0, ks[s], 0)),
                      pl.BlockSpec((H_DSA_KV, VT_ROWS, tk), lambda s, qs, ks: (0, 0, ks[s])),
                      pl.BlockSpec((tk, tq), lambda s, qs, ks: (ks[s], qs[s]))],
            out_specs=pl.BlockSpec((tq, DSA_W), lambda s, qs, ks: (qs[s], 0)),
            scratch_shapes=[pltpu.VMEM((H_DSA_KV, DSA_GROUP * tq, 2 * HEAD_DIM), BF16),
                            pltpu.VMEM((H_DSA_KV, 1, DSA_GROUP * tq), F32),
                            pltpu.VMEM((H_DSA_KV, VT_ROWS, DSA_GROUP * tq), F32)]),
        out_shape=jax.ShapeDtypeStruct((s_len, DSA_W), F32),
        compiler_params=_cparams(("arbitrary",)),
        name="dsa_prompt",
    )(q_steps, k_steps, z, dka, dvt, bias_t)


def _dec_idx_kernel(pt_ref, *refs, n_pages, t_new):
    del pt_ref
    ik_refs = refs[:n_pages]
    lf_refs = refs[n_pages:2 * n_pages]
    iq_ref, w_ref, ikn_ref, lfn_ref, key_ref, fneg_ref = refs[2 * n_pages:]

    row = lax.broadcasted_iota(jnp.int32, (PAGE_SIZE, PAGE_SIZE), 0)
    col = lax.broadcasted_iota(jnp.int32, (PAGE_SIZE, PAGE_SIZE), 1)
    tri = (row <= col).astype(F32)
    carry = jnp.zeros((H_FOX, 1), F32)
    for p in range(n_pages + 1):
        lf = lf_refs[p][...] if p < n_pages else lfn_ref[...]
        cum = jnp.dot(lf, tri, preferred_element_type=F32, precision=lax.Precision.HIGHEST) + carry
        fneg_ref[:, p * PAGE_SIZE:(p + 1) * PAGE_SIZE] = -cum
        carry = cum[:, PAGE_SIZE - 1:PAGE_SIZE]

    w = w_ref[...] * (H_IDX ** -0.5)
    iqs = _split_operand(iq_ref[...] * (jnp.abs(w) * (IDX_DIM ** -0.5)), 1, True)
    sgn = jnp.where(w > 0, 1.0, -1.0)
    qrow = lax.broadcasted_iota(jnp.int32, (16, PAGE_SIZE), 0) % t_new
    lane = lax.broadcasted_iota(jnp.int32, (16, PAGE_SIZE), 1)
    for p in range(n_pages + 1):
        kp_t = _split_operand((ik_refs[p] if p < n_pages else ikn_ref)[...], 0, False)
        d = jnp.dot(iqs, kp_t, preferred_element_type=F32)
        sc = jnp.sum((sgn * jnp.maximum(d, 0.0)).reshape(H_IDX, 16, PAGE_SIZE), axis=0)
        key = _sort_key(sc)
        if p == n_pages:
            key = jnp.where((lane <= qrow) & (lane < t_new), key, INT_MIN)
        key_ref[:, p * PAGE_SIZE:(p + 1) * PAGE_SIZE] = key


def _dec_idx(page_table, idx_pages, logf_pages_t, iq16, w16, ik_new_t, logf_new_t, t_new):
    b, n_pages = page_table.shape
    width = (n_pages + 1) * PAGE_SIZE
    page = lambda p: (lambda bi, pt: (pt[bi, p], 0, 0))
    per_b = lambda bi, pt: (bi, 0, 0)
    in_specs = ([pl.BlockSpec((None, IDX_DIM, PAGE_SIZE), page(p)) for p in range(n_pages)]
                + [pl.BlockSpec((None, H_FOX, PAGE_SIZE), page(p)) for p in range(n_pages)]
                + [pl.BlockSpec((None, H_IDX * 16, IDX_DIM), per_b), pl.BlockSpec((None, H_IDX * 16, 1), per_b),
                   pl.BlockSpec((None, IDX_DIM, PAGE_SIZE), per_b), pl.BlockSpec((None, H_FOX, PAGE_SIZE), per_b)])
    return pl.pallas_call(
        functools.partial(_dec_idx_kernel, n_pages=n_pages, t_new=t_new),
        grid_spec=pltpu.PrefetchScalarGridSpec(
            num_scalar_prefetch=1,
            grid=(b,),
            in_specs=in_specs,
            out_specs=[pl.BlockSpec((None, 16, width), per_b), pl.BlockSpec((None, H_FOX, width), per_b)]),
        out_shape=[jax.ShapeDtypeStruct((b, 16, width), jnp.int32), jax.ShapeDtypeStruct((b, H_FOX, width), F32)],
        compiler_params=_cparams(("parallel",)),
        name="dec_idx",
    )(page_table, *([idx_pages] * n_pages), *([logf_pages_t] * n_pages), iq16, w16, ik_new_t, logf_new_t)


def _dec_select_kernel(key_ref, nv_ref, sel_ref, *, k_top):
    def count_ge(cand):
        return jnp.sum(jnp.where(key_ref[...] >= cand, 1, 0), axis=-1, keepdims=True)

    thr = jnp.maximum(_bisect_threshold(count_ge, nv_ref[...], k_top), INT_MIN + 1)
    sel_ref[...] = jnp.where(key_ref[...] >= thr, 0.0, NEG)


def _dec_select(keys, n_valid, k_top, tm=256):
    rows, width = keys.shape
    tm = min(tm, rows)
    return pl.pallas_call(
        functools.partial(_dec_select_kernel, k_top=k_top),
        grid=(rows // tm,),
        in_specs=[pl.BlockSpec((tm, width), lambda i: (i, 0)), pl.BlockSpec((tm, 1), lambda i: (i, 0))],
        out_specs=pl.BlockSpec((tm, width), lambda i: (i, 0)),
        out_shape=jax.ShapeDtypeStruct((rows, width), F32),
        compiler_params=_cparams(("parallel",)),
        name="dec_select",
    )(keys, n_valid)


def _dec_attn_kernel(pt_ref, *refs, n_pages, kp, t_new):
    del pt_ref
    fk_refs, fv_refs, dk_refs, dv_refs = (refs[i * kp:(i + 1) * kp] for i in range(4))
    (fkn_ref, fvn_ref, dkn_ref, dvn_ref, qf_ref, qd_ref, fneg_ref, dbias_ref, fneg_new_ref, dbias_new_ref, slope_ref,
     of_ref, od_ref, mf_ref, lf_ref, accf_ref, md_ref, ld_ref, accd_ref, pairf_ref, paird_ref) = refs[4 * kp:]
    p_id = pl.program_id(1)
    n_steps = n_pages // kp
    rows_q = t_new * H_FOX
    wf = PAGE_SIZE * H_FOX
    wd = PAGE_SIZE * H_DSA_KV

    def pair_mask(n_keys, heads_per_key_head, n_kv):
        qh = lax.broadcasted_iota(jnp.int32, (rows_q, n_keys), 0) % H_FOX
        kh = lax.broadcasted_iota(jnp.int32, (rows_q, n_keys), 1) % n_kv
        return jnp.where(qh // heads_per_key_head == kh, 0.0, NEG)

    @pl.when(p_id == 0)
    def _():
        mf_ref[...] = jnp.full_like(mf_ref, NEG)
        lf_ref[...] = jnp.zeros_like(lf_ref)
        accf_ref[...] = jnp.zeros_like(accf_ref)
        md_ref[...] = jnp.full_like(md_ref, NEG)
        ld_ref[...] = jnp.zeros_like(ld_ref)
        accd_ref[...] = jnp.zeros_like(accd_ref)
        pairf_ref[...] = pair_mask(wf, 1, H_FOX)
        paird_ref[...] = pair_mask(wd, DSA_GROUP, H_DSA_KV)

    def attend(q_ref, k, v, bias, m_ref, l_ref, acc_ref):
        q = (q_ref[...] * (HEAD_DIM ** -0.5)).astype(BF16)
        s = lax.dot_general(q, k, _NT, preferred_element_type=F32) + bias
        m_prev = m_ref[...]
        m_new = jnp.maximum(m_prev, jnp.max(s, axis=-1, keepdims=True))
        alpha = jnp.exp(m_prev - m_new)
        p = jnp.exp(s - m_new)
        l_ref[...] = alpha * l_ref[...] + jnp.sum(p, axis=-1, keepdims=True)
        acc_ref[...] = alpha * acc_ref[...] + jnp.dot(p.astype(BF16), v, preferred_element_type=F32)
        m_ref[...] = m_new

    def sel_rows(db):
        return jnp.concatenate([jnp.broadcast_to(db[q:q + 1], (H_DSA, db.shape[1])) for q in range(t_new)], axis=0)

    def alibi(n_keys, tok0):
        tok = tok0 + lax.broadcasted_iota(jnp.int32, (1, n_keys), 1) // H_DSA_KV
        return slope_ref[...] * tok.astype(F32)

    cat = lambda page_refs: jnp.concatenate([r[...].astype(BF16) for r in page_refs], axis=0)
    bias = jnp.concatenate([pairf_ref[...]] * kp, axis=1) + fneg_ref[...]
    attend(qf_ref, cat(fk_refs), cat(fv_refs), bias, mf_ref, lf_ref, accf_ref)
    bias = (jnp.concatenate([paird_ref[...]] * kp, axis=1) + sel_rows(dbias_ref[...])
            + alibi(kp * wd, (p_id * kp - n_pages) * PAGE_SIZE))
    attend(qd_ref, cat(dk_refs), cat(dv_refs), bias, md_ref, ld_ref, accd_ref)

    @pl.when(p_id == n_steps - 1)
    def _():
        def causal(n_keys, n_kv):
            q = lax.broadcasted_iota(jnp.int32, (rows_q, n_keys), 0) // H_FOX
            tok = lax.broadcasted_iota(jnp.int32, (rows_q, n_keys), 1) // n_kv
            return jnp.where(tok <= q, 0.0, NEG)

        nf = t_new * H_FOX
        bias = pairf_ref[:, :nf] + fneg_new_ref[:, :nf] + causal(nf, H_FOX)
        attend(qf_ref, fkn_ref[...].astype(BF16), fvn_ref[...].astype(BF16), bias, mf_ref, lf_ref, accf_ref)
        nd = t_new * H_DSA_KV
        bias = paird_ref[:, :nd] + sel_rows(dbias_new_ref[:, :nd]) + alibi(nd, 0)
        attend(qd_ref, dkn_ref[...].astype(BF16), dvn_ref[...].astype(BF16), bias, md_ref, ld_ref, accd_ref)
        of_ref[...] = accf_ref[...] / lf_ref[...]
        od_ref[...] = accd_ref[...] / ld_ref[...]


def _dec_attn(page_table, fox_k, fox_v, dsa_k, dsa_v, fk_new, fv_new, dk_new, dv_new, qf, qd, fneg_rows, dbias_rows,
              slopes, t_new):
    b, n_pages = page_table.shape
    kp = 4 if n_pages % 4 == 0 else 1
    n_steps = n_pages // kp
    wf = PAGE_SIZE * H_FOX
    wd = PAGE_SIZE * H_DSA_KV
    rows_q = t_new * H_FOX

    def page(j):
        return lambda bi, p, pt: (pt[bi, p * kp + j], 0, 0)

    per_b = lambda bi, p, pt: (bi, 0, 0)
    past = lambda bi, p, pt: (bi, 0, p)
    new = lambda bi, p, pt: (bi, 0, n_pages)
    in_specs = ([pl.BlockSpec((None, wf, HEAD_DIM), page(j)) for j in range(kp)] * 2
                + [pl.BlockSpec((None, wd, HEAD_DIM), page(j)) for j in range(kp)] * 2
                + [pl.BlockSpec((None, t_new * H_FOX, HEAD_DIM), per_b)] * 2
                + [pl.BlockSpec((None, t_new * H_DSA_KV, HEAD_DIM), per_b)] * 2
                + [pl.BlockSpec((None, rows_q, HEAD_DIM), per_b)] * 2
                + [pl.BlockSpec((None, 1, kp * wf), past), pl.BlockSpec((None, 8, kp * wd), past),
                   pl.BlockSpec((None, 1, wf), new), pl.BlockSpec((None, 8, wd), new),
                   pl.BlockSpec((rows_q, 1), lambda bi, p, pt: (0, 0))])
    out = jax.ShapeDtypeStruct((b, rows_q, HEAD_DIM), F32)
    col = pltpu.VMEM((rows_q, 1), F32)
    acc = pltpu.VMEM((rows_q, HEAD_DIM), F32)
    return pl.pallas_call(
        functools.partial(_dec_attn_kernel, n_pages=n_pages, kp=kp, t_new=t_new),
        grid_spec=pltpu.PrefetchScalarGridSpec(
            num_scalar_prefetch=1,
            grid=(b, n_steps),
            in_specs=in_specs,
            out_specs=[pl.BlockSpec((None, rows_q, HEAD_DIM), per_b)] * 2,
            scratch_shapes=[col, col, acc, col, col, acc, pltpu.VMEM((rows_q, wf), F32),
                            pltpu.VMEM((rows_q, wd), F32)]),
        out_shape=[out, out],
        compiler_params=_cparams(("parallel", "arbitrary")),
        name="dec_attn",
    )(page_table, *([fox_k] * kp), *([fox_v] * kp), *([dsa_k] * kp), *([dsa_v] * kp),
      fk_new, fv_new, dk_new, dv_new, qf, qd, fneg_rows, dbias_rows, fneg_rows, dbias_rows, slopes)


def _ffn_up_kernel(h_ref, hprev_ref, wg_ref, wu_ref, wc_ref, bc_ref, a_ref, tail_ref, *, tm):
    i = pl.program_id(0)
    h = h_ref[...]
    wg = wg_ref[...].astype(BF16)
    g = jnp.dot(h, wg, preferred_element_type=F32)
    u = jnp.dot(h, wu_ref[...].astype(BF16), preferred_element_type=F32)
    gh = jnp.dot(hprev_ref[...], wg, preferred_element_type=F32)
    gh = jnp.where(i > 0, gh, 0.0)
    ext = jnp.concatenate([gh, g], axis=0)
    g1 = pltpu.roll(ext, 1, 0)[16:]
    g2 = pltpu.roll(ext, 2, 0)[16:]
    wc = wc_ref[...]
    conv = bc_ref[...] + wc[0:1] * g2 + wc[1:2] * g1 + wc[2:3] * g
    a_ref[...] = (conv * _sigmoid(conv) * u).astype(a_ref.dtype)
    tail_ref[...] = g[tm - 8:]


def _ffn_up_prompt(h2, w_gate, w_up, w_conv, b_conv, tm=1024, tn=256):
    m, d = h2.shape
    d_ff = w_gate.shape[1]
    tm = min(tm, m)
    return pl.pallas_call(
        functools.partial(_ffn_up_kernel, tm=tm),
        grid=(m // tm, d_ff // tn),
        in_specs=[pl.BlockSpec((tm, d), lambda i, j: (i, 0)),
                  pl.BlockSpec((16, d), lambda i, j: (jnp.maximum(i * (tm // 16) - 1, 0), 0)),
                  pl.BlockSpec((d, tn), lambda i, j: (0, j)), pl.BlockSpec((d, tn), lambda i, j: (0, j)),
                  pl.BlockSpec((CONV_W, tn), lambda i, j: (0, j)), pl.BlockSpec((1, tn), lambda i, j: (0, j))],
        out_specs=[pl.BlockSpec((tm, tn), lambda i, j: (i, j)), pl.BlockSpec((8, tn), lambda i, j: (i, j))],
        out_shape=[jax.ShapeDtypeStruct((m, d_ff), BF16), jax.ShapeDtypeStruct((m // tm * 8, d_ff), F32)],
        compiler_params=_cparams(("parallel", "parallel")),
        name="ffn_up_prompt",
    )(h2, h2, w_gate, w_up, w_conv, b_conv.reshape(1, d_ff))


def _ffn_up_dec_kernel(h_ref, wg_ref, wu_ref, wc_ref, bc_ref, s0_ref, s1_ref, a_ref, g_ref, *, t_new):
    h = h_ref[...]
    g = jnp.dot(h, wg_ref[...].astype(BF16), preferred_element_type=F32)
    u = jnp.dot(h, wu_ref[...].astype(BF16), preferred_element_type=F32)
    g_ref[...] = g
    t = lax.broadcasted_iota(jnp.int32, g.shape, 0) % t_new
    g1 = jnp.where(t >= 1, pltpu.roll(g, 1, 0), 0.0) + s1_ref[...]
    g2 = jnp.where(t >= 2, pltpu.roll(g, 2, 0), 0.0) + s0_ref[...]
    wc = wc_ref[...]
    conv = bc_ref[...] + wc[0:1] * g2 + wc[1:2] * g1 + wc[2:3] * g
    a_ref[...] = (conv * _sigmoid(conv) * u).astype(a_ref.dtype)


def _ffn_up_dec(h2, w_gate, w_up, w_conv, b_conv, tap0, tap1, t_new, tn=256):
    m, d = h2.shape
    d_ff = w_gate.shape[1]
    col = lambda j: (0, j)
    return pl.pallas_call(
        functools.partial(_ffn_up_dec_kernel, t_new=t_new),
        grid=(d_ff // tn,),
        in_specs=[pl.BlockSpec((m, d), lambda j: (0, 0)), pl.BlockSpec((d, tn), col), pl.BlockSpec((d, tn), col),
                  pl.BlockSpec((CONV_W, tn), col), pl.BlockSpec((1, tn), col),
                  pl.BlockSpec((m, tn), col), pl.BlockSpec((m, tn), col)],
        out_specs=[pl.BlockSpec((m, tn), col), pl.BlockSpec((m, tn), col)],
        out_shape=[jax.ShapeDtypeStruct((m, d_ff), BF16), jax.ShapeDtypeStruct((m, d_ff), F32)],
        compiler_params=_cparams(("parallel",)),
        name="ffn_up_dec",
    )(h2, w_gate, w_up, w_conv, b_conv.reshape(1, d_ff), tap0, tap1)


def _split_mod(mod):
    return jnp.split(mod, 6, axis=-1)


def _attn_out_and_ffn_in(x, o_f, o_d, mods, p):
    _, _, gt1, sh2, sc2, _ = mods
    a = _attn_norm(o_f, o_d, p["g_fox_out"], p["g_dsa_out"])
    o = _matmul(a, p["w_out"], 1024, 512, "out_proj")
    return _res_pre(o, x, gt1, p["g_post_attn"], p["g_pre_ffn"], sc2, sh2)


def _ffn_down_and_res(a, x1, mods, p):
    d_ff = a.shape[1]
    f = _matmul_ksplit(a, p["w_down"], 1024, 512, d_ff // 2, "ffn_down")
    return _res(f, x1, mods[5], p["g_post_ffn"])


def _project(x, mods, p):
    sh1, sc1 = mods[0], mods[1]
    h = _prenorm(x, p["g_pre_attn"], sc1, sh1)
    return (_matmul_nt(h, p["w_in_t"], 1024, ZF_TN, "in_proj_fox", n=ZF_W),
            _matmul_nt(h, p["w_in_tail_t"], 1024, ZD_TN, "in_proj_rest"))


def _kv_states(zf, zd, logf, conv):
    return (zf[:, Z_FK:Z_FK + FOX_W], zf[:, Z_FV:Z_FV + FOX_W], logf, zd[:, Z_DK:Z_DK + DSA_KV_W],
            zd[:, Z_DV:Z_DV + DSA_KV_W], zd[:, Z_SM + SM_IK:Z_SM + SM_IK + IDX_DIM], conv)


def _prompt_layer(x, mods, p):
    s_len = x.shape[0]
    zf, zd = _project(x, mods, p)
    logf, cum = _logf_cumsum(zd, p["b_f"])
    fka, fvt, dka, dvt, iqs, ikb, sgn_t = _attn_prep(zf, zd, cum)
    o_f = _fox_prompt(zf, fka, fvt)
    bias_t = _idx_prompt(iqs, sgn_t, ikb, min(TOPK_MAX, s_len // 4))
    o_d = _dsa_prompt(zd, dka, dvt, bias_t)
    x1, h2 = _attn_out_and_ffn_in(x, o_f, o_d, mods, p)
    a, tails = _ffn_up_prompt(h2, p["w_gate"], p["w_up"], p["w_conv"], p["b_conv"])
    y = _ffn_down_and_res(a, x1, mods, p)
    return y, _kv_states(zf, zd, logf, tails[-(CONV_W - 1):])


def _pad_rows(x, rows):
    return jnp.pad(x, ((0, 0), (0, rows - x.shape[1]), (0, 0)))


def _decode_layer(x, mods, p, page_table, caches, state_conv, t_new):
    m = x.shape[0]
    b = m // t_new
    n_pages = page_table.shape[1]
    cache_fox_k, cache_fox_v, cache_fox_logf_t, cache_dsa_k, cache_dsa_v, cache_idx_k = caches
    zf, zd = _project(x, mods, p)
    logf, _ = _logf_cumsum(zd, p["b_f"])
    zf3 = zf.reshape(b, t_new, ZF_W)
    zd3 = zd.reshape(b, t_new, ZD_W)

    rep = 16 // t_new
    iq16 = jnp.tile(zd3[:, :, Z_IQ:Z_IQ + IQ_W].reshape(b, t_new, H_IDX, IDX_DIM).transpose(0, 2, 1, 3),
                    (1, 1, rep, 1)).reshape(b, H_IDX * 16, IDX_DIM)
    w16 = jnp.tile(zd3[:, :, Z_SM + SM_IW:Z_SM + SM_IW + H_IDX].transpose(0, 2, 1), (1, 1, rep)).reshape(
        b, H_IDX * 16, 1)
    ik_new_t = jnp.pad(zd3[:, :, Z_SM + SM_IK:Z_SM + SM_IK + IDX_DIM].transpose(0, 2, 1),
                       ((0, 0), (0, 0), (0, PAGE_SIZE - t_new)))
    logf_new_t = jnp.pad(logf.reshape(b, t_new, H_FOX).transpose(0, 2, 1), ((0, 0), (0, 0), (0, PAGE_SIZE - t_new)))
    k_top = min(TOPK_MAX, (n_pages * PAGE_SIZE + t_new) // 4)
    keys, fneg = _dec_idx(page_table, cache_idx_k, cache_fox_logf_t, iq16, w16, ik_new_t, logf_new_t, t_new)
    n_valid = jnp.tile(n_pages * PAGE_SIZE + 1 + jnp.arange(8, dtype=jnp.int32) % t_new, b).reshape(b * 8, 1)
    dbias = _dec_select(keys[:, :8].reshape(b * 8, -1), n_valid, k_top).reshape(b, 8, -1)

    heads = lambda z3, off, w: z3[:, :, off:off + w].reshape(b, t_new * (w // HEAD_DIM), HEAD_DIM)
    fneg_rows = fneg.transpose(0, 2, 1).reshape(b, 1, -1)
    dbias_rows = jnp.repeat(dbias, H_DSA_KV, axis=-1)
    slopes = jnp.tile(2.0 ** (-8.0 * jnp.arange(1, H_DSA + 1, dtype=F32) / H_DSA), t_new).reshape(H_DSA * t_new, 1)
    o_f, o_d = _dec_attn(
        page_table, cache_fox_k, cache_fox_v, cache_dsa_k, cache_dsa_v,
        heads(zf3, Z_FK, FOX_W), heads(zf3, Z_FV, FOX_W), heads(zd3, Z_DK, DSA_KV_W), heads(zd3, Z_DV, DSA_KV_W),
        heads(zf3, Z_FQ, FOX_W), heads(zd3, Z_DQ, DSA_W), fneg_rows, dbias_rows, slopes, t_new)

    x1, h2 = _attn_out_and_ffn_in(x, o_f.reshape(m, FOX_W), o_d.reshape(m, DSA_W), mods, p)
    d_ff = state_conv.shape[-1]
    zero = jnp.zeros((b, 1, d_ff), F32)
    s0, s1 = state_conv[:, 0:1], state_conv[:, 1:2]
    tap0 = jnp.concatenate([s0, s1] + [zero] * (t_new - 2), axis=1).reshape(m, d_ff)
    tap1 = jnp.concatenate([s1] + [zero] * (t_new - 1), axis=1).reshape(m, d_ff)
    a, g = _ffn_up_dec(h2, p["w_gate"], p["w_up"], p["w_conv"], p["b_conv"], tap0, tap1, t_new)
    y = _ffn_down_and_res(a, x1, mods, p)
    conv_new = g.reshape(b, t_new, d_ff)[:, t_new - (CONV_W - 1):]
    return y, _kv_states(zf, zd, logf, conv_new)


def _regroup_w_in_tail(w_in_t):
    n, d = w_in_t.shape
    rest = w_in_t[ZF_W + H_FOX:].astype(BF16)
    fg = w_in_t[ZF_W:ZF_W + H_FOX].astype(BF16)
    assert n - ZF_W - H_FOX == Z_SM + SM_IW + H_IDX and SM_FG == SM_IW + H_IDX
    return jnp.concatenate([rest, fg, jnp.zeros((ZD_W - (n - ZF_W), d), BF16)], axis=0)


def kernel(x_prompt, x_sample, c_prompt, c_sample, page_table, cache_fox_k, cache_fox_v, cache_fox_logf, cache_dsa_k, cache_dsa_v, cache_idx_k, state_conv, w_in, b_f, w_out, g_fox_out, g_dsa_out, w_mod, b_mod, g_pre_attn, g_post_attn, g_pre_ffn, g_post_ffn, w_gate, w_up, w_conv, b_conv, w_down):
    depth = w_in.shape[0]
    bp, s_len, d = x_prompt.shape
    bs, t_new, _ = x_sample.shape
    assert bp == 1 and t_new >= CONV_W - 1 and 16 % t_new == 0

    xp = x_prompt.reshape(s_len, d)
    xs = x_sample.reshape(bs * t_new, d)
    n_c = bp + bs
    c_all = jnp.pad(jnp.concatenate([c_prompt, c_sample], axis=0), ((0, -n_c % 8), (0, 0)))
    n_phys = cache_fox_k.shape[1]
    fold = lambda c: c.reshape((depth * n_phys,) + c.shape[2:])
    rows = lambda c: c.reshape(depth * n_phys, PAGE_SIZE * c.shape[3], HEAD_DIM)
    caches = (rows(cache_fox_k), rows(cache_fox_v), fold(cache_fox_logf).transpose(0, 2, 1), rows(cache_dsa_k),
              rows(cache_dsa_v), fold(cache_idx_k).transpose(0, 2, 1))
    p_states, s_states = [], []
    for l in range(depth):
        w_in_t = w_in[l].T
        p = dict(w_in_t=w_in_t, w_in_tail_t=_regroup_w_in_tail(w_in_t), b_f=b_f[l], w_out=w_out[l].astype(BF16),
                 g_fox_out=g_fox_out[l][None], g_dsa_out=g_dsa_out[l][None],
                 g_pre_attn=g_pre_attn[l][None], g_post_attn=g_post_attn[l][None],
                 g_pre_ffn=g_pre_ffn[l][None], g_post_ffn=g_post_ffn[l][None],
                 w_gate=w_gate[l], w_up=w_up[l], w_conv=w_conv[l], b_conv=b_conv[l],
                 w_down=w_down[l].astype(BF16))
        mod = _modulation(c_all, w_mod[l], b_mod[l])
        mods_p = _split_mod(mod[:bp])
        mods_s = [jnp.repeat(v, t_new, axis=0) for v in _split_mod(mod[bp:n_c])]
        xp, st_p = _prompt_layer(xp, mods_p, p)
        xs, st_s = _decode_layer(xs, mods_s, p, page_table + l * n_phys, caches, state_conv[l], t_new)
        p_states.append(st_p)
        s_states.append(st_s)

    def stack(states, i, shape):
        return jnp.stack([st[i].reshape(shape) for st in states])

    d_ff = state_conv.shape[-1]
    outs = [xp.reshape(bp, s_len, d), xs.reshape(bs, t_new, d)]
    for states, (bb, tt) in ((p_states, (bp, s_len)), (s_states, (bs, t_new))):
        outs += [stack(states, 0, (bb, tt, H_FOX, HEAD_DIM)), stack(states, 1, (bb, tt, H_FOX, HEAD_DIM)),
                 stack(states, 2, (bb, tt, H_FOX)), stack(states, 3, (bb, tt, H_DSA_KV, HEAD_DIM)),
                 stack(states, 4, (bb, tt, H_DSA_KV, HEAD_DIM)), stack(states, 5, (bb, tt, IDX_DIM)),
                 stack(states, 6, (bb, CONV_W - 1, d_ff))]
    return tuple(outs)
```

```python
import functools

import jax
import jax.numpy as jnp
import numpy as np
from jax import lax
from jax.experimental import pallas as pl
from jax.experimental.pallas import tpu as pltpu

HEAD_DIM = 128
H_FOX = 16
H_DSA = 16
H_DSA_KV = 4
DSA_GROUP = H_DSA // H_DSA_KV
H_IDX = 16
IDX_DIM = 64
TOPK_MAX = 256
CONV_W = 3
PAGE_SIZE = 128
RMS_EPS = 1e-6
FOX_W = H_FOX * HEAD_DIM
DSA_W = H_DSA * HEAD_DIM
DSA_KV_W = H_DSA_KV * HEAD_DIM
IQ_W = H_IDX * IDX_DIM

Z_FQ = 0
Z_FK = Z_FQ + FOX_W
Z_FV = Z_FK + FOX_W
ZF_W = Z_FV + FOX_W
ZF_TN = 512
Z_DQ = 0
Z_DK = Z_DQ + DSA_W
Z_DV = Z_DK + DSA_KV_W
Z_IQ = Z_DV + DSA_KV_W
Z_SM = Z_IQ + IQ_W
SM_IK = 0
SM_IW = SM_IK + IDX_DIM
SM_FG = SM_IW + H_IDX
ZD_TN = 768
ZD_W = -(-(Z_SM + 128) // ZD_TN) * ZD_TN
ZI_SM = IQ_W
ZI_TN = 384
ZI_W = -(-(ZI_SM + SM_FG) // ZI_TN) * ZI_TN

NEG = -1e30
LOG2E = 1.4426950408889634
VT_ROWS = HEAD_DIM + 16
INT_MIN = -2 ** 31
MIB = 1024 * 1024
BF16 = jnp.bfloat16
F32 = jnp.float32

_NT = (((1,), (1,)), ((), ()))


def _cparams(sem, vmem_mib=48):
    return pltpu.CompilerParams(dimension_semantics=sem, vmem_limit_bytes=vmem_mib * MIB)


def _rms(x, g):
    return x * lax.rsqrt(jnp.mean(x * x, axis=-1, keepdims=True) + RMS_EPS) * g


def _sigmoid(x):
    return 1.0 / (1.0 + jnp.exp(-x))


def _sort_key(x):
    b = pltpu.bitcast(x, jnp.int32)
    return b ^ ((b >> 31) & jnp.int32(0x7FFFFFFF))


def _row_spec(arr, tm):
    d = arr.shape[1]
    if arr.shape[0] == 1:
        return pl.BlockSpec((1, d), lambda i: (0, 0))
    return pl.BlockSpec((tm, d), lambda i: (i, 0))


def _mod_kernel(c_ref, w_ref, b_ref, o_ref):
    c = c_ref[...]
    a = (c * _sigmoid(c)).astype(BF16)
    o_ref[...] = jnp.dot(a, w_ref[...].astype(BF16), preferred_element_type=F32) + b_ref[...]


def _modulation(c, w_mod, b_mod):
    r, d = c.shape
    n = w_mod.shape[1]
    tn = 512
    return pl.pallas_call(
        _mod_kernel,
        grid=(n // tn,),
        in_specs=[pl.BlockSpec((r, d), lambda j: (0, 0)),
                  pl.BlockSpec((d, tn), lambda j: (0, j)),
                  pl.BlockSpec((1, tn), lambda j: (0, j))],
        out_specs=pl.BlockSpec((r, tn), lambda j: (0, j)),
        out_shape=jax.ShapeDtypeStruct((r, n), F32),
        compiler_params=_cparams(("parallel",)),
        name="modulation",
    )(c, w_mod, b_mod.reshape(1, n))


def _prenorm_kernel(x_ref, g_ref, sc_ref, sh_ref, o_ref, lo_ref):
    y = _rms(x_ref[...], g_ref[...])
    h = y * (1.0 + sc_ref[...]) + sh_ref[...]
    hi = h.astype(BF16)
    o_ref[...] = hi
    lo_ref[...] = (h - hi.astype(F32)).astype(BF16)


def _prenorm(x, g, sc, sh, tm=256):
    m, d = x.shape
    tm = min(tm, m)
    blk = pl.BlockSpec((tm, d), lambda i: (i, 0))
    out = jax.ShapeDtypeStruct((m, d), BF16)
    return pl.pallas_call(
        _prenorm_kernel,
        grid=(m // tm,),
        in_specs=[blk, _row_spec(g, tm), _row_spec(sc, tm), _row_spec(sh, tm)],
        out_specs=[blk, blk],
        out_shape=[out, out],
        compiler_params=_cparams(("parallel",)),
        name="prenorm",
    )(x, g, sc, sh)


def _attn_norm_kernel(of_ref, od_ref, gf_ref, gd_ref, o_ref):
    o_ref[:, :FOX_W] = _rms(of_ref[...], gf_ref[...]).astype(o_ref.dtype)
    o_ref[:, FOX_W:] = _rms(od_ref[...], gd_ref[...]).astype(o_ref.dtype)


def _attn_norm(o_f, o_d, g_f, g_d, tm=256):
    m = o_f.shape[0]
    tm = min(tm, m)
    return pl.pallas_call(
        _attn_norm_kernel,
        grid=(m // tm,),
        in_specs=[pl.BlockSpec((tm, FOX_W), lambda i: (i, 0)), pl.BlockSpec((tm, DSA_W), lambda i: (i, 0)),
                  _row_spec(g_f, tm), _row_spec(g_d, tm)],
        out_specs=pl.BlockSpec((tm, FOX_W + DSA_W), lambda i: (i, 0)),
        out_shape=jax.ShapeDtypeStruct((m, FOX_W + DSA_W), BF16),
        compiler_params=_cparams(("parallel",)),
        name="attn_norm",
    )(o_f, o_d, g_f, g_d)


def _res_pre_kernel(o_ref, x_ref, gt_ref, gpost_ref, gpre_ref, sc_ref, sh_ref, x1_ref, h2_ref):
    x1 = x_ref[...] + gt_ref[...] * _rms(o_ref[...], gpost_ref[...])
    x1_ref[...] = x1
    h2_ref[...] = (_rms(x1, gpre_ref[...]) * (1.0 + sc_ref[...]) + sh_ref[...]).astype(h2_ref.dtype)


def _res_pre(o, x, gt, g_post, g_pre, sc, sh, tm=128):
    m, d = x.shape
    tm = min(tm, m)
    blk =pl.BlockSpec((tm, d), lambda i: (i, 0))
    return pl.pallas_call(
        _res_pre_kernel,
        grid=(m // tm,),
        in_specs=[blk, blk, _row_spec(gt, tm), _row_spec(g_post, tm), _row_spec(g_pre, tm),
                  _row_spec(sc, tm), _row_spec(sh, tm)],
        out_specs=[blk, blk],
        out_shape=[jax.ShapeDtypeStruct((m, d), F32), jax.ShapeDtypeStruct((m, d), BF16)],
        compiler_params=_cparams(("parallel",)),
        name="res_pre",
    )(o, x, gt, g_post, g_pre, sc, sh)


def _res_kernel(f_ref, x_ref, gt_ref, gpost_ref, y_ref):
    y_ref[...] = x_ref[...] + gt_ref[...] * _rms(f_ref[...], gpost_ref[...])


def _res(f, x, gt, g_post, tm=256):
    m, d = x.shape
    tm = min(tm, m)
    blk =pl.BlockSpec((tm, d), lambda i: (i, 0))
    return pl.pallas_call(
        _res_kernel,
        grid=(m // tm,),
        in_specs=[blk, blk, _row_spec(gt, tm), _row_spec(g_post, tm)],
        out_specs=blk,
        out_shape=jax.ShapeDtypeStruct((m, d), F32),
        compiler_params=_cparams(("parallel",)),
        name="res",
    )(f, x, gt, g_post)


def _mm_kernel(a_ref, w_ref, o_ref):
    o_ref[...] = jnp.dot(a_ref[...], w_ref[...], preferred_element_type=F32)


def _matmul(a, w, tm, tn, name):
    m, k = a.shape
    n = w.shape[1]
    tm = min(tm, m)
    return pl.pallas_call(
        _mm_kernel,
        grid=(m // tm, n // tn),
        in_specs=[pl.BlockSpec((tm, k), lambda i, j: (i, 0)), pl.BlockSpec((k, tn), lambda i, j: (0, j))],
        out_specs=pl.BlockSpec((tm, tn), lambda i, j: (i, j)),
        out_shape=jax.ShapeDtypeStruct((m, n), F32),
        compiler_params=_cparams(("parallel", "parallel")),
        name=name,
    )(a, w)


def _mm_nt_kernel(a_ref, wt_ref, o_ref):
    o_ref[...] = lax.dot_general(a_ref[...], wt_ref[...].astype(BF16), _NT, preferred_element_type=F32)


def _matmul_nt(a, w_t, tm, tn, name, n=None):
    m, k = a.shape
    n = w_t.shape[0] if n is None else n
    tm = min(tm, m)
    return pl.pallas_call(
        _mm_nt_kernel,
        grid=(m // tm, n // tn),
        in_specs=[pl.BlockSpec((tm, k), lambda i, j: (i, 0)), pl.BlockSpec((tn, k), lambda i, j: (j, 0))],
        out_specs=pl.BlockSpec((tm, tn), lambda i, j: (i, j)),
        out_shape=jax.ShapeDtypeStruct((m, n), F32),
        compiler_params=_cparams(("parallel", "parallel")),
        name=name,
    )(a, w_t)


def _mm_nt_split_kernel(a_ref, alo_ref, wt_ref, o_ref):
    w = wt_ref[...]
    w_hi = w.astype(BF16)
    w_lo = (w - w_hi.astype(F32)).astype(BF16)
    dot = lambda x, y: lax.dot_general(x, y, _NT, preferred_element_type=F32)
    o_ref[...] = dot(a_ref[...], w_hi) + (dot(a_ref[...], w_lo) + dot(alo_ref[...], w_hi))


def _matmul_nt_split(a, a_lo, w_t, tm, tn, name):
    m, k = a.shape
    n = w_t.shape[0]
    tm = min(tm, m)
    act = pl.BlockSpec((tm, k), lambda i, j: (i, 0))
    return pl.pallas_call(
        _mm_nt_split_kernel,
        grid=(m // tm, n // tn),
        in_specs=[act, act, pl.BlockSpec((tn, k), lambda i, j: (j, 0))],
        out_specs=pl.BlockSpec((tm, tn), lambda i, j: (i, j)),
        out_shape=jax.ShapeDtypeStruct((m, n), F32),
        compiler_params=_cparams(("parallel", "parallel")),
        name=name,
    )(a, a_lo, w_t)


def _mmk_kernel(a_ref, w_ref, o_ref):
    @pl.when(pl.program_id(2) == 0)
    def _():
        o_ref[...] = jnp.zeros_like(o_ref)

    o_ref[...] += jnp.dot(a_ref[...], w_ref[...], preferred_element_type=F32)


def _matmul_ksplit(a, w, tm, tn, tk, name):
    m, k = a.shape
    n = w.shape[1]
    tm = min(tm, m)
    return pl.pallas_call(
        _mmk_kernel,
        grid=(m // tm, n // tn, k // tk),
        in_specs=[pl.BlockSpec((tm, tk), lambda i, j, kk: (i, kk)), pl.BlockSpec((tk, tn), lambda i, j, kk: (kk, j))],
        out_specs=pl.BlockSpec((tm, tn), lambda i, j, kk: (i, j)),
        out_shape=jax.ShapeDtypeStruct((m, n), F32),
        compiler_params=_cparams(("parallel", "parallel", "arbitrary")),
        name=name,
    )(a, w)


def _logf_kernel(zs_ref, bf_ref, logf_ref, cum_ref, carry_ref, *, tm):
    @pl.when(pl.program_id(0) == 0)
    def _():
        carry_ref[...] = jnp.zeros_like(carry_ref)

    x = zs_ref[:, SM_FG:SM_FG + H_FOX] + bf_ref[...]
    lf = jnp.minimum(x, 0.0) - jnp.log1p(jnp.exp(-jnp.abs(x)))
    logf_ref[...] = lf
    row = lax.broadcasted_iota(jnp.int32, (tm, tm), 0)
    col = lax.broadcasted_iota(jnp.int32, (tm, tm), 1)
    tri = (col <= row).astype(F32)
    cum = jnp.dot(tri, lf, preferred_element_type=F32, precision=lax.Precision.HIGHEST) + carry_ref[...]
    cum_ref[...] = cum
    carry_ref[...] = cum[tm - 1:tm, :]


def _logf_cumsum(z, b_f, tm=256):
    m = z.shape[0]
    tm = min(tm, m)
    out = jax.ShapeDtypeStruct((m, H_FOX), F32)
    return pl.pallas_call(
        functools.partial(_logf_kernel, tm=tm),
        grid=(m // tm,),
        in_specs=[pl.BlockSpec((tm, 128), lambda i: (i, Z_SM // 128)), pl.BlockSpec((1, H_FOX), lambda i: (0, 0))],
        out_specs=[pl.BlockSpec((tm, H_FOX), lambda i: (i, 0)), pl.BlockSpec((tm, H_FOX), lambda i: (i, 0))],
        out_shape=[out, out],
        scratch_shapes=[pltpu.VMEM((1, H_FOX), F32)],
        compiler_params=_cparams(("arbitrary",)),
        name="logf_cumsum",
    )(z, b_f.reshape(1, H_FOX))


def _split3(x):
    hi = x.astype(BF16).astype(F32)
    mid = (x - hi).astype(BF16).astype(F32)
    lo = (x - hi - mid).astype(BF16).astype(F32)
    return hi, mid, lo


IDX_K = 3 * IDX_DIM


def _split_operand(x, axis, left):
    hi = x.astype(BF16)
    lo = (x - hi.astype(F32)).astype(BF16)
    return jnp.concatenate([hi, hi, lo] if left else [hi, lo, hi], axis=axis)


def _lane_row(shape, values):
    lane = lax.broadcasted_iota(jnp.int32, shape, len(shape) - 1)
    out = jnp.zeros(shape, F32)
    for i, v in enumerate(values):
        out = jnp.where(lane == i, v, out)
    return out


def _prep_kernel(fk_ref, fv_ref, dk_ref, dv_ref, iq_ref, sm_ref, cum_ref,
                 fka_ref, fvt_ref, dka_ref, dvt_ref, iqs_ref, ikb_ref, sgn_ref, *, tm):
    i = pl.program_id(0)
    ones_rows = jnp.where(lax.broadcasted_iota(jnp.int32, (VT_ROWS - HEAD_DIM, tm), 0) == 0, 1.0, 0.0).astype(BF16)
    nf = cum_ref[...] * (-LOG2E)
    for h in range(H_FOX):
        sl = slice(h * HEAD_DIM, (h + 1) * HEAD_DIM)
        fka_ref[h, :, :HEAD_DIM] = fk_ref[:, sl].astype(BF16)
        fka_ref[h, :, HEAD_DIM:] = _lane_row((tm, HEAD_DIM), _split3(nf[:, h:h + 1])).astype(BF16)
        fvt_ref[h, :HEAD_DIM, :] = fv_ref[:, sl].T.astype(BF16)
        fvt_ref[h, HEAD_DIM:, :] = ones_rows
    kpos = i * tm + lax.broadcasted_iota(jnp.int32, (tm, 1), 0)
    a = (kpos // 64).astype(F32)
    b = (kpos % 64).astype(F32)
    pos_aug = _lane_row((tm, HEAD_DIM), (a, a, a, b, b, b)).astype(BF16)
    for g in range(H_DSA_KV):
        sl = slice(g * HEAD_DIM, (g + 1) * HEAD_DIM)
        dka_ref[g, :, :HEAD_DIM] = dk_ref[:, sl].astype(BF16)
        dka_ref[g, :, HEAD_DIM:] = pos_aug
        dvt_ref[g, :HEAD_DIM, :] = dv_ref[:, sl].T.astype(BF16)
        dvt_ref[g, HEAD_DIM:, :] = ones_rows
    sm = sm_ref[...]
    w = sm[:, SM_IW:SM_IW + H_IDX] * (H_IDX ** -0.5)
    wabs = jnp.abs(w) * (IDX_DIM ** -0.5)
    for h in range(H_IDX):
        iqs_ref[h] = _split_operand(iq_ref[:, h * IDX_DIM:(h + 1) * IDX_DIM] * wabs[:, h:h + 1], 1, True)
    ikb_ref[...] = _split_operand(sm[:, SM_IK:SM_IK + IDX_DIM], 1, False)
    sgn_ref[...] = jnp.where(sm > 0, 1.0, -1.0).T[SM_IW:SM_IW + H_IDX, :]


def _attn_prep(zf, zd, zi, cum, tm=256):
    s_len = zf.shape[0]
    blk = lambda w, off: pl.BlockSpec((tm, w), lambda i: (i, off // w))
    return pl.pallas_call(
        functools.partial(_prep_kernel, tm=tm),
        grid=(s_len // tm,),
        in_specs=[blk(FOX_W, Z_FK), blk(FOX_W, Z_FV), blk(DSA_KV_W, Z_DK), blk(DSA_KV_W, Z_DV), blk(IQ_W, 0),
                  blk(128, ZI_SM), pl.BlockSpec((tm, H_FOX), lambda i: (i, 0))],
        out_specs=[pl.BlockSpec((H_FOX, tm, 2 * HEAD_DIM), lambda i: (0, i, 0)),
                   pl.BlockSpec((H_FOX, VT_ROWS, tm), lambda i: (0, 0, i)),
                   pl.BlockSpec((H_DSA_KV, tm, 2 * HEAD_DIM), lambda i: (0, i, 0)),
                   pl.BlockSpec((H_DSA_KV, VT_ROWS, tm), lambda i: (0, 0, i)),
                   pl.BlockSpec((H_IDX, tm, IDX_K), lambda i: (0, i, 0)),
                   pl.BlockSpec((tm, IDX_K), lambda i: (i, 0)),
                   pl.BlockSpec((H_IDX, tm), lambda i: (0, i))],
        out_shape=[jax.ShapeDtypeStruct((H_FOX, s_len, 2 * HEAD_DIM), BF16),
                   jax.ShapeDtypeStruct((H_FOX, VT_ROWS, s_len), BF16),
                   jax.ShapeDtypeStruct((H_DSA_KV, s_len, 2 * HEAD_DIM), BF16),
                   jax.ShapeDtypeStruct((H_DSA_KV, VT_ROWS, s_len), BF16),
                   jax.ShapeDtypeStruct((H_IDX, s_len, IDX_K), BF16),
                   jax.ShapeDtypeStruct((s_len, IDX_K), BF16),
                   jax.ShapeDtypeStruct((H_IDX, s_len), F32)],
        compiler_params=_cparams(("parallel",)),
        name="attn_prep",
    )(zf, zf, zd, zd, zi, zi, cum)


def _flash_update(s, vt, m_ref, acc_ref):
    _flash_accumulate(*_flash_probs(s, m_ref), vt, acc_ref)


def _flash_probs(s, m_ref):
    m_prev = m_ref[...]
    m_new = jnp.maximum(m_prev, jnp.max(s, axis=0, keepdims=True))
    m_ref[...] = m_new
    return jnp.exp2(m_prev - m_new), jnp.exp2((s - m_new).astype(BF16))


def _flash_accumulate(alpha, p, vt, acc_ref):
    acc_ref[...] = alpha * acc_ref[...] + jnp.dot(vt, p, preferred_element_type=F32)


def _flash_finish(acc):
    return (acc[:HEAD_DIM] / acc[HEAD_DIM:HEAD_DIM + 1]).T


FOX_HEADS_PER_STEP = 8


def _causal_steps(n_q, last_key_block):
    steps = [(qi, ki) for qi in range(n_q) for ki in range(last_key_block(qi) + 1)]
    return (np.asarray([qi for qi, _ in steps], np.int32), np.asarray([ki for _, ki in steps], np.int32))


def _fox_kernel(qi_ref, ki_ref, q_ref, ka_ref, vt_ref, o_ref, qa_ref, m_ref, acc_ref, *, t):
    qi = qi_ref[pl.program_id(1)]
    ki = ki_ref[pl.program_id(1)]
    heads = range(FOX_HEADS_PER_STEP)

    @pl.when(ki == 0)
    def _():
        for h in heads:
            qa_ref[h, :, :HEAD_DIM] = (
                q_ref[:, h * HEAD_DIM:(h + 1) * HEAD_DIM] * (HEAD_DIM ** -0.5 * LOG2E)).astype(BF16)
            qa_ref[h, :, HEAD_DIM:] = _lane_row((t, HEAD_DIM), (1.0, 1.0, 1.0)).astype(BF16)
        m_ref[...] = jnp.full_like(m_ref, NEG)
        acc_ref[...] = jnp.zeros_like(acc_ref)

    def step(diagonal):
        logits = [lax.dot_general(ka_ref[h], qa_ref[h], _NT, preferred_element_type=F32) for h in heads]
        if diagonal:
            visible = (lax.broadcasted_iota(jnp.int32, (t, t), 0) <= lax.broadcasted_iota(jnp.int32, (t, t), 1))
            logits = [jnp.where(visible, s, NEG) for s in logits]
        probs = [_flash_probs(logits[h], m_ref.at[h]) for h in heads]
        for h in heads:
            _flash_accumulate(*probs[h], vt_ref[h], acc_ref.at[h])

    @pl.when(ki < qi)
    def _():
        step(False)

    @pl.when(ki == qi)
    def _():
        step(True)
        for h in heads:
            o_ref[:, h * HEAD_DIM:(h + 1) * HEAD_DIM] = _flash_finish(acc_ref[h])


def _fox_prompt(z, fka, fvt, t=512):
    s_len = z.shape[0]
    n = s_len // t
    hp = FOX_HEADS_PER_STEP
    w = hp * HEAD_DIM
    q_steps, k_steps = _causal_steps(n, lambda qi: qi)
    return pl.pallas_call(
        functools.partial(_fox_kernel, t=t),
        grid_spec=pltpu.PrefetchScalarGridSpec(
            num_scalar_prefetch=2,
            grid=(H_FOX // hp, q_steps.shape[0]),
            in_specs=[pl.BlockSpec((t, w), lambda h, s, qs, ks: (qs[s], Z_FQ // w + h)),
                      pl.BlockSpec((hp, t, 2 * HEAD_DIM), lambda h, s, qs, ks: (h, ks[s], 0)),
                      pl.BlockSpec((hp, VT_ROWS, t), lambda h, s, qs, ks: (h, 0, ks[s]))],
            out_specs=pl.BlockSpec((t, w), lambda h, s, qs, ks: (qs[s], h)),
            scratch_shapes=[pltpu.VMEM((hp, t, 2 * HEAD_DIM), BF16), pltpu.VMEM((hp, 1, t), F32),
                            pltpu.VMEM((hp, VT_ROWS, t), F32)]),
        out_shape=jax.ShapeDtypeStruct((s_len, FOX_W), F32),
        compiler_params=_cparams(("parallel", "arbitrary")),
        name="fox_prompt",
    )(q_steps, k_steps, z, fka, fvt)


def _bisect_threshold(count_ge, n_valid, k_top):
    def cond(carry):
        it, _, cnt = carry
        unsettled = jnp.sum(jnp.where(cnt > k_top, 1, 0))
        return (it < 32) & (unsettled > 0)

    def body(carry):
        it, thr, cnt = carry
        cand = thr + lax.shift_left(jnp.int32(1), jnp.int32(31) - it)
        c = count_ge(cand)
        take = c >= k_top
        return it + 1, jnp.where(take, cand, thr), jnp.where(take, c, cnt)

    init = (jnp.int32(0), jnp.full(n_valid.shape, INT_MIN, jnp.int32), n_valid)
    return lax.while_loop(cond, body, init)[1]


def _idx_kernel(iqs_ref, sgn_ref, ik_ref, bias_ref, key_ref, *, tq, tkc, k_top):
    i = pl.program_id(0)
    s_len = key_ref.shape[0]
    n_chunks = ((i + 1) * tq + tkc - 1) // tkc
    qpos = i * tq + lax.broadcasted_iota(jnp.int32, (tkc, tq), 1)

    def score_chunk(c, carry):
        off = pl.multiple_of(c * tkc, tkc)
        kc = ik_ref[pl.ds(off, tkc), :]
        acc = jnp.zeros((tkc, tq), F32)
        for h in range(H_IDX):
            d = lax.dot_general(kc, iqs_ref[h], _NT, preferred_element_type=F32)
            acc = acc + sgn_ref[h:h + 1, :] * jnp.maximum(d, 0.0)
        kpos = off + lax.broadcasted_iota(jnp.int32, (tkc, tq), 0)
        key_ref[pl.ds(off, tkc), :] = jnp.where(kpos <= qpos, _sort_key(acc), INT_MIN)
        return carry

    lax.fori_loop(0, n_chunks, score_chunk, 0)

    def count_ge(cand):
        def body(c, cnt):
            off = pl.multiple_of(c * tq, tq)
            hit = jnp.where(key_ref[pl.ds(off, tq), :] >= cand, 1, 0)
            return cnt + jnp.sum(hit.reshape(tq // 8, 8, tq), axis=0)

        cnt = lax.fori_loop(0, i + 1, body, jnp.zeros((8, tq), jnp.int32))
        return jnp.sum(cnt, axis=0, keepdims=True)

    n_valid = i * tq + lax.broadcasted_iota(jnp.int32, (1, tq), 1) + 1
    thr = _bisect_threshold(count_ge, n_valid, k_top)
    thr = jnp.maximum(thr, INT_MIN + 1)

    def write_chunk(c, carry):
        off = pl.multiple_of(c * tkc, tkc)
        bias_ref[pl.ds(off, tkc), :] = jnp.where(key_ref[pl.ds(off, tkc), :] >= thr, 0.0, NEG).astype(BF16)
        return carry

    lax.fori_loop(0, n_chunks, write_chunk, 0)

    def fill_chunk(c, carry):
        off = pl.multiple_of(c * tkc, tkc)
        bias_ref[pl.ds(off, tkc), :] = jnp.full((tkc, tq), NEG, BF16)
        return carry

    lax.fori_loop(n_chunks, s_len // tkc, fill_chunk, 0)


def _idx_prompt(iqs, sgn_t, ikb, k_top, tq=256, tkc=256):
    s_len = ikb.shape[0]
    return pl.pallas_call(
        functools.partial(_idx_kernel, tq=tq, tkc=tkc, k_top=k_top),
        grid=(s_len // tq,),
        in_specs=[pl.BlockSpec((H_IDX, tq, IDX_K), lambda i: (0, i, 0)),
                  pl.BlockSpec((H_IDX, tq), lambda i: (0, i)),
                  pl.BlockSpec((s_len, IDX_K), lambda i: (0, 0))],
        out_specs=pl.BlockSpec((s_len, tq), lambda i: (0, i)),
        out_shape=jax.ShapeDtypeStruct((s_len, s_len), BF16),
        scratch_shapes=[pltpu.VMEM((s_len, tq), jnp.int32)],
        compiler_params=_cparams(("parallel",), 56),
        name="idx_prompt",
    )(iqs, sgn_t, ikb)


def _alibi_slope(h):
    return 2.0 ** (-8.0 * (h + 1) / H_DSA)


def _bf16_terms(x):
    out = []
    for _ in range(3):
        t = float(np.float32(x).astype(BF16))
        out.append(t)
        x = x - t
    return out


def _dsa_kernel(qi_ref, ki_ref, q_ref, ka_ref, vt_ref, bias_ref, o_ref, qa_ref, m_ref, acc_ref, *, tq, tk):
    qi = qi_ref[pl.program_id(0)]
    ki = ki_ref[pl.program_id(0)]
    ki_last = (qi * tq + tq - 1) // tk

    @pl.when(ki == 0)
    def _():
        for h in range(H_DSA):
            g, j = divmod(h, DSA_GROUP)
            rows = slice(j * tq, (j + 1) * tq)
            qa_ref[g, rows, :HEAD_DIM] = (
                q_ref[:, h * HEAD_DIM:(h + 1) * HEAD_DIM] * (HEAD_DIM ** -0.5 * LOG2E)).astype(BF16)
            sl = _bf16_terms(_alibi_slope(h) * LOG2E)
            qa_ref[g, rows, HEAD_DIM:] = _lane_row((tq, HEAD_DIM), [64.0 * t for t in sl] + sl).astype(BF16)
        m_ref[...] = jnp.full_like(m_ref, NEG)
        acc_ref[...] = jnp.zeros_like(acc_ref)

    mask = jnp.concatenate([bias_ref[...].astype(F32)] * DSA_GROUP, axis=1)
    groups = range(H_DSA_KV)
    logits = [lax.dot_general(ka_ref[g], qa_ref[g], _NT, preferred_element_type=F32) for g in groups]
    probs = [_flash_probs(logits[g] + mask, m_ref.at[g]) for g in groups]
    for g in groups:
        _flash_accumulate(*probs[g], vt_ref[g], acc_ref.at[g])

    @pl.when(ki == ki_last)
    def _():
        for h in range(H_DSA):
            g, j = divmod(h, DSA_GROUP)
            o_ref[:, h * HEAD_DIM:(h + 1) * HEAD_DIM] = _flash_finish(acc_ref[g, :, j * tq:(j + 1) * tq])


def _dsa_prompt(z, dka, dvt, bias_t, tq=256, tk=512):
    s_len = z.shape[0]
    q_steps, k_steps = _causal_steps(s_len // tq, lambda qi: (qi * tq + tq - 1) // tk)
    return pl.pallas_call(
        functools.partial(_dsa_kernel, tq=tq, tk=tk),
        grid_spec=pltpu.PrefetchScalarGridSpec(
            num_scalar_prefetch=2,
            grid=(q_steps.shape[0],),
            in_specs=[pl.BlockSpec((tq, DSA_W), lambda s, qs, ks: (qs[s], Z_DQ // DSA_W)),
                      pl.BlockSpec((H_DSA_KV, tk, 2 * HEAD_DIM), lambda s, qs, ks: (0, ks[s], 0)),
                      pl.BlockSpec((H_DSA_KV, VT_ROWS, tk), lambda s, qs, ks: (0, 0, ks[s])),
                      pl.BlockSpec((tk, tq), lambda s, qs, ks: (ks[s], qs[s]))],
            out_specs=pl.BlockSpec((tq, DSA_W), lambda s, qs, ks: (qs[s], 0)),
            scratch_shapes=[pltpu.VMEM((H_DSA_KV, DSA_GROUP * tq, 2 * HEAD_DIM), BF16),
                            pltpu.VMEM((H_DSA_KV, 1, DSA_GROUP * tq), F32),
                            pltpu.VMEM((H_DSA_KV, VT_ROWS, DSA_GROUP * tq), F32)]),
        out_shape=jax.ShapeDtypeStruct((s_len, DSA_W), F32),
        compiler_params=_cparams(("arbitrary",)),
        name="dsa_prompt",
    )(q_steps, k_steps, z, dka, dvt, bias_t)


def _dec_idx_kernel(pt_ref, *refs, n_pages, t_new):
    del pt_ref
    ik_refs = refs[:n_pages]
    lf_refs = refs[n_pages:2 * n_pages]
    iq_ref, w_ref, ikn_ref, lfn_ref, key_ref, fneg_ref = refs[2 * n_pages:]

    row = lax.broadcasted_iota(jnp.int32, (PAGE_SIZE, PAGE_SIZE), 0)
    col = lax.broadcasted_iota(jnp.int32, (PAGE_SIZE, PAGE_SIZE), 1)
    tri = (row <= col).astype(F32)
    carry = jnp.zeros((H_FOX, 1), F32)
    for p in range(n_pages + 1):
        lf = lf_refs[p][...] if p < n_pages else lfn_ref[...]
        cum = jnp.dot(lf, tri, preferred_element_type=F32, precision=lax.Precision.HIGHEST) + carry
        fneg_ref[:, p * PAGE_SIZE:(p + 1) * PAGE_SIZE] = -cum
        carry = cum[:, PAGE_SIZE - 1:PAGE_SIZE]

    w = w_ref[...] * (H_IDX ** -0.5)
    iqs = _split_operand(iq_ref[...] * (jnp.abs(w) * (IDX_DIM ** -0.5)), 1, True)
    sgn = jnp.where(w > 0, 1.0, -1.0)
    qrow = lax.broadcasted_iota(jnp.int32, (16, PAGE_SIZE), 0) % t_new
    lane = lax.broadcasted_iota(jnp.int32, (16, PAGE_SIZE), 1)
    for p in range(n_pages + 1):
        kp_t = _split_operand((ik_refs[p] if p < n_pages else ikn_ref)[...], 0, False)
        d = jnp.dot(iqs, kp_t, preferred_element_type=F32)
        sc = jnp.sum((sgn * jnp.maximum(d, 0.0)).reshape(H_IDX, 16, PAGE_SIZE), axis=0)
        key = _sort_key(sc)
        if p == n_pages:
            key = jnp.where((lane <= qrow) & (lane < t_new), key, INT_MIN)
        key_ref[:, p * PAGE_SIZE:(p + 1) * PAGE_SIZE] = key


def _dec_idx(page_table, idx_pages, logf_pages_t, iq16, w16, ik_new_t, logf_new_t, t_new):
    b, n_pages = page_table.shape
    width = (n_pages + 1) * PAGE_SIZE
    page = lambda p: (lambda bi, pt: (pt[bi, p], 0, 0))
    per_b = lambda bi, pt: (bi, 0, 0)
    in_specs = ([pl.BlockSpec((None, IDX_DIM, PAGE_SIZE), page(p)) for p in range(n_pages)]
                + [pl.BlockSpec((None, H_FOX, PAGE_SIZE), page(p)) for p in range(n_pages)]
                + [pl.BlockSpec((None, H_IDX * 16, IDX_DIM), per_b), pl.BlockSpec((None, H_IDX * 16, 1), per_b),
                   pl.BlockSpec((None, IDX_DIM, PAGE_SIZE), per_b), pl.BlockSpec((None, H_FOX, PAGE_SIZE), per_b)])
    return pl.pallas_call(
        functools.partial(_dec_idx_kernel, n_pages=n_pages, t_new=t_new),
        grid_spec=pltpu.PrefetchScalarGridSpec(
            num_scalar_prefetch=1,
            grid=(b,),
            in_specs=in_specs,
            out_specs=[pl.BlockSpec((None, 16, width), per_b), pl.BlockSpec((None, H_FOX, width), per_b)]),
        out_shape=[jax.ShapeDtypeStruct((b, 16, width), jnp.int32), jax.ShapeDtypeStruct((b, H_FOX, width), F32)],
        compiler_params=_cparams(("parallel",)),
        name="dec_idx",
    )(page_table, *([idx_pages] * n_pages), *([logf_pages_t] * n_pages), iq16, w16, ik_new_t, logf_new_t)


def _dec_select_kernel(key_ref, nv_ref, sel_ref, *, k_top):
    def count_ge(cand):
        return jnp.sum(jnp.where(key_ref[...] >= cand, 1, 0), axis=-1, keepdims=True)

    thr = jnp.maximum(_bisect_threshold(count_ge, nv_ref[...], k_top), INT_MIN + 1)
    sel_ref[...] = jnp.where(key_ref[...] >= thr, 0.0, NEG)


def _dec_select(keys, n_valid, k_top, tm=256):
    rows, width = keys.shape
    tm = min(tm, rows)
    return pl.pallas_call(
        functools.partial(_dec_select_kernel, k_top=k_top),
        grid=(rows // tm,),
        in_specs=[pl.BlockSpec((tm, width), lambda i: (i, 0)), pl.BlockSpec((tm, 1), lambda i: (i, 0))],
        out_specs=pl.BlockSpec((tm, width), lambda i: (i, 0)),
        out_shape=jax.ShapeDtypeStruct((rows, width), F32),
        compiler_params=_cparams(("parallel",)),
        name="dec_select",
    )(keys, n_valid)


def _dec_attn_kernel(pt_ref, *refs, n_pages, kp, t_new):
    del pt_ref
    fk_refs, fv_refs, dk_refs, dv_refs = (refs[i * kp:(i + 1) * kp] for i in range(4))
    (fkn_ref, fvn_ref, dkn_ref, dvn_ref, qf_ref, qd_ref, fneg_ref, dbias_ref, fneg_new_ref, dbias_new_ref, slope_ref,
     of_ref, od_ref, mf_ref, lf_ref, accf_ref, md_ref, ld_ref, accd_ref, pairf_ref, paird_ref) = refs[4 * kp:]
    p_id = pl.program_id(1)
    n_steps = n_pages // kp
    rows_q = t_new * H_FOX
    wf = PAGE_SIZE * H_FOX
    wd = PAGE_SIZE * H_DSA_KV

    def pair_mask(n_keys, heads_per_key_head, n_kv):
        qh = lax.broadcasted_iota(jnp.int32, (rows_q, n_keys), 0) % H_FOX
        kh = lax.broadcasted_iota(jnp.int32, (rows_q, n_keys), 1) % n_kv
        return jnp.where(qh // heads_per_key_head == kh, 0.0, NEG)

    @pl.when(p_id == 0)
    def _():
        mf_ref[...] = jnp.full_like(mf_ref, NEG)
        lf_ref[...] = jnp.zeros_like(lf_ref)
        accf_ref[...] = jnp.zeros_like(accf_ref)
        md_ref[...] = jnp.full_like(md_ref, NEG)
        ld_ref[...] = jnp.zeros_like(ld_ref)
        accd_ref[...] = jnp.zeros_like(accd_ref)
        pairf_ref[...] = pair_mask(wf, 1, H_FOX)
        paird_ref[...] = pair_mask(wd, DSA_GROUP, H_DSA_KV)

    def attend(q_ref, k, v, bias, m_ref, l_ref, acc_ref):
        q = (q_ref[...] * (HEAD_DIM ** -0.5)).astype(BF16)
        s = lax.dot_general(q, k, _NT, preferred_element_type=F32) + bias
        m_prev = m_ref[...]
        m_new = jnp.maximum(m_prev, jnp.max(s, axis=-1, keepdims=True))
        alpha = jnp.exp(m_prev - m_new)
        p = jnp.exp(s - m_new)
        l_ref[...] = alpha * l_ref[...] + jnp.sum(p, axis=-1, keepdims=True)
        acc_ref[...] = alpha * acc_ref[...] + jnp.dot(p.astype(BF16), v, preferred_element_type=F32)
        m_ref[...] = m_new

    def sel_rows(db):
        return jnp.concatenate([jnp.broadcast_to(db[q:q + 1], (H_DSA, db.shape[1])) for q in range(t_new)], axis=0)

    def alibi(n_keys, tok0):
        tok = tok0 + lax.broadcasted_iota(jnp.int32, (1, n_keys), 1) // H_DSA_KV
        return slope_ref[...] * tok.astype(F32)

    cat = lambda page_refs: jnp.concatenate([r[...].astype(BF16) for r in page_refs], axis=0)
    bias = jnp.concatenate([pairf_ref[...]] * kp, axis=1) + fneg_ref[...]
    attend(qf_ref, cat(fk_refs), cat(fv_refs), bias, mf_ref, lf_ref, accf_ref)
    bias = (jnp.concatenate([paird_ref[...]] * kp, axis=1) + sel_rows(dbias_ref[...])
            + alibi(kp * wd, (p_id * kp - n_pages) * PAGE_SIZE))
    attend(qd_ref, cat(dk_refs), cat(dv_refs), bias, md_ref, ld_ref, accd_ref)

    @pl.when(p_id == n_steps - 1)
    def _():
        def causal(n_keys, n_kv):
            q = lax.broadcasted_iota(jnp.int32, (rows_q, n_keys), 0) // H_FOX
            tok = lax.broadcasted_iota(jnp.int32, (rows_q, n_keys), 1) // n_kv
            return jnp.where(tok <= q, 0.0, NEG)

        nf = t_new * H_FOX
        bias = pairf_ref[:, :nf] + fneg_new_ref[:, :nf] + causal(nf, H_FOX)
        attend(qf_ref, fkn_ref[...].astype(BF16), fvn_ref[...].astype(BF16), bias, mf_ref, lf_ref, accf_ref)
        nd = t_new * H_DSA_KV
        bias = paird_ref[:, :nd] + sel_rows(dbias_new_ref[:, :nd]) + alibi(nd, 0)
        attend(qd_ref, dkn_ref[...].astype(BF16), dvn_ref[...].astype(BF16), bias, md_ref, ld_ref, accd_ref)
        of_ref[...] = accf_ref[...] / lf_ref[...]
        od_ref[...] = accd_ref[...] / ld_ref[...]


def _dec_attn(page_table, fox_k, fox_v, dsa_k, dsa_v, fk_new, fv_new, dk_new, dv_new, qf, qd, fneg_rows, dbias_rows,
              slopes, t_new):
    b, n_pages = page_table.shape
    kp = 4 if n_pages % 4 == 0 else 1
    n_steps = n_pages // kp
    wf = PAGE_SIZE * H_FOX
    wd = PAGE_SIZE * H_DSA_KV
    rows_q = t_new * H_FOX

    def page(j):
        return lambda bi, p, pt: (pt[bi, p * kp + j], 0, 0)

    per_b = lambda bi, p, pt: (bi, 0, 0)
    past = lambda bi, p, pt: (bi, 0, p)
    new = lambda bi, p, pt: (bi, 0, n_pages)
    in_specs = ([pl.BlockSpec((None, wf, HEAD_DIM), page(j)) for j in range(kp)] * 2
                + [pl.BlockSpec((None, wd, HEAD_DIM), page(j)) for j in range(kp)] * 2
                + [pl.BlockSpec((None, t_new * H_FOX, HEAD_DIM), per_b)] * 2
                + [pl.BlockSpec((None, t_new * H_DSA_KV, HEAD_DIM), per_b)] * 2
                + [pl.BlockSpec((None, rows_q, HEAD_DIM), per_b)] * 2
                + [pl.BlockSpec((None, 1, kp * wf), past), pl.BlockSpec((None, 8, kp * wd), past),
                   pl.BlockSpec((None, 1, wf), new), pl.BlockSpec((None, 8, wd), new),
                   pl.BlockSpec((rows_q, 1), lambda bi, p, pt: (0, 0))])
    out = jax.ShapeDtypeStruct((b, rows_q, HEAD_DIM), F32)
    col = pltpu.VMEM((rows_q, 1), F32)
    acc = pltpu.VMEM((rows_q, HEAD_DIM), F32)
    return pl.pallas_call(
        functools.partial(_dec_attn_kernel, n_pages=n_pages, kp=kp, t_new=t_new),
        grid_spec=pltpu.PrefetchScalarGridSpec(
            num_scalar_prefetch=1,
            grid=(b, n_steps),
            in_specs=in_specs,
            out_specs=[pl.BlockSpec((None, rows_q, HEAD_DIM), per_b)] * 2,
            scratch_shapes=[col, col, acc, col, col, acc, pltpu.VMEM((rows_q, wf), F32),
                            pltpu.VMEM((rows_q, wd), F32)]),
        out_shape=[out, out],
        compiler_params=_cparams(("parallel", "arbitrary")),
        name="dec_attn",
    )(page_table, *([fox_k] * kp), *([fox_v] * kp), *([dsa_k] * kp), *([dsa_v] * kp),
      fk_new, fv_new, dk_new, dv_new, qf, qd, fneg_rows, dbias_rows, fneg_rows, dbias_rows, slopes)


def _ffn_up_kernel(h_ref, hprev_ref, wg_ref, wu_ref, wc_ref, bc_ref, a_ref, tail_ref, *, tm):
    i = pl.program_id(0)
    h = h_ref[...]
    wg = wg_ref[...].astype(BF16)
    g = jnp.dot(h, wg, preferred_element_type=F32)
    u = jnp.dot(h, wu_ref[...].astype(BF16), preferred_element_type=F32)
    gh = jnp.dot(hprev_ref[...], wg, preferred_element_type=F32)
    gh = jnp.where(i > 0, gh, 0.0)
    ext = jnp.concatenate([gh, g], axis=0)
    g1 = pltpu.roll(ext, 1, 0)[16:]
    g2 = pltpu.roll(ext, 2, 0)[16:]
    wc = wc_ref[...]
    conv = bc_ref[...] + wc[0:1] * g2 + wc[1:2] * g1 + wc[2:3] * g
    a_ref[...] = (conv * _sigmoid(conv) * u).astype(a_ref.dtype)
    tail_ref[...] = g[tm - 8:]


def _ffn_up_prompt(h2, w_gate, w_up, w_conv, b_conv, tm=1024, tn=256):
    m, d = h2.shape
    d_ff = w_gate.shape[1]
    tm = min(tm, m)
    return pl.pallas_call(
        functools.partial(_ffn_up_kernel, tm=tm),
        grid=(m // tm, d_ff // tn),
        in_specs=[pl.BlockSpec((tm, d), lambda i, j: (i, 0)),
                  pl.BlockSpec((16, d), lambda i, j: (jnp.maximum(i * (tm // 16) - 1, 0), 0)),
                  pl.BlockSpec((d, tn), lambda i, j: (0, j)), pl.BlockSpec((d, tn), lambda i, j: (0, j)),
                  pl.BlockSpec((CONV_W, tn), lambda i, j: (0, j)), pl.BlockSpec((1, tn), lambda i, j: (0, j))],
        out_specs=[pl.BlockSpec((tm, tn), lambda i, j: (i, j)), pl.BlockSpec((8, tn), lambda i, j: (i, j))],
        out_shape=[jax.ShapeDtypeStruct((m, d_ff), BF16), jax.ShapeDtypeStruct((m // tm * 8, d_ff), F32)],
        compiler_params=_cparams(("parallel", "parallel")),
        name="ffn_up_prompt",
    )(h2, h2, w_gate, w_up, w_conv, b_conv.reshape(1, d_ff))


def _ffn_up_dec_kernel(h_ref, wg_ref, wu_ref, wc_ref, bc_ref, s0_ref, s1_ref, a_ref, g_ref, *, t_new):
    h = h_ref[...]
    g = jnp.dot(h, wg_ref[...].astype(BF16), preferred_element_type=F32)
    u = jnp.dot(h, wu_ref[...].astype(BF16), preferred_element_type=F32)
    g_ref[...] = g
    t = lax.broadcasted_iota(jnp.int32, g.shape, 0) % t_new
    g1 = jnp.where(t >= 1, pltpu.roll(g, 1, 0), 0.0) + s1_ref[...]
    g2 = jnp.where(t >= 2, pltpu.roll(g, 2, 0), 0.0) + s0_ref[...]
    wc = wc_ref[...]
    conv = bc_ref[...] + wc[0:1] * g2 + wc[1:2] * g1 + wc[2:3] * g
    a_ref[...] = (conv * _sigmoid(conv) * u).astype(a_ref.dtype)


def _ffn_up_dec(h2, w_gate, w_up, w_conv, b_conv, tap0, tap1, t_new, tn=256):
    m, d = h2.shape
    d_ff = w_gate.shape[1]
    col = lambda j: (0, j)
    return pl.pallas_call(
        functools.partial(_ffn_up_dec_kernel, t_new=t_new),
        grid=(d_ff // tn,),
        in_specs=[pl.BlockSpec((m, d), lambda j: (0, 0)), pl.BlockSpec((d, tn), col), pl.BlockSpec((d, tn), col),
                  pl.BlockSpec((CONV_W, tn), col), pl.BlockSpec((1, tn), col),
                  pl.BlockSpec((m, tn), col), pl.BlockSpec((m, tn), col)],
        out_specs=[pl.BlockSpec((m, tn), col), pl.BlockSpec((m, tn), col)],
        out_shape=[jax.ShapeDtypeStruct((m, d_ff), BF16), jax.ShapeDtypeStruct((m, d_ff), F32)],
        compiler_params=_cparams(("parallel",)),
        name="ffn_up_dec",
    )(h2, w_gate, w_up, w_conv, b_conv.reshape(1, d_ff), tap0, tap1)


def _split_mod(mod):
    return jnp.split(mod, 6, axis=-1)


def _attn_out_and_ffn_in(x, o_f, o_d, mods, p):
    _, _, gt1, sh2, sc2, _ = mods
    a = _attn_norm(o_f, o_d, p["g_fox_out"], p["g_dsa_out"])
    o = _matmul(a, p["w_out"], 1024, 512, "out_proj")
    return _res_pre(o, x, gt1, p["g_post_attn"], p["g_pre_ffn"], sc2, sh2)


def _ffn_down_and_res(a, x1, mods, p):
    d_ff = a.shape[1]
    f = _matmul_ksplit(a, p["w_down"], 1024, 512, d_ff // 2, "ffn_down")
    return _res(f, x1, mods[5], p["g_post_ffn"])


def _project(x, mods, p):
    sh1, sc1 = mods[0], mods[1]
    h, h_lo = _prenorm(x, p["g_pre_attn"], sc1, sh1)
    return (_matmul_nt(h, p["w_in_t"], 1024, ZF_TN, "in_proj_fox", n=ZF_W),
            _matmul_nt(h, p["w_in_tail_t"], 1024, ZD_TN, "in_proj_rest"),
            _matmul_nt_split(h, h_lo, p["w_idx_t"], 512, ZI_TN, "in_proj_idx"))


def _kv_states(zf, zd, logf, conv):
    return (zf[:, Z_FK:Z_FK + FOX_W], zf[:, Z_FV:Z_FV + FOX_W], logf, zd[:, Z_DK:Z_DK + DSA_KV_W],
            zd[:, Z_DV:Z_DV + DSA_KV_W], zd[:, Z_SM + SM_IK:Z_SM + SM_IK + IDX_DIM], conv)


def _prompt_layer(x, mods, p):
    s_len = x.shape[0]
    zf, zd, zi = _project(x, mods, p)
    logf, cum = _logf_cumsum(zd, p["b_f"])
    fka, fvt, dka, dvt, iqs, ikb, sgn_t = _attn_prep(zf, zd, zi, cum)
    o_f = _fox_prompt(zf, fka, fvt)
    bias_t = _idx_prompt(iqs, sgn_t, ikb, min(TOPK_MAX, s_len // 4))
    o_d = _dsa_prompt(zd, dka, dvt, bias_t)
    x1, h2 = _attn_out_and_ffn_in(x, o_f, o_d, mods, p)
    a, tails = _ffn_up_prompt(h2, p["w_gate"], p["w_up"], p["w_conv"], p["b_conv"])
    y = _ffn_down_and_res(a, x1, mods, p)
    return y, _kv_states(zf, zd, logf, tails[-(CONV_W - 1):])


def _pad_rows(x, rows):
    return jnp.pad(x, ((0, 0), (0, rows - x.shape[1]), (0, 0)))


def _decode_layer(x, mods, p, page_table, caches, state_conv, t_new):
    m = x.shape[0]
    b = m // t_new
    n_pages = page_table.shape[1]
    cache_fox_k, cache_fox_v, cache_fox_logf_t, cache_dsa_k, cache_dsa_v, cache_idx_k = caches
    zf, zd, zi = _project(x, mods, p)
    logf, _ = _logf_cumsum(zd, p["b_f"])
    zf3 = zf.reshape(b, t_new, ZF_W)
    zd3 = zd.reshape(b, t_new, ZD_W)
    zi3 = zi.reshape(b, t_new, ZI_W)

    rep = 16 // t_new
    iq16 = jnp.tile(zi3[:, :, :IQ_W].reshape(b, t_new, H_IDX, IDX_DIM).transpose(0, 2, 1, 3),
                    (1, 1, rep, 1)).reshape(b, H_IDX * 16, IDX_DIM)
    w16 = jnp.tile(zi3[:, :, ZI_SM + SM_IW:ZI_SM + SM_IW + H_IDX].transpose(0, 2, 1), (1, 1, rep)).reshape(
        b, H_IDX * 16, 1)
    ik_new_t = jnp.pad(zi3[:, :, ZI_SM + SM_IK:ZI_SM + SM_IK + IDX_DIM].transpose(0, 2, 1),
                       ((0, 0), (0, 0), (0, PAGE_SIZE - t_new)))
    logf_new_t = jnp.pad(logf.reshape(b, t_new, H_FOX).transpose(0, 2, 1), ((0, 0), (0, 0), (0, PAGE_SIZE - t_new)))
    k_top = min(TOPK_MAX, (n_pages * PAGE_SIZE + t_new) // 4)
    keys, fneg = _dec_idx(page_table, cache_idx_k, cache_fox_logf_t, iq16, w16, ik_new_t, logf_new_t, t_new)
    n_valid = jnp.tile(n_pages * PAGE_SIZE + 1 + jnp.arange(8, dtype=jnp.int32) % t_new, b).reshape(b * 8, 1)
    dbias = _dec_select(keys[:, :8].reshape(b * 8, -1), n_valid, k_top).reshape(b, 8, -1)

    heads = lambda z3, off, w: z3[:, :, off:off + w].reshape(b, t_new * (w // HEAD_DIM), HEAD_DIM)
    fneg_rows = fneg.transpose(0, 2, 1).reshape(b, 1, -1)
    dbias_rows = jnp.repeat(dbias, H_DSA_KV, axis=-1)
    slopes = jnp.tile(2.0 ** (-8.0 * jnp.arange(1, H_DSA + 1, dtype=F32) / H_DSA), t_new).reshape(H_DSA * t_new, 1)
    o_f, o_d = _dec_attn(
        page_table, cache_fox_k, cache_fox_v, cache_dsa_k, cache_dsa_v,
        heads(zf3, Z_FK, FOX_W), heads(zf3, Z_FV, FOX_W), heads(zd3, Z_DK, DSA_KV_W), heads(zd3, Z_DV, DSA_KV_W),
        heads(zf3, Z_FQ, FOX_W), heads(zd3, Z_DQ, DSA_W), fneg_rows, dbias_rows, slopes, t_new)

    x1, h2 = _attn_out_and_ffn_in(x, o_f.reshape(m, FOX_W), o_d.reshape(m, DSA_W), mods, p)
    d_ff = state_conv.shape[-1]
    zero = jnp.zeros((b, 1, d_ff), F32)
    s0, s1 = state_conv[:, 0:1], state_conv[:, 1:2]
    tap0 = jnp.concatenate([s0, s1] + [zero] * (t_new - 2), axis=1).reshape(m, d_ff)
    tap1 = jnp.concatenate([s1] + [zero] * (t_new - 1), axis=1).reshape(m, d_ff)
    a, g = _ffn_up_dec(h2, p["w_gate"], p["w_up"], p["w_conv"], p["b_conv"], tap0, tap1, t_new)
    y = _ffn_down_and_res(a, x1, mods, p)
    conv_new = g.reshape(b, t_new, d_ff)[:, t_new - (CONV_W - 1):]
    return y, _kv_states(zf, zd, logf, conv_new)


def _regroup_w_in_tail(w_in_t):
    n, d = w_in_t.shape
    rest = w_in_t[ZF_W + H_FOX:].astype(BF16)
    fg = w_in_t[ZF_W:ZF_W + H_FOX].astype(BF16)
    assert n - ZF_W - H_FOX == Z_SM + SM_IW + H_IDX and SM_FG == SM_IW + H_IDX
    return jnp.concatenate([rest, fg, jnp.zeros((ZD_W - (n - ZF_W), d), BF16)], axis=0)


def kernel(x_prompt, x_sample, c_prompt, c_sample, page_table, cache_fox_k, cache_fox_v, cache_fox_logf, cache_dsa_k, cache_dsa_v, cache_idx_k, state_conv, w_in, b_f, w_out, g_fox_out, g_dsa_out, w_mod, b_mod, g_pre_attn, g_post_attn, g_pre_ffn, g_post_ffn, w_gate, w_up, w_conv, b_conv, w_down):
    depth = w_in.shape[0]
    bp, s_len, d = x_prompt.shape
    bs, t_new, _ = x_sample.shape
    assert bp == 1 and t_new >= CONV_W - 1 and 16 % t_new == 0

    xp = x_prompt.reshape(s_len, d)
    xs = x_sample.reshape(bs * t_new, d)
    n_c = bp + bs
    c_all = jnp.pad(jnp.concatenate([c_prompt, c_sample], axis=0), ((0, -n_c % 8), (0, 0)))
    n_phys = cache_fox_k.shape[1]
    fold = lambda c: c.reshape((depth * n_phys,) + c.shape[2:])
    rows = lambda c: c.reshape(depth * n_phys, PAGE_SIZE * c.shape[3], HEAD_DIM)
    caches = (rows(cache_fox_k), rows(cache_fox_v), fold(cache_fox_logf).transpose(0, 2, 1), rows(cache_dsa_k),
              rows(cache_dsa_v), fold(cache_idx_k).transpose(0, 2, 1))
    p_states, s_states = [], []
    for l in range(depth):
        w_in_t = w_in[l].T
        p = dict(w_in_t=w_in_t, w_in_tail_t=_regroup_w_in_tail(w_in_t),
                 w_idx_t=jnp.pad(w_in_t[ZF_W + H_FOX + Z_IQ:], ((0, ZI_W - (ZI_SM + SM_FG)), (0, 0))), b_f=b_f[l], w_out=w_out[l].astype(BF16),
                 g_fox_out=g_fox_out[l][None], g_dsa_out=g_dsa_out[l][None],
                 g_pre_attn=g_pre_attn[l][None], g_post_attn=g_post_attn[l][None],
                 g_pre_ffn=g_pre_ffn[l][None], g_post_ffn=g_post_ffn[l][None],
                 w_gate=w_gate[l], w_up=w_up[l], w_conv=w_conv[l], b_conv=b_conv[l],
                 w_down=w_down[l].astype(BF16))
        mod = _modulation(c_all, w_mod[l], b_mod[l])
        mods_p = _split_mod(mod[:bp])
        mods_s = [jnp.repeat(v, t_new, axis=0) for v in _split_mod(mod[bp:n_c])]
        xp, st_p = _prompt_layer(xp, mods_p, p)
        xs, st_s = _decode_layer(xs, mods_s, p, page_table + l * n_phys, caches, state_conv[l], t_new)
        p_states.append(st_p)
        s_states.append(st_s)

    def stack(states, i, shape):
        return jnp.stack([st[i].reshape(shape) for st in states])

    d_ff = state_conv.shape[-1]
    outs = [xp.reshape(bp, s_len, d), xs.reshape(bs, t_new, d)]
    for states, (bb, tt) in ((p_states, (bp, s_len)), (s_states, (bs, t_new))):
        outs += [stack(states, 0, (bb, tt, H_FOX, HEAD_DIM)), stack(states, 1, (bb, tt, H_FOX, HEAD_DIM)),
                 stack(states, 2, (bb, tt, H_FOX)), stack(states, 3, (bb, tt, H_DSA_KV, HEAD_DIM)),
                 stack(states, 4, (bb, tt, H_DSA_KV, HEAD_DIM)), stack(states, 5, (bb, tt, IDX_DIM)),
                 stack(states, 6, (bb, CONV_W - 1, d_ff))]
    return tuple(outs)
```

```python
import functools

import jax
import jax.numpy as jnp
import numpy as np
from jax import lax
from jax.experimental import pallas as pl
from jax.experimental.pallas import tpu as pltpu

HEAD_DIM = 128
H_FOX = 16
H_DSA = 16
H_DSA_KV = 4
DSA_GROUP = H_DSA // H_DSA_KV
H_IDX = 16
IDX_DIM = 64
TOPK_MAX = 256
CONV_W = 3
PAGE_SIZE = 128
RMS_EPS = 1e-6
FOX_W = H_FOX * HEAD_DIM
DSA_W = H_DSA * HEAD_DIM
DSA_KV_W = H_DSA_KV * HEAD_DIM
IQ_W = H_IDX * IDX_DIM

Z_FQ = 0
Z_FK = Z_FQ + FOX_W
Z_FV = Z_FK + FOX_W
ZF_W = Z_FV + FOX_W
ZF_TN = 512
Z_DQ = 0
Z_DK = Z_DQ + DSA_W
Z_DV = Z_DK + DSA_KV_W
IQ_ROW = ZF_W + H_FOX + DSA_W + 2 * DSA_KV_W
Z_SM = Z_DV + DSA_KV_W
SM_IK = 0
SM_IW = SM_IK + IDX_DIM
SM_FG = SM_IW + H_IDX
ZD_TN = 768
ZD_W = -(-(Z_SM + 128) // ZD_TN) * ZD_TN
ZI_SM = IQ_W
ZI_TN = 384
ZI_W = -(-(ZI_SM + SM_FG) // ZI_TN) * ZI_TN

NEG = -1e30
LOG2E = 1.4426950408889634
VT_ROWS = HEAD_DIM + 16
INT_MIN = -2 ** 31
MIB = 1024 * 1024
BF16 = jnp.bfloat16
F32 = jnp.float32

_NT = (((1,), (1,)), ((), ()))


def _cparams(sem, vmem_mib=48):
    return pltpu.CompilerParams(dimension_semantics=sem, vmem_limit_bytes=vmem_mib * MIB)


def _rms(x, g):
    return x * lax.rsqrt(jnp.mean(x * x, axis=-1, keepdims=True) + RMS_EPS) * g


def _sigmoid(x):
    return 1.0 / (1.0 + jnp.exp(-x))


def _sort_key(x):
    b = pltpu.bitcast(x, jnp.int32)
    return b ^ ((b >> 31) & jnp.int32(0x7FFFFFFF))


def _row_spec(arr, tm):
    d = arr.shape[1]
    if arr.shape[0] == 1:
        return pl.BlockSpec((1, d), lambda i: (0, 0))
    return pl.BlockSpec((tm, d), lambda i: (i, 0))


def _mod_kernel(c_ref, w_ref, b_ref, o_ref):
    c = c_ref[...]
    a = (c * _sigmoid(c)).astype(BF16)
    o_ref[...] = jnp.dot(a, w_ref[...].astype(BF16), preferred_element_type=F32) + b_ref[...]


def _modulation(c, w_mod, b_mod):
    r, d = c.shape
    n = w_mod.shape[1]
    tn = 512
    return pl.pallas_call(
        _mod_kernel,
        grid=(n // tn,),
        in_specs=[pl.BlockSpec((r, d), lambda j: (0, 0)),
                  pl.BlockSpec((d, tn), lambda j: (0, j)),
                  pl.BlockSpec((1, tn), lambda j: (0, j))],
        out_specs=pl.BlockSpec((r, tn), lambda j: (0, j)),
        out_shape=jax.ShapeDtypeStruct((r, n), F32),
        compiler_params=_cparams(("parallel",)),
        name="modulation",
    )(c, w_mod, b_mod.reshape(1, n))


def _prenorm_kernel(x_ref, g_ref, sc_ref, sh_ref, o_ref, lo_ref):
    y = _rms(x_ref[...], g_ref[...])
    h = y * (1.0 + sc_ref[...]) + sh_ref[...]
    hi = h.astype(BF16)
    o_ref[...] = hi
    lo_ref[...] = (h - hi.astype(F32)).astype(BF16)


def _prenorm(x, g, sc, sh, tm=256):
    m, d = x.shape
    tm = min(tm, m)
    blk = pl.BlockSpec((tm, d), lambda i: (i, 0))
    out = jax.ShapeDtypeStruct((m, d), BF16)
    return pl.pallas_call(
        _prenorm_kernel,
        grid=(m // tm,),
        in_specs=[blk, _row_spec(g, tm), _row_spec(sc, tm), _row_spec(sh, tm)],
        out_specs=[blk, blk],
        out_shape=[out, out],
        compiler_params=_cparams(("parallel",)),
        name="prenorm",
    )(x, g, sc, sh)


def _attn_norm_kernel(of_ref, od_ref, gf_ref, gd_ref, o_ref):
    o_ref[:, :FOX_W] = _rms(of_ref[...], gf_ref[...]).astype(o_ref.dtype)
    o_ref[:, FOX_W:] = _rms(od_ref[...], gd_ref[...]).astype(o_ref.dtype)


def _attn_norm(o_f, o_d, g_f, g_d, tm=256):
    m = o_f.shape[0]
    tm = min(tm, m)
    return pl.pallas_call(
        _attn_norm_kernel,
        grid=(m // tm,),
        in_specs=[pl.BlockSpec((tm, FOX_W), lambda i: (i, 0)), pl.BlockSpec((tm, DSA_W), lambda i: (i, 0)),
                  _row_spec(g_f, tm), _row_spec(g_d, tm)],
        out_specs=pl.BlockSpec((tm, FOX_W + DSA_W), lambda i: (i, 0)),
        out_shape=jax.ShapeDtypeStruct((m, FOX_W + DSA_W), BF16),
        compiler_params=_cparams(("parallel",)),
        name="attn_norm",
    )(o_f, o_d, g_f, g_d)


def _res_pre_kernel(o_ref, x_ref, gt_ref, gpost_ref, gpre_ref, sc_ref, sh_ref, x1_ref, h2_ref):
    x1 = x_ref[...] + gt_ref[...] * _rms(o_ref[...], gpost_ref[...])
    x1_ref[...] = x1
    h2_ref[...] = (_rms(x1, gpre_ref[...]) * (1.0 + sc_ref[...]) + sh_ref[...]).astype(h2_ref.dtype)


def _res_pre(o, x, gt, g_post, g_pre, sc, sh, tm=128):
    m, d = x.shape
    tm = min(tm, m)
    blk =pl.BlockSpec((tm, d), lambda i: (i, 0))
    return pl.pallas_call(
        _res_pre_kernel,
        grid=(m // tm,),
        in_specs=[blk, blk, _row_spec(gt, tm), _row_spec(g_post, tm), _row_spec(g_pre, tm),
                  _row_spec(sc, tm), _row_spec(sh, tm)],
        out_specs=[blk, blk],
        out_shape=[jax.ShapeDtypeStruct((m, d), F32), jax.ShapeDtypeStruct((m, d), BF16)],
        compiler_params=_cparams(("parallel",)),
        name="res_pre",
    )(o, x, gt, g_post, g_pre, sc, sh)


def _res_kernel(f_ref, x_ref, gt_ref, gpost_ref, y_ref):
    y_ref[...] = x_ref[...] + gt_ref[...] * _rms(f_ref[...], gpost_ref[...])


def _res(f, x, gt, g_post, tm=256):
    m, d = x.shape
    tm = min(tm, m)
    blk =pl.BlockSpec((tm, d), lambda i: (i, 0))
    return pl.pallas_call(
        _res_kernel,
        grid=(m // tm,),
        in_specs=[blk, blk, _row_spec(gt, tm), _row_spec(g_post, tm)],
        out_specs=blk,
        out_shape=jax.ShapeDtypeStruct((m, d), F32),
        compiler_params=_cparams(("parallel",)),
        name="res",
    )(f, x, gt, g_post)


def _mm_kernel(a_ref, w_ref, o_ref):
    o_ref[...] = jnp.dot(a_ref[...], w_ref[...], preferred_element_type=F32)


def _matmul(a, w, tm, tn, name):
    m, k = a.shape
    n = w.shape[1]
    tm = min(tm, m)
    return pl.pallas_call(
        _mm_kernel,
        grid=(m // tm, n // tn),
        in_specs=[pl.BlockSpec((tm, k), lambda i, j: (i, 0)), pl.BlockSpec((k, tn), lambda i, j: (0, j))],
        out_specs=pl.BlockSpec((tm, tn), lambda i, j: (i, j)),
        out_shape=jax.ShapeDtypeStruct((m, n), F32),
        compiler_params=_cparams(("parallel", "parallel")),
        name=name,
    )(a, w)


def _mm_nt_kernel(a_ref, wt_ref, o_ref):
    o_ref[...] = lax.dot_general(a_ref[...], wt_ref[...].astype(BF16), _NT, preferred_element_type=F32)


def _matmul_nt(a, w_t, tm, tn, name, n=None):
    m, k = a.shape
    n = w_t.shape[0] if n is None else n
    tm = min(tm, m)
    return pl.pallas_call(
        _mm_nt_kernel,
        grid=(m // tm, n // tn),
        in_specs=[pl.BlockSpec((tm, k), lambda i, j: (i, 0)), pl.BlockSpec((tn, k), lambda i, j: (j, 0))],
        out_specs=pl.BlockSpec((tm, tn), lambda i, j: (i, j)),
        out_shape=jax.ShapeDtypeStruct((m, n), F32),
        compiler_params=_cparams(("parallel", "parallel")),
        name=name,
    )(a, w_t)


def _mm_nt_split_kernel(a_ref, alo_ref, wt_ref, o_ref):
    w = wt_ref[...]
    w_hi = w.astype(BF16)
    w_lo = (w - w_hi.astype(F32)).astype(BF16)
    dot = lambda x, y: lax.dot_general(x, y, _NT, preferred_element_type=F32)
    o_ref[...] = dot(a_ref[...], w_hi) + (dot(a_ref[...], w_lo) + dot(alo_ref[...], w_hi))


def _matmul_nt_split(a, a_lo, w_t, tm, tn, name):
    m, k = a.shape
    n = w_t.shape[0]
    tm = min(tm, m)
    act = pl.BlockSpec((tm, k), lambda i, j: (i, 0))
    return pl.pallas_call(
        _mm_nt_split_kernel,
        grid=(m // tm, n // tn),
        in_specs=[act, act, pl.BlockSpec((tn, k), lambda i, j: (j, 0))],
        out_specs=pl.BlockSpec((tm, tn), lambda i, j: (i, j)),
        out_shape=jax.ShapeDtypeStruct((m, n), F32),
        compiler_params=_cparams(("parallel", "parallel")),
        name=name,
    )(a, a_lo, w_t)


def _mmk_kernel(a_ref, w_ref, o_ref):
    @pl.when(pl.program_id(2) == 0)
    def _():
        o_ref[...] = jnp.zeros_like(o_ref)

    o_ref[...] += jnp.dot(a_ref[...], w_ref[...], preferred_element_type=F32)


def _matmul_ksplit(a, w, tm, tn, tk, name):
    m, k = a.shape
    n = w.shape[1]
    tm = min(tm, m)
    return pl.pallas_call(
        _mmk_kernel,
        grid=(m // tm, n // tn, k // tk),
        in_specs=[pl.BlockSpec((tm, tk), lambda i, j, kk: (i, kk)), pl.BlockSpec((tk, tn), lambda i, j, kk: (kk, j))],
        out_specs=pl.BlockSpec((tm, tn), lambda i, j, kk: (i, j)),
        out_shape=jax.ShapeDtypeStruct((m, n), F32),
        compiler_params=_cparams(("parallel", "parallel", "arbitrary")),
        name=name,
    )(a, w)


def _logf_kernel(zs_ref, bf_ref, logf_ref, cum_ref, carry_ref, *, tm):
    @pl.when(pl.program_id(0) == 0)
    def _():
        carry_ref[...] = jnp.zeros_like(carry_ref)

    x = zs_ref[:, SM_FG:SM_FG + H_FOX] + bf_ref[...]
    lf = jnp.minimum(x, 0.0) - jnp.log1p(jnp.exp(-jnp.abs(x)))
    logf_ref[...] = lf
    row = lax.broadcasted_iota(jnp.int32, (tm, tm), 0)
    col = lax.broadcasted_iota(jnp.int32, (tm, tm), 1)
    tri = (col <= row).astype(F32)
    cum = jnp.dot(tri, lf, preferred_element_type=F32, precision=lax.Precision.HIGHEST) + carry_ref[...]
    cum_ref[...] = cum
    carry_ref[...] = cum[tm - 1:tm, :]


def _logf_cumsum(z, b_f, tm=256):
    m = z.shape[0]
    tm = min(tm, m)
    out = jax.ShapeDtypeStruct((m, H_FOX), F32)
    return pl.pallas_call(
        functools.partial(_logf_kernel, tm=tm),
        grid=(m // tm,),
        in_specs=[pl.BlockSpec((tm, 128), lambda i: (i, Z_SM // 128)), pl.BlockSpec((1, H_FOX), lambda i: (0, 0))],
        out_specs=[pl.BlockSpec((tm, H_FOX), lambda i: (i, 0)), pl.BlockSpec((tm, H_FOX), lambda i: (i, 0))],
        out_shape=[out, out],
        scratch_shapes=[pltpu.VMEM((1, H_FOX), F32)],
        compiler_params=_cparams(("arbitrary",)),
        name="logf_cumsum",
    )(z, b_f.reshape(1, H_FOX))


def _split3(x):
    hi = x.astype(BF16).astype(F32)
    mid = (x - hi).astype(BF16).astype(F32)
    lo = (x - hi - mid).astype(BF16).astype(F32)
    return hi, mid, lo


IDX_K = 3 * IDX_DIM


def _split_operand(x, axis, left):
    hi = x.astype(BF16)
    lo = (x - hi.astype(F32)).astype(BF16)
    return jnp.concatenate([hi, hi, lo] if left else [hi, lo, hi], axis=axis)


def _lane_row(shape, values):
    lane = lax.broadcasted_iota(jnp.int32, shape, len(shape) - 1)
    out = jnp.zeros(shape, F32)
    for i, v in enumerate(values):
        out = jnp.where(lane == i, v, out)
    return out


def _prep_kernel(fk_ref, fv_ref, dk_ref, dv_ref, iq_ref, sm_ref, cum_ref,
                 fka_ref, fvt_ref, dka_ref, dvt_ref, iqs_ref, ikb_ref, sgn_ref, *, tm):
    i = pl.program_id(0)
    ones_rows = jnp.where(lax.broadcasted_iota(jnp.int32, (VT_ROWS - HEAD_DIM, tm), 0) == 0, 1.0, 0.0).astype(BF16)
    nf = cum_ref[...] * (-LOG2E)
    for h in range(H_FOX):
        sl = slice(h * HEAD_DIM, (h + 1) * HEAD_DIM)
        fka_ref[h, :, :HEAD_DIM] = fk_ref[:, sl].astype(BF16)
        fka_ref[h, :, HEAD_DIM:] = _lane_row((tm, HEAD_DIM), _split3(nf[:, h:h + 1])).astype(BF16)
        fvt_ref[h, :HEAD_DIM, :] = fv_ref[:, sl].T.astype(BF16)
        fvt_ref[h, HEAD_DIM:, :] = ones_rows
    kpos = i * tm + lax.broadcasted_iota(jnp.int32, (tm, 1), 0)
    a = (kpos // 64).astype(F32)
    b = (kpos % 64).astype(F32)
    pos_aug = _lane_row((tm, HEAD_DIM), (a, a, a, b, b, b)).astype(BF16)
    for g in range(H_DSA_KV):
        sl = slice(g * HEAD_DIM, (g + 1) * HEAD_DIM)
        dka_ref[g, :, :HEAD_DIM] = dk_ref[:, sl].astype(BF16)
        dka_ref[g, :, HEAD_DIM:] = pos_aug
        dvt_ref[g, :HEAD_DIM, :] = dv_ref[:, sl].T.astype(BF16)
        dvt_ref[g, HEAD_DIM:, :] = ones_rows
    sm = sm_ref[...]
    w = sm[:, SM_IW:SM_IW + H_IDX] * (H_IDX ** -0.5)
    wabs = jnp.abs(w) * (IDX_DIM ** -0.5)
    for h in range(H_IDX):
        iqs_ref[h] = _split_operand(iq_ref[:, h * IDX_DIM:(h + 1) * IDX_DIM] * wabs[:, h:h + 1], 1, True)
    ikb_ref[...] = _split_operand(sm[:, SM_IK:SM_IK + IDX_DIM], 1, False)
    sgn_ref[...] = jnp.where(sm > 0, 1.0, -1.0).T[SM_IW:SM_IW + H_IDX, :]


def _attn_prep(zf, zd, zi, cum, tm=256):
    s_len = zf.shape[0]
    blk = lambda w, off: pl.BlockSpec((tm, w), lambda i: (i, off // w))
    return pl.pallas_call(
        functools.partial(_prep_kernel, tm=tm),
        grid=(s_len // tm,),
        in_specs=[blk(FOX_W, Z_FK), blk(FOX_W, Z_FV), blk(DSA_KV_W, Z_DK), blk(DSA_KV_W, Z_DV), blk(IQ_W, 0),
                  blk(128, ZI_SM), pl.BlockSpec((tm, H_FOX), lambda i: (i, 0))],
        out_specs=[pl.BlockSpec((H_FOX, tm, 2 * HEAD_DIM), lambda i: (0, i, 0)),
                   pl.BlockSpec((H_FOX, VT_ROWS, tm), lambda i: (0, 0, i)),
                   pl.BlockSpec((H_DSA_KV, tm, 2 * HEAD_DIM), lambda i: (0, i, 0)),
                   pl.BlockSpec((H_DSA_KV, VT_ROWS, tm), lambda i: (0, 0, i)),
                   pl.BlockSpec((H_IDX, tm, IDX_K), lambda i: (0, i, 0)),
                   pl.BlockSpec((tm, IDX_K), lambda i: (i, 0)),
                   pl.BlockSpec((H_IDX, tm), lambda i: (0, i))],
        out_shape=[jax.ShapeDtypeStruct((H_FOX, s_len, 2 * HEAD_DIM), BF16),
                   jax.ShapeDtypeStruct((H_FOX, VT_ROWS, s_len), BF16),
                   jax.ShapeDtypeStruct((H_DSA_KV, s_len, 2 * HEAD_DIM), BF16),
                   jax.ShapeDtypeStruct((H_DSA_KV, VT_ROWS, s_len), BF16),
                   jax.ShapeDtypeStruct((H_IDX, s_len, IDX_K), BF16),
                   jax.ShapeDtypeStruct((s_len, IDX_K), BF16),
                   jax.ShapeDtypeStruct((H_IDX, s_len), F32)],
        compiler_params=_cparams(("parallel",)),
        name="attn_prep",
    )(zf, zf, zd, zd, zi, zi, cum)


def _flash_update(s, vt, m_ref, acc_ref):
    _flash_accumulate(*_flash_probs(s, m_ref), vt, acc_ref)


def _flash_probs(s, m_ref):
    m_prev = m_ref[...]
    m_new = jnp.maximum(m_prev, jnp.max(s, axis=0, keepdims=True))
    m_ref[...] = m_new
    return jnp.exp2(m_prev - m_new), jnp.exp2((s - m_new).astype(BF16))


def _flash_accumulate(alpha, p, vt, acc_ref):
    acc_ref[...] = alpha * acc_ref[...] + jnp.dot(vt, p, preferred_element_type=F32)


def _flash_finish(acc):
    return (acc[:HEAD_DIM] / acc[HEAD_DIM:HEAD_DIM + 1]).T


FOX_HEADS_PER_STEP = 8


def _causal_steps(n_q, last_key_block):
    steps = [(qi, ki) for qi in range(n_q) for ki in range(last_key_block(qi) + 1)]
    return (np.asarray([qi for qi, _ in steps], np.int32), np.asarray([ki for _, ki in steps], np.int32))


def _fox_kernel(qi_ref, ki_ref, q_ref, ka_ref, vt_ref, o_ref, qa_ref, m_ref, acc_ref, *, t):
    qi = qi_ref[pl.program_id(1)]
    ki = ki_ref[pl.program_id(1)]
    heads = range(FOX_HEADS_PER_STEP)

    @pl.when(ki == 0)
    def _():
        for h in heads:
            qa_ref[h, :, :HEAD_DIM] = (
                q_ref[:, h * HEAD_DIM:(h + 1) * HEAD_DIM] * (HEAD_DIM ** -0.5 * LOG2E)).astype(BF16)
            qa_ref[h, :, HEAD_DIM:] = _lane_row((t, HEAD_DIM), (1.0, 1.0, 1.0)).astype(BF16)
        m_ref[...] = jnp.full_like(m_ref, NEG)
        acc_ref[...] = jnp.zeros_like(acc_ref)

    def step(diagonal):
        logits = [lax.dot_general(ka_ref[h], qa_ref[h], _NT, preferred_element_type=F32) for h in heads]
        if diagonal:
            visible = (lax.broadcasted_iota(jnp.int32, (t, t), 0) <= lax.broadcasted_iota(jnp.int32, (t, t), 1))
            logits = [jnp.where(visible, s, NEG) for s in logits]
        probs = [_flash_probs(logits[h], m_ref.at[h]) for h in heads]
        for h in heads:
            _flash_accumulate(*probs[h], vt_ref[h], acc_ref.at[h])

    @pl.when(ki < qi)
    def _():
        step(False)

    @pl.when(ki == qi)
    def _():
        step(True)
        for h in heads:
            o_ref[:, h * HEAD_DIM:(h + 1) * HEAD_DIM] = _flash_finish(acc_ref[h])


def _fox_prompt(z, fka, fvt, t=512):
    s_len = z.shape[0]
    n = s_len // t
    hp = FOX_HEADS_PER_STEP
    w = hp * HEAD_DIM
    q_steps, k_steps = _causal_steps(n, lambda qi: qi)
    return pl.pallas_call(
        functools.partial(_fox_kernel, t=t),
        grid_spec=pltpu.PrefetchScalarGridSpec(
            num_scalar_prefetch=2,
            grid=(H_FOX // hp, q_steps.shape[0]),
            in_specs=[pl.BlockSpec((t, w), lambda h, s, qs, ks: (qs[s], Z_FQ // w + h)),
                      pl.BlockSpec((hp, t, 2 * HEAD_DIM), lambda h, s, qs, ks: (h, ks[s], 0)),
                      pl.BlockSpec((hp, VT_ROWS, t), lambda h, s, qs, ks: (h, 0, ks[s]))],
            out_specs=pl.BlockSpec((t, w), lambda h, s, qs, ks: (qs[s], h)),
            scratch_shapes=[pltpu.VMEM((hp, t, 2 * HEAD_DIM), BF16), pltpu.VMEM((hp, 1, t), F32),
                            pltpu.VMEM((hp, VT_ROWS, t), F32)]),
        out_shape=jax.ShapeDtypeStruct((s_len, FOX_W), F32),
        compiler_params=_cparams(("parallel", "arbitrary")),
        name="fox_prompt",
    )(q_steps, k_steps, z, fka, fvt)


def _bisect_threshold(count_ge, n_valid, k_top):
    def cond(carry):
        it, _, cnt = carry
        unsettled = jnp.sum(jnp.where(cnt > k_top, 1, 0))
        return (it < 32) & (unsettled > 0)

    def body(carry):
        it, thr, cnt = carry
        cand = thr + lax.shift_left(jnp.int32(1), jnp.int32(31) - it)
        c = count_ge(cand)
        take = c >= k_top
        return it + 1, jnp.where(take, cand, thr), jnp.where(take, c, cnt)

    init = (jnp.int32(0), jnp.full(n_valid.shape, INT_MIN, jnp.int32), n_valid)
    return lax.while_loop(cond, body, init)[1]


def _idx_kernel(iqs_ref, sgn_ref, ik_ref, bias_ref, key_ref, *, tq, tkc, k_top):
    i = pl.program_id(0)
    s_len = key_ref.shape[0]
    n_chunks = ((i + 1) * tq + tkc - 1) // tkc
    qpos = i * tq + lax.broadcasted_iota(jnp.int32, (tkc, tq), 1)

    def score_chunk(c, carry):
        off = pl.multiple_of(c * tkc, tkc)
        kc = ik_ref[pl.ds(off, tkc), :]
        acc = jnp.zeros((tkc, tq), F32)
        for h in range(H_IDX):
            d = lax.dot_general(kc, iqs_ref[h], _NT, preferred_element_type=F32)
            acc = acc + sgn_ref[h:h + 1, :] * jnp.maximum(d, 0.0)
        kpos = off + lax.broadcasted_iota(jnp.int32, (tkc, tq), 0)
        key_ref[pl.ds(off, tkc), :] = jnp.where(kpos <= qpos, _sort_key(acc), INT_MIN)
        return carry

    lax.fori_loop(0, n_chunks, score_chunk, 0)

    def count_ge(cand):
        def body(c, cnt):
            off = pl.multiple_of(c * tq, tq)
            hit = jnp.where(key_ref[pl.ds(off, tq), :] >= cand, 1, 0)
            return cnt + jnp.sum(hit.reshape(tq // 8, 8, tq), axis=0)

        cnt = lax.fori_loop(0, i + 1, body, jnp.zeros((8, tq), jnp.int32))
        return jnp.sum(cnt, axis=0, keepdims=True)

    n_valid = i * tq + lax.broadcasted_iota(jnp.int32, (1, tq), 1) + 1
    thr = _bisect_threshold(count_ge, n_valid, k_top)
    thr = jnp.maximum(thr, INT_MIN + 1)

    def write_chunk(c, carry):
        off = pl.multiple_of(c * tkc, tkc)
        bias_ref[pl.ds(off, tkc), :] = jnp.where(key_ref[pl.ds(off, tkc), :] >= thr, 0.0, NEG).astype(BF16)
        return carry

    lax.fori_loop(0, n_chunks, write_chunk, 0)

    def fill_chunk(c, carry):
        off = pl.multiple_of(c * tkc, tkc)
        bias_ref[pl.ds(off, tkc), :] = jnp.full((tkc, tq), NEG, BF16)
        return carry

    lax.fori_loop(n_chunks, s_len // tkc, fill_chunk, 0)


def _idx_prompt(iqs, sgn_t, ikb, k_top, tq=256, tkc=256):
    s_len = ikb.shape[0]
    return pl.pallas_call(
        functools.partial(_idx_kernel, tq=tq, tkc=tkc, k_top=k_top),
        grid=(s_len // tq,),
        in_specs=[pl.BlockSpec((H_IDX, tq, IDX_K), lambda i: (0, i, 0)),
                  pl.BlockSpec((H_IDX, tq), lambda i: (0, i)),
                  pl.BlockSpec((s_len, IDX_K), lambda i: (0, 0))],
        out_specs=pl.BlockSpec((s_len, tq), lambda i: (0, i)),
        out_shape=jax.ShapeDtypeStruct((s_len, s_len), BF16),
        scratch_shapes=[pltpu.VMEM((s_len, tq), jnp.int32)],
        compiler_params=_cparams(("parallel",), 56),
        name="idx_prompt",
    )(iqs, sgn_t, ikb)


def _alibi_slope(h):
    return 2.0 ** (-8.0 * (h + 1) / H_DSA)


def _bf16_terms(x):
    out = []
    for _ in range(3):
        t = float(np.float32(x).astype(BF16))
        out.append(t)
        x = x - t
    return out


def _dsa_kernel(qi_ref, ki_ref, q_ref, ka_ref, vt_ref, bias_ref, o_ref, qa_ref, m_ref, acc_ref, *, tq, tk):
    qi = qi_ref[pl.program_id(0)]
    ki = ki_ref[pl.program_id(0)]
    ki_last = (qi * tq + tq - 1) // tk

    @pl.when(ki == 0)
    def _():
        for h in range(H_DSA):
            g, j = divmod(h, DSA_GROUP)
            rows = slice(j * tq, (j + 1) * tq)
            qa_ref[g, rows, :HEAD_DIM] = (
                q_ref[:, h * HEAD_DIM:(h + 1) * HEAD_DIM] * (HEAD_DIM ** -0.5 * LOG2E)).astype(BF16)
            sl = _bf16_terms(_alibi_slope(h) * LOG2E)
            qa_ref[g, rows, HEAD_DIM:] = _lane_row((tq, HEAD_DIM), [64.0 * t for t in sl] + sl).astype(BF16)
        m_ref[...] = jnp.full_like(m_ref, NEG)
        acc_ref[...] = jnp.zeros_like(acc_ref)

    mask = jnp.concatenate([bias_ref[...].astype(F32)] * DSA_GROUP, axis=1)
    groups = range(H_DSA_KV)
    logits = [lax.dot_general(ka_ref[g], qa_ref[g], _NT, preferred_element_type=F32) for g in groups]
    probs = [_flash_probs(logits[g] + mask, m_ref.at[g]) for g in groups]
    for g in groups:
        _flash_accumulate(*probs[g], vt_ref[g], acc_ref.at[g])

    @pl.when(ki == ki_last)
    def _():
        for h in range(H_DSA):
            g, j = divmod(h, DSA_GROUP)
            o_ref[:, h * HEAD_DIM:(h + 1) * HEAD_DIM] = _flash_finish(acc_ref[g, :, j * tq:(j + 1) * tq])


def _dsa_prompt(z, dka, dvt, bias_t, tq=256, tk=512):
    s_len = z.shape[0]
    q_steps, k_steps = _causal_steps(s_len // tq, lambda qi: (qi * tq + tq - 1) // tk)
    return pl.pallas_call(
        functools.partial(_dsa_kernel, tq=tq, tk=tk),
        grid_spec=pltpu.PrefetchScalarGridSpec(
            num_scalar_prefetch=2,
            grid=(q_steps.shape[0],),
            in_specs=[pl.BlockSpec((tq, DSA_W), lambda s, qs, ks: (qs[s], Z_DQ // DSA_W)),
                      pl.BlockSpec((H_DSA_KV, tk, 2 * HEAD_DIM), lambda s, qs, ks: (0, ks[s], 0)),
                      pl.BlockSpec((H_DSA_KV, VT_ROWS, tk), lambda s, qs, ks: (0, 0, ks[s])),
                      pl.BlockSpec((tk, tq), lambda s, qs, ks: (ks[s], qs[s]))],
            out_specs=pl.BlockSpec((tq, DSA_W), lambda s, qs, ks: (qs[s], 0)),
            scratch_shapes=[pltpu.VMEM((H_DSA_KV, DSA_GROUP * tq, 2 * HEAD_DIM), BF16),
                            pltpu.VMEM((H_DSA_KV, 1, DSA_GROUP * tq), F32),
                            pltpu.VMEM((H_DSA_KV, VT_ROWS, DSA_GROUP * tq), F32)]),
        out_shape=jax.ShapeDtypeStruct((s_len, DSA_W), F32),
        compiler_params=_cparams(("arbitrary",)),
        name="dsa_prompt",
    )(q_steps, k_steps, z, dka, dvt, bias_t)


def _dec_idx_kernel(pt_ref, *refs, n_pages, t_new):
    del pt_ref
    ik_refs = refs[:n_pages]
    lf_refs = refs[n_pages:2 * n_pages]
    iq_ref, w_ref, ikn_ref, lfn_ref, key_ref, fneg_ref = refs[2 * n_pages:]

    row = lax.broadcasted_iota(jnp.int32, (PAGE_SIZE, PAGE_SIZE), 0)
    col = lax.broadcasted_iota(jnp.int32, (PAGE_SIZE, PAGE_SIZE), 1)
    tri = (row <= col).astype(F32)
    carry = jnp.zeros((H_FOX, 1), F32)
    for p in range(n_pages + 1):
        lf = lf_refs[p][...] if p < n_pages else lfn_ref[...]
        cum = jnp.dot(lf, tri, preferred_element_type=F32, precision=lax.Precision.HIGHEST) + carry
        fneg_ref[:, p * PAGE_SIZE:(p + 1) * PAGE_SIZE] = -cum
        carry = cum[:, PAGE_SIZE - 1:PAGE_SIZE]

    w = w_ref[...] * (H_IDX ** -0.5)
    iqs = _split_operand(iq_ref[...] * (jnp.abs(w) * (IDX_DIM ** -0.5)), 1, True)
    sgn = jnp.where(w > 0, 1.0, -1.0)
    qrow = lax.broadcasted_iota(jnp.int32, (16, PAGE_SIZE), 0) % t_new
    lane = lax.broadcasted_iota(jnp.int32, (16, PAGE_SIZE), 1)
    for p in range(n_pages + 1):
        kp_t = _split_operand((ik_refs[p] if p < n_pages else ikn_ref)[...], 0, False)
        d = jnp.dot(iqs, kp_t, preferred_element_type=F32)
        sc = jnp.sum((sgn * jnp.maximum(d, 0.0)).reshape(H_IDX, 16, PAGE_SIZE), axis=0)
        key = _sort_key(sc)
        if p == n_pages:
            key = jnp.where((lane <= qrow) & (lane < t_new), key, INT_MIN)
        key_ref[:, p * PAGE_SIZE:(p + 1) * PAGE_SIZE] = key


def _dec_idx(page_table, idx_pages, logf_pages_t, iq16, w16, ik_new_t, logf_new_t, t_new):
    b, n_pages = page_table.shape
    width = (n_pages + 1) * PAGE_SIZE
    page = lambda p: (lambda bi, pt: (pt[bi, p], 0, 0))
    per_b = lambda bi, pt: (bi, 0, 0)
    in_specs = ([pl.BlockSpec((None, IDX_DIM, PAGE_SIZE), page(p)) for p in range(n_pages)]
                + [pl.BlockSpec((None, H_FOX, PAGE_SIZE), page(p)) for p in range(n_pages)]
                + [pl.BlockSpec((None, H_IDX * 16, IDX_DIM), per_b), pl.BlockSpec((None, H_IDX * 16, 1), per_b),
                   pl.BlockSpec((None, IDX_DIM, PAGE_SIZE), per_b), pl.BlockSpec((None, H_FOX, PAGE_SIZE), per_b)])
    return pl.pallas_call(
        functools.partial(_dec_idx_kernel, n_pages=n_pages, t_new=t_new),
        grid_spec=pltpu.PrefetchScalarGridSpec(
            num_scalar_prefetch=1,
            grid=(b,),
            in_specs=in_specs,
            out_specs=[pl.BlockSpec((None, 16, width), per_b), pl.BlockSpec((None, H_FOX, width), per_b)]),
        out_shape=[jax.ShapeDtypeStruct((b, 16, width), jnp.int32), jax.ShapeDtypeStruct((b, H_FOX, width), F32)],
        compiler_params=_cparams(("parallel",)),
        name="dec_idx",
    )(page_table, *([idx_pages] * n_pages), *([logf_pages_t] * n_pages), iq16, w16, ik_new_t, logf_new_t)


def _dec_select_kernel(key_ref, nv_ref, sel_ref, *, k_top):
    def count_ge(cand):
        return jnp.sum(jnp.where(key_ref[...] >= cand, 1, 0), axis=-1, keepdims=True)

    thr = jnp.maximum(_bisect_threshold(count_ge, nv_ref[...], k_top), INT_MIN + 1)
    sel_ref[...] = jnp.where(key_ref[...] >= thr, 0.0, NEG)


def _dec_select(keys, n_valid, k_top, tm=256):
    rows, width = keys.shape
    tm = min(tm, rows)
    return pl.pallas_call(
        functools.partial(_dec_select_kernel, k_top=k_top),
        grid=(rows // tm,),
        in_specs=[pl.BlockSpec((tm, width), lambda i: (i, 0)), pl.BlockSpec((tm, 1), lambda i: (i, 0))],
        out_specs=pl.BlockSpec((tm, width), lambda i: (i, 0)),
        out_shape=jax.ShapeDtypeStruct((rows, width), F32),
        compiler_params=_cparams(("parallel",)),
        name="dec_select",
    )(keys, n_valid)


def _dec_attn_kernel(pt_ref, *refs, n_pages, kp, t_new):
    del pt_ref
    fk_refs, fv_refs, dk_refs, dv_refs = (refs[i * kp:(i + 1) * kp] for i in range(4))
    (fkn_ref, fvn_ref, dkn_ref, dvn_ref, qf_ref, qd_ref, fneg_ref, dbias_ref, fneg_new_ref, dbias_new_ref, slope_ref,
     of_ref, od_ref, mf_ref, lf_ref, accf_ref, md_ref, ld_ref, accd_ref, pairf_ref, paird_ref) = refs[4 * kp:]
    p_id = pl.program_id(1)
    n_steps = n_pages // kp
    rows_q = t_new * H_FOX
    wf = PAGE_SIZE * H_FOX
    wd = PAGE_SIZE * H_DSA_KV

    def pair_mask(n_keys, heads_per_key_head, n_kv):
        qh = lax.broadcasted_iota(jnp.int32, (rows_q, n_keys), 0) % H_FOX
        kh = lax.broadcasted_iota(jnp.int32, (rows_q, n_keys), 1) % n_kv
        return jnp.where(qh // heads_per_key_head == kh, 0.0, NEG)

    @pl.when(p_id == 0)
    def _():
        mf_ref[...] = jnp.full_like(mf_ref, NEG)
        lf_ref[...] = jnp.zeros_like(lf_ref)
        accf_ref[...] = jnp.zeros_like(accf_ref)
        md_ref[...] = jnp.full_like(md_ref, NEG)
        ld_ref[...] = jnp.zeros_like(ld_ref)
        accd_ref[...] = jnp.zeros_like(accd_ref)
        pairf_ref[...] = pair_mask(wf, 1, H_FOX)
        paird_ref[...] = pair_mask(wd, DSA_GROUP, H_DSA_KV)

    def attend(q_ref, k, v, bias, m_ref, l_ref, acc_ref):
        q = (q_ref[...] * (HEAD_DIM ** -0.5)).astype(BF16)
        s = lax.dot_general(q, k, _NT, preferred_element_type=F32) + bias
        m_prev = m_ref[...]
        m_new = jnp.maximum(m_prev, jnp.max(s, axis=-1, keepdims=True))
        alpha = jnp.exp(m_prev - m_new)
        p = jnp.exp(s - m_new)
        l_ref[...] = alpha * l_ref[...] + jnp.sum(p, axis=-1, keepdims=True)
        acc_ref[...] = alpha * acc_ref[...] + jnp.dot(p.astype(BF16), v, preferred_element_type=F32)
        m_ref[...] = m_new

    def sel_rows(db):
        return jnp.concatenate([jnp.broadcast_to(db[q:q + 1], (H_DSA, db.shape[1])) for q in range(t_new)], axis=0)

    def alibi(n_keys, tok0):
        tok = tok0 + lax.broadcasted_iota(jnp.int32, (1, n_keys), 1) // H_DSA_KV
        return slope_ref[...] * tok.astype(F32)

    cat = lambda page_refs: jnp.concatenate([r[...].astype(BF16) for r in page_refs], axis=0)
    bias = jnp.concatenate([pairf_ref[...]] * kp, axis=1) + fneg_ref[...]
    attend(qf_ref, cat(fk_refs), cat(fv_refs), bias, mf_ref, lf_ref, accf_ref)
    bias = (jnp.concatenate([paird_ref[...]] * kp, axis=1) + sel_rows(dbias_ref[...])
            + alibi(kp * wd, (p_id * kp - n_pages) * PAGE_SIZE))
    attend(qd_ref, cat(dk_refs), cat(dv_refs), bias, md_ref, ld_ref, accd_ref)

    @pl.when(p_id == n_steps - 1)
    def _():
        def causal(n_keys, n_kv):
            q = lax.broadcasted_iota(jnp.int32, (rows_q, n_keys), 0) // H_FOX
            tok = lax.broadcasted_iota(jnp.int32, (rows_q, n_keys), 1) // n_kv
            return jnp.where(tok <= q, 0.0, NEG)

        nf = t_new * H_FOX
        bias = pairf_ref[:, :nf] + fneg_new_ref[:, :nf] + causal(nf, H_FOX)
        attend(qf_ref, fkn_ref[...].astype(BF16), fvn_ref[...].astype(BF16), bias, mf_ref, lf_ref, accf_ref)
        nd = t_new * H_DSA_KV
        bias = paird_ref[:, :nd] + sel_rows(dbias_new_ref[:, :nd]) + alibi(nd, 0)
        attend(qd_ref, dkn_ref[...].astype(BF16), dvn_ref[...].astype(BF16), bias, md_ref, ld_ref, accd_ref)
        of_ref[...] = accf_ref[...] / lf_ref[...]
        od_ref[...] = accd_ref[...] / ld_ref[...]


def _dec_attn(page_table, fox_k, fox_v, dsa_k, dsa_v, fk_new, fv_new, dk_new, dv_new, qf, qd, fneg_rows, dbias_rows,
              slopes, t_new):
    b, n_pages = page_table.shape
    kp = 4 if n_pages % 4 == 0 else 1
    n_steps = n_pages // kp
    wf = PAGE_SIZE * H_FOX
    wd = PAGE_SIZE * H_DSA_KV
    rows_q = t_new * H_FOX

    def page(j):
        return lambda bi, p, pt: (pt[bi, p * kp + j], 0, 0)

    per_b = lambda bi, p, pt: (bi, 0, 0)
    past = lambda bi, p, pt: (bi, 0, p)
    new = lambda bi, p, pt: (bi, 0, n_pages)
    in_specs = ([pl.BlockSpec((None, wf, HEAD_DIM), page(j)) for j in range(kp)] * 2
                + [pl.BlockSpec((None, wd, HEAD_DIM), page(j)) for j in range(kp)] * 2
                + [pl.BlockSpec((None, t_new * H_FOX, HEAD_DIM), per_b)] * 2
                + [pl.BlockSpec((None, t_new * H_DSA_KV, HEAD_DIM), per_b)] * 2
                + [pl.BlockSpec((None, rows_q, HEAD_DIM), per_b)] * 2
                + [pl.BlockSpec((None, 1, kp * wf), past), pl.BlockSpec((None, 8, kp * wd), past),
                   pl.BlockSpec((None, 1, wf), new), pl.BlockSpec((None, 8, wd), new),
                   pl.BlockSpec((rows_q, 1), lambda bi, p, pt: (0, 0))])
    out = jax.ShapeDtypeStruct((b, rows_q, HEAD_DIM), F32)
    col = pltpu.VMEM((rows_q, 1), F32)
    acc = pltpu.VMEM((rows_q, HEAD_DIM), F32)
    return pl.pallas_call(
        functools.partial(_dec_attn_kernel, n_pages=n_pages, kp=kp, t_new=t_new),
        grid_spec=pltpu.PrefetchScalarGridSpec(
            num_scalar_prefetch=1,
            grid=(b, n_steps),
            in_specs=in_specs,
            out_specs=[pl.BlockSpec((None, rows_q, HEAD_DIM), per_b)] * 2,
            scratch_shapes=[col, col, acc, col, col, acc, pltpu.VMEM((rows_q, wf), F32),
                            pltpu.VMEM((rows_q, wd), F32)]),
        out_shape=[out, out],
        compiler_params=_cparams(("parallel", "arbitrary")),
        name="dec_attn",
    )(page_table, *([fox_k] * kp), *([fox_v] * kp), *([dsa_k] * kp), *([dsa_v] * kp),
      fk_new, fv_new, dk_new, dv_new, qf, qd, fneg_rows, dbias_rows, fneg_rows, dbias_rows, slopes)


def _ffn_up_kernel(h_ref, hprev_ref, wg_ref, wu_ref, wc_ref, bc_ref, a_ref, tail_ref, *, tm):
    i = pl.program_id(0)
    h = h_ref[...]
    wg = wg_ref[...].astype(BF16)
    g = jnp.dot(h, wg, preferred_element_type=F32)
    u = jnp.dot(h, wu_ref[...].astype(BF16), preferred_element_type=F32)
    gh = jnp.dot(hprev_ref[...], wg, preferred_element_type=F32)
    gh = jnp.where(i > 0, gh, 0.0)
    ext = jnp.concatenate([gh, g], axis=0)
    g1 = pltpu.roll(ext, 1, 0)[16:]
    g2 = pltpu.roll(ext, 2, 0)[16:]
    wc = wc_ref[...]
    conv = bc_ref[...] + wc[0:1] * g2 + wc[1:2] * g1 + wc[2:3] * g
    a_ref[...] = (conv * _sigmoid(conv) * u).astype(a_ref.dtype)
    tail_ref[...] = g[tm - 8:]


def _ffn_up_prompt(h2, w_gate, w_up, w_conv, b_conv, tm=1024, tn=256):
    m, d = h2.shape
    d_ff = w_gate.shape[1]
    tm = min(tm, m)
    return pl.pallas_call(
        functools.partial(_ffn_up_kernel, tm=tm),
        grid=(m // tm, d_ff // tn),
        in_specs=[pl.BlockSpec((tm, d), lambda i, j: (i, 0)),
                  pl.BlockSpec((16, d), lambda i, j: (jnp.maximum(i * (tm // 16) - 1, 0), 0)),
                  pl.BlockSpec((d, tn), lambda i, j: (0, j)), pl.BlockSpec((d, tn), lambda i, j: (0, j)),
                  pl.BlockSpec((CONV_W, tn), lambda i, j: (0, j)), pl.BlockSpec((1, tn), lambda i, j: (0, j))],
        out_specs=[pl.BlockSpec((tm, tn), lambda i, j: (i, j)), pl.BlockSpec((8, tn), lambda i, j: (i, j))],
        out_shape=[jax.ShapeDtypeStruct((m, d_ff), BF16), jax.ShapeDtypeStruct((m // tm * 8, d_ff), F32)],
        compiler_params=_cparams(("parallel", "parallel")),
        name="ffn_up_prompt",
    )(h2, h2, w_gate, w_up, w_conv, b_conv.reshape(1, d_ff))


def _ffn_up_dec_kernel(h_ref, wg_ref, wu_ref, wc_ref, bc_ref, s0_ref, s1_ref, a_ref, g_ref, *, t_new):
    h = h_ref[...]
    g = jnp.dot(h, wg_ref[...].astype(BF16), preferred_element_type=F32)
    u = jnp.dot(h, wu_ref[...].astype(BF16), preferred_element_type=F32)
    g_ref[...] = g
    t = lax.broadcasted_iota(jnp.int32, g.shape, 0) % t_new
    g1 = jnp.where(t >= 1, pltpu.roll(g, 1, 0), 0.0) + s1_ref[...]
    g2 = jnp.where(t >= 2, pltpu.roll(g, 2, 0), 0.0) + s0_ref[...]
    wc = wc_ref[...]
    conv = bc_ref[...] + wc[0:1] * g2 + wc[1:2] * g1 + wc[2:3] * g
    a_ref[...] = (conv * _sigmoid(conv) * u).astype(a_ref.dtype)


def _ffn_up_dec(h2, w_gate, w_up, w_conv, b_conv, tap0, tap1, t_new, tn=256):
    m, d = h2.shape
    d_ff = w_gate.shape[1]
    col = lambda j: (0, j)
    return pl.pallas_call(
        functools.partial(_ffn_up_dec_kernel, t_new=t_new),
        grid=(d_ff // tn,),
        in_specs=[pl.BlockSpec((m, d), lambda j: (0, 0)), pl.BlockSpec((d, tn), col), pl.BlockSpec((d, tn), col),
                  pl.BlockSpec((CONV_W, tn), col), pl.BlockSpec((1, tn), col),
                  pl.BlockSpec((m, tn), col), pl.BlockSpec((m, tn), col)],
        out_specs=[pl.BlockSpec((m, tn), col), pl.BlockSpec((m, tn), col)],
        out_shape=[jax.ShapeDtypeStruct((m, d_ff), BF16), jax.ShapeDtypeStruct((m, d_ff), F32)],
        compiler_params=_cparams(("parallel",)),
        name="ffn_up_dec",
    )(h2, w_gate, w_up, w_conv, b_conv.reshape(1, d_ff), tap0, tap1)


def _split_mod(mod):
    return jnp.split(mod, 6, axis=-1)


def _attn_out_and_ffn_in(x, o_f, o_d, mods, p):
    _, _, gt1, sh2, sc2, _ = mods
    a = _attn_norm(o_f, o_d, p["g_fox_out"], p["g_dsa_out"])
    o = _matmul(a, p["w_out"], 1024, 512, "out_proj")
    return _res_pre(o, x, gt1, p["g_post_attn"], p["g_pre_ffn"], sc2, sh2)


def _ffn_down_and_res(a, x1, mods, p):
    d_ff = a.shape[1]
    f = _matmul_ksplit(a, p["w_down"], 1024, 512, d_ff // 2, "ffn_down")
    return _res(f, x1, mods[5], p["g_post_ffn"])


def _project(x, mods, p):
    sh1, sc1 = mods[0], mods[1]
    h, h_lo = _prenorm(x, p["g_pre_attn"], sc1, sh1)
    return (_matmul_nt(h, p["w_in_t"], 1024, ZF_TN, "in_proj_fox", n=ZF_W),
            _matmul_nt(h, p["w_in_tail_t"], 1024, ZD_TN, "in_proj_rest"),
            _matmul_nt_split(h, h_lo, p["w_idx_t"], 512, ZI_TN, "in_proj_idx"))


def _kv_states(zf, zd, logf, conv):
    return (zf[:, Z_FK:Z_FK + FOX_W], zf[:, Z_FV:Z_FV + FOX_W], logf, zd[:, Z_DK:Z_DK + DSA_KV_W],
            zd[:, Z_DV:Z_DV + DSA_KV_W], zd[:, Z_SM + SM_IK:Z_SM + SM_IK + IDX_DIM], conv)


def _prompt_layer(x, mods, p):
    s_len = x.shape[0]
    zf, zd, zi = _project(x, mods, p)
    logf, cum = _logf_cumsum(zd, p["b_f"])
    fka, fvt, dka, dvt, iqs, ikb, sgn_t = _attn_prep(zf, zd, zi, cum)
    o_f = _fox_prompt(zf, fka, fvt)
    bias_t = _idx_prompt(iqs, sgn_t, ikb, min(TOPK_MAX, s_len // 4))
    o_d = _dsa_prompt(zd, dka, dvt, bias_t)
    x1, h2 = _attn_out_and_ffn_in(x, o_f, o_d, mods, p)
    a, tails = _ffn_up_prompt(h2, p["w_gate"], p["w_up"], p["w_conv"], p["b_conv"])
    y = _ffn_down_and_res(a, x1, mods, p)
    return y, _kv_states(zf, zd, logf, tails[-(CONV_W - 1):])


def _pad_rows(x, rows):
    return jnp.pad(x, ((0, 0), (0, rows - x.shape[1]), (0, 0)))


def _decode_layer(x, mods, p, page_table, caches, state_conv, t_new):
    m = x.shape[0]
    b = m // t_new
    n_pages = page_table.shape[1]
    cache_fox_k, cache_fox_v, cache_fox_logf_t, cache_dsa_k, cache_dsa_v, cache_idx_k = caches
    zf, zd, zi = _project(x, mods, p)
    logf, _ = _logf_cumsum(zd, p["b_f"])
    zf3 = zf.reshape(b, t_new, ZF_W)
    zd3 = zd.reshape(b, t_new, ZD_W)
    zi3 = zi.reshape(b, t_new, ZI_W)

    rep = 16 // t_new
    iq16 = jnp.tile(zi3[:, :, :IQ_W].reshape(b, t_new, H_IDX, IDX_DIM).transpose(0, 2, 1, 3),
                    (1, 1, rep, 1)).reshape(b, H_IDX * 16, IDX_DIM)
    w16 = jnp.tile(zi3[:, :, ZI_SM + SM_IW:ZI_SM + SM_IW + H_IDX].transpose(0, 2, 1), (1, 1, rep)).reshape(
        b, H_IDX * 16, 1)
    ik_new_t = jnp.pad(zi3[:, :, ZI_SM + SM_IK:ZI_SM + SM_IK + IDX_DIM].transpose(0, 2, 1),
                       ((0, 0), (0, 0), (0, PAGE_SIZE - t_new)))
    logf_new_t = jnp.pad(logf.reshape(b, t_new, H_FOX).transpose(0, 2, 1), ((0, 0), (0, 0), (0, PAGE_SIZE - t_new)))
    k_top = min(TOPK_MAX, (n_pages * PAGE_SIZE + t_new) // 4)
    keys, fneg = _dec_idx(page_table, cache_idx_k, cache_fox_logf_t, iq16, w16, ik_new_t, logf_new_t, t_new)
    n_valid = jnp.tile(n_pages * PAGE_SIZE + 1 + jnp.arange(8, dtype=jnp.int32) % t_new, b).reshape(b * 8, 1)
    dbias = _dec_select(keys[:, :8].reshape(b * 8, -1), n_valid, k_top).reshape(b, 8, -1)

    heads = lambda z3, off, w: z3[:, :, off:off + w].reshape(b, t_new * (w // HEAD_DIM), HEAD_DIM)
    fneg_rows = fneg.transpose(0, 2, 1).reshape(b, 1, -1)
    dbias_rows = jnp.repeat(dbias, H_DSA_KV, axis=-1)
    slopes = jnp.tile(2.0 ** (-8.0 * jnp.arange(1, H_DSA + 1, dtype=F32) / H_DSA), t_new).reshape(H_DSA * t_new, 1)
    o_f, o_d = _dec_attn(
        page_table, cache_fox_k, cache_fox_v, cache_dsa_k, cache_dsa_v,
        heads(zf3, Z_FK, FOX_W), heads(zf3, Z_FV, FOX_W), heads(zd3, Z_DK, DSA_KV_W), heads(zd3, Z_DV, DSA_KV_W),
        heads(zf3, Z_FQ, FOX_W), heads(zd3, Z_DQ, DSA_W), fneg_rows, dbias_rows, slopes, t_new)

    x1, h2 = _attn_out_and_ffn_in(x, o_f.reshape(m, FOX_W), o_d.reshape(m, DSA_W), mods, p)
    d_ff = state_conv.shape[-1]
    zero = jnp.zeros((b, 1, d_ff), F32)
    s0, s1 = state_conv[:, 0:1], state_conv[:, 1:2]
    tap0 = jnp.concatenate([s0, s1] + [zero] * (t_new - 2), axis=1).reshape(m, d_ff)
    tap1 = jnp.concatenate([s1] + [zero] * (t_new - 1), axis=1).reshape(m, d_ff)
    a, g = _ffn_up_dec(h2, p["w_gate"], p["w_up"], p["w_conv"], p["b_conv"], tap0, tap1, t_new)
    y = _ffn_down_and_res(a, x1, mods, p)
    conv_new = g.reshape(b, t_new, d_ff)[:, t_new - (CONV_W - 1):]
    return y, _kv_states(zf, zd, logf, conv_new)


def _regroup_w_in_tail(w_in_t):
    n, d = w_in_t.shape
    dsa = w_in_t[ZF_W + H_FOX:IQ_ROW].astype(BF16)
    small = w_in_t[IQ_ROW + IQ_W:].astype(BF16)
    fg = w_in_t[ZF_W:ZF_W + H_FOX].astype(BF16)
    assert IQ_ROW - ZF_W - H_FOX == Z_SM and n - IQ_ROW - IQ_W == SM_IW + H_IDX and SM_FG == SM_IW + H_IDX
    return jnp.concatenate([dsa, small, fg, jnp.zeros((ZD_W - Z_SM - SM_FG - H_FOX, d), BF16)], axis=0)


def kernel(x_prompt, x_sample, c_prompt, c_sample, page_table, cache_fox_k, cache_fox_v, cache_fox_logf, cache_dsa_k, cache_dsa_v, cache_idx_k, state_conv, w_in, b_f, w_out, g_fox_out, g_dsa_out, w_mod, b_mod, g_pre_attn, g_post_attn, g_pre_ffn, g_post_ffn, w_gate, w_up, w_conv, b_conv, w_down):
    depth = w_in.shape[0]
    bp, s_len, d = x_prompt.shape
    bs, t_new, _ = x_sample.shape
    assert bp == 1 and t_new >= CONV_W - 1 and 16 % t_new == 0

    xp = x_prompt.reshape(s_len, d)
    xs = x_sample.reshape(bs * t_new, d)
    n_c = bp + bs
    c_all = jnp.pad(jnp.concatenate([c_prompt, c_sample], axis=0), ((0, -n_c % 8), (0, 0)))
    n_phys = cache_fox_k.shape[1]
    fold = lambda c: c.reshape((depth * n_phys,) + c.shape[2:])
    rows = lambda c: c.reshape(depth * n_phys, PAGE_SIZE * c.shape[3], HEAD_DIM)
    caches = (rows(cache_fox_k), rows(cache_fox_v), fold(cache_fox_logf).transpose(0, 2, 1), rows(cache_dsa_k),
              rows(cache_dsa_v), fold(cache_idx_k).transpose(0, 2, 1))
    p_states, s_states = [], []
    for l in range(depth):
        w_in_t = w_in[l].T
        p = dict(w_in_t=w_in_t, w_in_tail_t=_regroup_w_in_tail(w_in_t),
                 w_idx_t=jnp.pad(w_in_t[IQ_ROW:], ((0, ZI_W - (ZI_SM + SM_FG)), (0, 0))), b_f=b_f[l], w_out=w_out[l].astype(BF16),
                 g_fox_out=g_fox_out[l][None], g_dsa_out=g_dsa_out[l][None],
                 g_pre_attn=g_pre_attn[l][None], g_post_attn=g_post_attn[l][None],
                 g_pre_ffn=g_pre_ffn[l][None], g_post_ffn=g_post_ffn[l][None],
                 w_gate=w_gate[l], w_up=w_up[l], w_conv=w_conv[l], b_conv=b_conv[l],
                 w_down=w_down[l].astype(BF16))
        mod = _modulation(c_all, w_mod[l], b_mod[l])
        mods_p = _split_mod(mod[:bp])
        mods_s = [jnp.repeat(v, t_new, axis=0) for v in _split_mod(mod[bp:n_c])]
        xp, st_p = _prompt_layer(xp, mods_p, p)
        xs, st_s = _decode_layer(xs, mods_s, p, page_table + l * n_phys, caches, state_conv[l], t_new)
        p_states.append(st_p)
        s_states.append(st_s)

    def stack(states, i, shape):
        return jnp.stack([st[i].reshape(shape) for st in states])

    d_ff = state_conv.shape[-1]
    outs = [xp.reshape(bp, s_len, d), xs.reshape(bs, t_new, d)]
    for states, (bb, tt) in ((p_states, (bp, s_len)), (s_states, (bs, t_new))):
        outs += [stack(states, 0, (bb, tt, H_FOX, HEAD_DIM)), stack(states, 1, (bb, tt, H_FOX, HEAD_DIM)),
                 stack(states, 2, (bb, tt, H_FOX)), stack(states, 3, (bb, tt, H_DSA_KV, HEAD_DIM)),
                 stack(states, 4, (bb, tt, H_DSA_KV, HEAD_DIM)), stack(states, 5, (bb, tt, IDX_DIM)),
                 stack(states, 6, (bb, CONV_W - 1, d_ff))]
    return tuple(outs)
```
